```python
import math
import jax
import jax.numpy as jnp
from jax import lax
import numpy as np

D_MODEL = 2048
BATCH = 2
SEQ = 16384
DEPTH = 1
DEC_BATCH = 8
DEC_SEQ = 4096
PAST_LEN = 128

CONV_CH = 1024
CONV_WIDTH = 31
CONV_PAD = (CONV_WIDTH - 1) // 2
SSM_CH = D_MODEL - CONV_CH
SSM_GROUP = 16
SSM_GROUPS = SSM_CH // SSM_GROUP
SSM_STATE = 64
MIX_IN = 2 * CONV_CH + SSM_CH
DT_MIN = 0.001
DT_MAX = 0.1
N_EXPERTS = 256
TOP_K = 8
N_EXPERT_GROUPS = 8
TOPK_GROUPS = 4
EXPERT_FF = 512
SHARED_FF = 512
ROUTED_SCALE = 2.5
MOE_BLOCK = 128
N_ADA = 6
LN_EPS = 1e-5
DEEPNORM_ALPHA = (2 * DEPTH) ** 0.25
DEEPNORM_BETA = (8 * DEPTH) ** -0.25

kernel_name = "hymba_conformer_s5_moe_encoder"


def _layer_norm(x, g, b):
    xf = x.astype(jnp.float32)
    mu = jnp.mean(xf, axis=-1, keepdims=True)
    var = jnp.mean(jnp.square(xf - mu), axis=-1, keepdims=True)
    y = (xf - mu) * lax.rsqrt(var + LN_EPS) * g.astype(jnp.float32) + b.astype(jnp.float32)
    return y.astype(x.dtype)


def _conv_module(a, gate, conv_w, conv_b, ln_g, ln_b):
    u = a * jax.nn.sigmoid(gate)
    k = conv_w[:, None, :].astype(u.dtype)
    v = lax.conv_general_dilated(u, k, window_strides=(1,), padding=[(CONV_PAD, CONV_PAD)],
                                 dimension_numbers=("NWC", "WIO", "NWC"),
                                 feature_group_count=CONV_CH) + conv_b
    v = _layer_norm(v, ln_g, ln_b)
    return jax.nn.silu(v)


def _zoh(a_re, a_im, log_dt, b_re, b_im):
    lam = lax.complex(a_re.astype(jnp.float32), a_im.astype(jnp.float32))
    dt = jnp.exp(log_dt.astype(jnp.float32))[:, None]
    lam_bar = jnp.exp(lam * dt)
    b = lax.complex(b_re.astype(jnp.float32), b_im.astype(jnp.float32))
    b_bar = ((lam_bar - 1.0) / lam)[..., None] * b
    return lam_bar, b_bar


def _ssm_binop(e1, e2):
    a1, b1 = e1
    a2, b2 = e2
    return a1 * a2, a2 * b1 + b2


def _s5_direction(u, lam_bar, b_bar, c_re, c_im, reverse):
    bu = lax.complex(jnp.einsum("blgh,gph->blgp", u, jnp.real(b_bar)),
                     jnp.einsum("blgh,gph->blgp", u, jnp.imag(b_bar)))
    a = jnp.broadcast_to(lam_bar, bu.shape)
    _, s = lax.associative_scan(_ssm_binop, (a, bu), axis=1, reverse=reverse)
    return (jnp.einsum("blgp,ghp->blgh", jnp.real(s), c_re.astype(jnp.float32))
            - jnp.einsum("blgp,ghp->blgh", jnp.imag(s), c_im.astype(jnp.float32)))


def _s5_module(x_ssm, b_re, b_im, a_re_f, a_im_f, log_dt_f, a_re_b, a_im_b, log_dt_b,
               c_re_f, c_im_f, c_re_b, c_im_b, d_skip, glu_w, glu_b):
    bsz, seq, _ = x_ssm.shape
    u = x_ssm.astype(jnp.float32).reshape(bsz, seq, SSM_GROUPS, SSM_GROUP)
    lam_f, bbar_f = _zoh(a_re_f, a_im_f, log_dt_f, b_re, b_im)
    lam_b, bbar_b = _zoh(a_re_b, a_im_b, log_dt_b, b_re, b_im)
    y = (_s5_direction(u, lam_f, bbar_f, c_re_f, c_im_f, False)
         + _s5_direction(u, lam_b, bbar_b, c_re_b, c_im_b, True)
         + d_skip.astype(jnp.float32).reshape(SSM_GROUPS, SSM_GROUP) * u)
    y = y.reshape(bsz, seq, SSM_CH).astype(x_ssm.dtype)
    g = jax.nn.gelu(y)
    return g * jax.nn.sigmoid(jnp.dot(g, glu_w) + glu_b)


def _route(h, router_w, router_bias):
    t = h.shape[0]
    scores = jax.nn.sigmoid(jnp.dot(h.astype(jnp.float32), router_w.astype(jnp.float32)))
    biased = scores + router_bias.astype(jnp.float32)
    grouped = biased.reshape(t, N_EXPERT_GROUPS, N_EXPERTS // N_EXPERT_GROUPS)
    group_score = jnp.sum(lax.top_k(grouped, 2)[0], axis=-1)
    _, g_idx = lax.top_k(group_score, TOPK_GROUPS)
    g_mask = jnp.sum(jax.nn.one_hot(g_idx, N_EXPERT_GROUPS, dtype=jnp.float32), axis=1) > 0
    e_mask = jnp.repeat(g_mask, N_EXPERTS // N_EXPERT_GROUPS, axis=1)
    masked = jnp.where(e_mask, biased, -jnp.inf)
    _, e_idx = lax.top_k(masked, TOP_K)
    w = jnp.take_along_axis(scores, e_idx, axis=1)
    w = w / jnp.sum(w, axis=-1, keepdims=True) * ROUTED_SCALE
    return e_idx, w


def _routed_experts(h, e_idx, e_w, w_gate, w_up, w_down):
    t, d = h.shape
    tk = t * TOP_K
    n_blocks = -(-tk // MOE_BLOCK) + N_EXPERTS
    n_rows = n_blocks * MOE_BLOCK
    flat_e = e_idx.reshape(tk)
    order = jnp.argsort(flat_e)
    sorted_e = flat_e[order]
    sorted_tok = (order // TOP_K).astype(jnp.int32)
    sorted_w = e_w.reshape(tk)[order].astype(h.dtype)
    counts = jnp.bincount(flat_e, length=N_EXPERTS)
    padded = (counts + MOE_BLOCK - 1) // MOE_BLOCK * MOE_BLOCK
    pad_end = jnp.cumsum(padded)
    pad_start = pad_end - padded
    start = jnp.cumsum(counts) - counts
    dest = pad_start[sorted_e] + (jnp.arange(tk) - start[sorted_e])
    row_tok = jnp.full((n_rows,), t, jnp.int32).at[dest].set(sorted_tok)
    row_w = jnp.zeros((n_rows,), h.dtype).at[dest].set(sorted_w)
    block_e = jnp.minimum(jnp.searchsorted(pad_end, jnp.arange(n_blocks) * MOE_BLOCK, side="right"),
                          N_EXPERTS - 1)
    h_pad = jnp.concatenate([h, jnp.zeros((1, d), h.dtype)], axis=0)

    def body(y, blk):
        rows = lax.dynamic_slice_in_dim(row_tok, blk * MOE_BLOCK, MOE_BLOCK)
        wts = lax.dynamic_slice_in_dim(row_w, blk * MOE_BLOCK, MOE_BLOCK)
        e = block_e[blk]
        xb = h_pad[rows]
        act = jax.nn.silu(jnp.dot(xb, w_gate[e])) * jnp.dot(xb, w_up[e])
        return y.at[rows].add(jnp.dot(act, w_down[e]) * wts[:, None]), None

    y, _ = lax.scan(body, jnp.zeros((t + 1, d), h.dtype), jnp.arange(n_blocks, dtype=jnp.int32))
    return y[:t]


def _encoder_layer(x, c, w_ada, b_ada, w_in, b_in, conv_w, conv_b, conv_ln_g, conv_ln_b,
                   ssm_b_re, ssm_b_im, ssm_a_re_f, ssm_a_im_f, ssm_log_dt_f,
                   ssm_a_re_b, ssm_a_im_b, ssm_log_dt_b, ssm_c_re_f, ssm_c_im_f,
                   ssm_c_re_b, ssm_c_im_b, ssm_d, ssm_glu_w, ssm_glu_b, w_out, b_out,
                   ln1_g, ln1_b, router_w, router_bias, exp_w_gate, exp_w_up, exp_w_down,
                   sh_w_gate, sh_w_up, sh_w_down, ln2_g, ln2_b):
    bsz, seq, _ = x.shape
    ada = jnp.dot(jax.nn.silu(c), w_ada) + b_ada
    shift1, scale1, gate1, shift2, scale2, gate2 = jnp.split(ada[:, None, :], N_ADA, axis=-1)
    h = x * (1.0 + scale1) + shift1
    proj = jnp.dot(h, w_in) + b_in
    conv_a, conv_g, ssm_in = jnp.split(proj, [CONV_CH, 2 * CONV_CH], axis=-1)
    conv_out = _conv_module(conv_a, conv_g, conv_w, conv_b, conv_ln_g, conv_ln_b)
    ssm_out = _s5_module(ssm_in, ssm_b_re, ssm_b_im, ssm_a_re_f, ssm_a_im_f, ssm_log_dt_f,
                         ssm_a_re_b, ssm_a_im_b, ssm_log_dt_b, ssm_c_re_f, ssm_c_im_f,
                         ssm_c_re_b, ssm_c_im_b, ssm_d, ssm_glu_w, ssm_glu_b)
    mix = jnp.dot(jnp.concatenate([conv_out, ssm_out], axis=-1), w_out) + b_out
    x = _layer_norm(DEEPNORM_ALPHA * x + gate1 * mix, ln1_g, ln1_b)
    h2 = x * (1.0 + scale2) + shift2
    tokens = h2.reshape(bsz * seq, D_MODEL)
    e_idx, e_w = _route(tokens, router_w, router_bias)
    routed = _routed_experts(tokens, e_idx, e_w, exp_w_gate, exp_w_up, exp_w_down)
    shared = jnp.dot(jax.nn.silu(jnp.dot(tokens, sh_w_gate)) * jnp.dot(tokens, sh_w_up), sh_w_down)
    ffn = (routed + shared).reshape(bsz, seq, D_MODEL)
    return _layer_norm(DEEPNORM_ALPHA * x + gate2 * ffn, ln2_g, ln2_b)


def setup_inputs(seed: int = 0) -> dict:
    key = jax.random.key(seed)
    ks = iter(jax.random.split(key, 48))

    def nrm(shape, std):
        return jax.random.normal(next(ks), shape, jnp.float32) * std

    def gain(shape):
        return 1.0 + nrm(shape, 0.01)

    L_, D_, G_, P_, H_ = DEPTH, D_MODEL, SSM_GROUPS, SSM_STATE, SSM_GROUP
    n_idx = jnp.arange(P_, dtype=jnp.float32)
    xavier_out = DEEPNORM_BETA * math.sqrt(2.0 / (D_ + D_))
    xavier_exp = DEEPNORM_BETA * math.sqrt(2.0 / (EXPERT_FF + D_))
    xavier_sh = DEEPNORM_BETA * math.sqrt(2.0 / (SHARED_FF + D_))
    return {
        "x_prompt": nrm((BATCH, SEQ, D_), 1.0),
        "x_sample": nrm((DEC_BATCH, DEC_SEQ, D_), 1.0),
        "c_prompt": nrm((BATCH, D_), 1.0),
        "c_sample": nrm((DEC_BATCH, D_), 1.0),
        "w_ada": nrm((L_, D_, N_ADA * D_), 0.5 * D_ ** -0.5),
        "b_ada": nrm((L_, N_ADA * D_), 0.02),
        "w_in": nrm((L_, D_, MIX_IN), D_ ** -0.5),
        "b_in": nrm((L_, MIX_IN), 0.01),
        "conv_w": nrm((L_, CONV_WIDTH, CONV_CH), CONV_WIDTH ** -0.5),
        "conv_b": nrm((L_, CONV_CH), 0.01),
        "conv_ln_g": gain((L_, CONV_CH)),
        "conv_ln_b": nrm((L_, CONV_CH), 0.01),
        "ssm_b_re": nrm((L_, G_, P_, H_), (2.0 * H_) ** -0.5),
        "ssm_b_im": nrm((L_, G_, P_, H_), (2.0 * H_) ** -0.5),
        "ssm_a_re_f": -0.5 + nrm((L_, G_, P_), 0.01),
        "ssm_a_im_f": math.pi * n_idx + nrm((L_, G_, P_), 0.01),
        "ssm_log_dt_f": jax.random.uniform(next(ks), (L_, G_), jnp.float32, math.log(DT_MIN), math.log(DT_MAX)),
        "ssm_a_re_b": -0.5 + nrm((L_, G_, P_), 0.01),
        "ssm_a_im_b": math.pi * n_idx + nrm((L_, G_, P_), 0.01),
        "ssm_log_dt_b": jax.random.uniform(next(ks), (L_, G_), jnp.float32, math.log(DT_MIN), math.log(DT_MAX)),
        "ssm_c_re_f": nrm((L_, G_, H_, P_), (2.0 * P_) ** -0.5),
        "ssm_c_im_f": nrm((L_, G_, H_, P_), (2.0 * P_) ** -0.5),
        "ssm_c_re_b": nrm((L_, G_, H_, P_), (2.0 * P_) ** -0.5),
        "ssm_c_im_b": nrm((L_, G_, H_, P_), (2.0 * P_) ** -0.5),
        "ssm_d": nrm((L_, SSM_CH), 1.0),
        "ssm_glu_w": nrm((L_, SSM_CH, SSM_CH), SSM_CH ** -0.5),
        "ssm_glu_b": nrm((L_, SSM_CH), 0.01),
        "w_out": nrm((L_, D_, D_), xavier_out),
        "b_out": nrm((L_, D_), 0.01),
        "ln1_g": gain((L_, D_)),
        "ln1_b": nrm((L_, D_), 0.01),
        "router_w": nrm((L_, D_, N_EXPERTS), D_ ** -0.5),
        "router_bias": nrm((L_, N_EXPERTS), 0.01),
        "exp_w_gate": nrm((L_, N_EXPERTS, D_, EXPERT_FF), D_ ** -0.5),
        "exp_w_up": nrm((L_, N_EXPERTS, D_, EXPERT_FF), D_ ** -0.5),
        "exp_w_down": nrm((L_, N_EXPERTS, EXPERT_FF, D_), xavier_exp),
        "sh_w_gate": nrm((L_, D_, SHARED_FF), D_ ** -0.5),
        "sh_w_up": nrm((L_, D_, SHARED_FF), D_ ** -0.5),
        "sh_w_down": nrm((L_, SHARED_FF, D_), xavier_sh),
        "ln2_g": gain((L_, D_)),
        "ln2_b": nrm((L_, D_), 0.01),
    }


def reference(x_prompt, x_sample, c_prompt, c_sample, w_ada, b_ada, w_in, b_in, conv_w, conv_b,
              conv_ln_g, conv_ln_b, ssm_b_re, ssm_b_im, ssm_a_re_f, ssm_a_im_f, ssm_log_dt_f,
              ssm_a_re_b, ssm_a_im_b, ssm_log_dt_b, ssm_c_re_f, ssm_c_im_f, ssm_c_re_b, ssm_c_im_b,
              ssm_d, ssm_glu_w, ssm_glu_b, w_out, b_out, ln1_g, ln1_b, router_w, router_bias,
              exp_w_gate, exp_w_up, exp_w_down, sh_w_gate, sh_w_up, sh_w_down, ln2_g, ln2_b):
    y_prompt = x_prompt
    y_sample = x_sample
    for l in range(DEPTH):
        layer_params = (w_ada[l], b_ada[l], w_in[l], b_in[l], conv_w[l], conv_b[l], conv_ln_g[l],
                        conv_ln_b[l], ssm_b_re[l], ssm_b_im[l], ssm_a_re_f[l], ssm_a_im_f[l],
                        ssm_log_dt_f[l], ssm_a_re_b[l], ssm_a_im_b[l], ssm_log_dt_b[l],
                        ssm_c_re_f[l], ssm_c_im_f[l], ssm_c_re_b[l], ssm_c_im_b[l], ssm_d[l],
                        ssm_glu_w[l], ssm_glu_b[l], w_out[l], b_out[l], ln1_g[l], ln1_b[l],
                        router_w[l], router_bias[l], exp_w_gate[l], exp_w_up[l], exp_w_down[l],
                        sh_w_gate[l], sh_w_up[l], sh_w_down[l], ln2_g[l], ln2_b[l])
        y_prompt = _encoder_layer(y_prompt, c_prompt, *layer_params)
        y_sample = _encoder_layer(y_sample, c_sample, *layer_params)
    return (y_prompt, y_sample)
```

```python
import functools
import math

import numpy as np
import jax
import jax.numpy as jnp
from jax import lax
from jax.experimental import pallas as pl
from jax.experimental.pallas import tpu as pltpu

F32 = jnp.float32
BF16 = jnp.bfloat16
U32 = jnp.uint32
I32 = jnp.int32

LN_EPS = 1e-5
N_ADA = 6
TOP_K = 8
N_EXPERT_GROUPS = 8
TOPK_GROUPS = 4
ROUTED_SCALE = 2.5

NCHAIN = 8
LANES = 128
MXU_DIM = 256
VMEM_LIMIT = 56 * 1024 * 1024

TILES = dict(inproj=64, conv=128, s5=128, glu=1024, outproj=32, route=512, expert=512, dispatch=512,
             combine=256)


def _cparams(sem):
    return pltpu.CompilerParams(dimension_semantics=sem, vmem_limit_bytes=VMEM_LIMIT)


def _dot(a, b):
    return jnp.dot(a, b, preferred_element_type=F32)


def _sigmoid(x):
    return jax.nn.sigmoid(x)


def _layer_norm(v, g, b):
    mu = jnp.mean(v, axis=-1, keepdims=True)
    d = v - mu
    var = jnp.mean(d * d, axis=-1, keepdims=True)
    return d * lax.rsqrt(var + LN_EPS) * g + b


def _ada_kernel(c_ref, w_ref, b_ref, o_ref):
    c = c_ref[...]
    s = (c * _sigmoid(c)).astype(BF16)
    o_ref[...] = _dot(s, w_ref[...].astype(BF16)) + b_ref[...]


def _ada(c_all, w_ada, b_ada):
    rows, d = c_all.shape
    n = w_ada.shape[1]
    tn = min(n, 2048)
    return pl.pallas_call(
        _ada_kernel,
        grid=(n // tn,),
        in_specs=[pl.BlockSpec((rows, d), lambda j: (0, 0)),
                  pl.BlockSpec((d, tn), lambda j: (0, j)),
                  pl.BlockSpec((1, tn), lambda j: (0, j))],
        out_specs=pl.BlockSpec((rows, tn), lambda j: (0, j)),
        out_shape=jax.ShapeDtypeStruct((rows, n), F32),
        compiler_params=_cparams(("arbitrary",)),
        name="ada",
    )(c_all, w_ada, b_ada.reshape(1, n))


def _x_specs(tt, d, nt):
    xp = pl.BlockSpec((NCHAIN, tt, d), lambda g, t: (0, jnp.where(g == 0, t, nt - 1), 0))
    xs = pl.BlockSpec((NCHAIN, tt, d), lambda g, t: (0, jnp.where(g == 1, t, 0), 0))
    return xp, xs


def _mod_spec(d):
    return pl.BlockSpec((1, NCHAIN, 1, d), lambda g, t: (g, 0, 0, 0))


def _inproj_kernel(xp_ref, xs_ref, sc_ref, sh_ref, wa_ref, wg_ref, ws_ref, ba_ref, bg_ref, bs_ref,
                   u_ref, s_ref):
    g = pl.program_id(0)
    x = jnp.where(g == 0, xp_ref[...], xs_ref[...])
    h = x * (1.0 + sc_ref[0]) + sh_ref[0]
    tt = h.shape[1]
    ht = pltpu.einshape("ctd->tcd", h).reshape(tt * NCHAIN, h.shape[2]).astype(BF16)
    a = _dot(ht, wa_ref[...]) + ba_ref[...]
    gt = _dot(ht, wg_ref[...]) + bg_ref[...]
    u = a * _sigmoid(gt)
    s = _dot(ht, ws_ref[...]) + bs_ref[...]
    u_ref[0] = u.reshape(tt, NCHAIN, u.shape[-1])
    s_ref[0] = s.reshape(tt, NCHAIN, s.shape[-1])


def _inproj(xp, xs, scale1, shift1, wa, wg, ws, ba, bg, bs, tt):
    _, ls, d = xp.shape
    cc, cs = wa.shape[1], ws.shape[1]
    nt = ls // tt
    xp_spec, xs_spec = _x_specs(tt, d, nt)
    const2 = lambda g, t: (0, 0)
    return pl.pallas_call(
        _inproj_kernel,
        grid=(2, nt),
        in_specs=[xp_spec, xs_spec, _mod_spec(d), _mod_spec(d),
                  pl.BlockSpec((d, cc), const2), pl.BlockSpec((d, cc), const2), pl.BlockSpec((d, cs), const2),
                  pl.BlockSpec((1, cc), const2), pl.BlockSpec((1, cc), const2), pl.BlockSpec((1, cs), const2)],
        out_specs=[pl.BlockSpec((1, tt, NCHAIN, cc), lambda g, t: (g, t, 0, 0)),
                   pl.BlockSpec((1, tt, NCHAIN, cs), lambda g, t: (g, t, 0, 0))],
        out_shape=[jax.ShapeDtypeStruct((2, ls, NCHAIN, cc), F32),
                   jax.ShapeDtypeStruct((2, ls, NCHAIN, cs), F32)],
        compiler_params=_cparams(("arbitrary", "arbitrary")),
        name="inproj",
    )(xp, xs, scale1, shift1, wa, wg, ws, ba, bg, bs)


CONV_HALO = 16


def _conv_kernel(u_ref, up_ref, un_ref, pok_ref, nok_ref, w_ref, cb_ref, g_ref, b_ref, o_ref,
                 buf_ref, acc_ref, *, width):
    t = pl.program_id(1)
    nt = pl.num_programs(1)
    tt = u_ref.shape[1]
    pad = (width - 1) // 2
    prev = up_ref[0]
    prev_wrapped = pltpu.roll(prev, shift=1, axis=1) * pok_ref[0]
    buf_ref[0:CONV_HALO] = jnp.where(t == 0, prev_wrapped, prev)
    buf_ref[CONV_HALO:CONV_HALO + tt] = u_ref[0]
    nxt = un_ref[0]
    next_wrapped = pltpu.roll(nxt, shift=NCHAIN - 1, axis=1) * nok_ref[0]
    buf_ref[CONV_HALO + tt:2 * CONV_HALO + tt] = jnp.where(t == nt - 1, next_wrapped, nxt)

    base = CONV_HALO - pad

    def body(j, carry):
        acc = cb_ref[...]
        for k in range(width):
            acc = acc + buf_ref[j + base + k] * w_ref[k]
        acc_ref[j] = acc
        return carry

    lax.fori_loop(0, tt, body, 0)
    v = _layer_norm(acc_ref[...], g_ref[...], b_ref[...])
    v = v * _sigmoid(v)
    o_ref[0] = v.reshape(tt * NCHAIN, v.shape[-1]).astype(o_ref.dtype)


def _conv(u, prev_ok, next_ok, conv_w, conv_b, ln_g, ln_b, tt):
    _, ls, _, cc = u.shape
    width = conv_w.shape[0]
    assert (width - 1) // 2 <= CONV_HALO and tt % CONV_HALO == 0
    nt = ls // tt
    hb = tt // CONV_HALO
    nh = ls // CONV_HALO
    wb = jnp.broadcast_to(conv_w[:, None, :], (width, NCHAIN, cc))
    row = lambda v: jnp.broadcast_to(v[None, :], (NCHAIN, cc))
    const2 = lambda g, t: (0, 0)
    return pl.pallas_call(
        functools.partial(_conv_kernel, width=width),
        grid=(2, nt),
        in_specs=[pl.BlockSpec((1, tt, NCHAIN, cc), lambda g, t: (g, t, 0, 0)),
                  pl.BlockSpec((1, CONV_HALO, NCHAIN, cc),
                               lambda g, t: (g, jnp.where(t == 0, nh - 1, t * hb - 1), 0, 0)),
                  pl.BlockSpec((1, CONV_HALO, NCHAIN, cc),
                               lambda g, t: (g, jnp.where(t == nt - 1, 0, (t + 1) * hb), 0, 0)),
                  pl.BlockSpec((1, NCHAIN, 1), lambda g, t: (g, 0, 0)),
                  pl.BlockSpec((1, NCHAIN, 1), lambda g, t: (g, 0, 0)),
                  pl.BlockSpec((width, NCHAIN, cc), lambda g, t: (0, 0, 0)),
                  pl.BlockSpec((NCHAIN, cc), const2), pl.BlockSpec((NCHAIN, cc), const2),
                  pl.BlockSpec((NCHAIN, cc), const2)],
        out_specs=pl.BlockSpec((1, tt * NCHAIN, cc), lambda g, t: (g, t, 0)),
        out_shape=jax.ShapeDtypeStruct((2, ls * NCHAIN, cc), BF16),
        scratch_shapes=[pltpu.VMEM((tt + 2 * CONV_HALO, NCHAIN, cc), F32),
                        pltpu.VMEM((tt, NCHAIN, cc), F32)],
        compiler_params=_cparams(("arbitrary", "arbitrary")),
        name="conv",
    )(u, u, u, prev_ok, next_ok, wb, row(conv_b), row(ln_g), row(ln_b))


S5_LANE_BLOCK = 512
S5_UNROLL = 4


def _s5_kernel(u_ref, wb_ref, wc_ref, are_ref, aim_ref, s0_ref, *rest, reverse, emit_y):
    if emit_y:
        y_ref, sfin_ref, bu_ref, st_ref, carry_ref = rest
    else:
        sfin_ref, bu_ref, carry_ref = rest
        st_ref = None
    i = pl.program_id(1)
    tt = u_ref.shape[1]
    nk, ks, sw2 = wb_ref.shape
    sw = sw2 // 2

    @pl.when(i == 0)
    def _():
        carry_ref[...] = s0_ref[0]

    u2 = u_ref[0].reshape(tt * NCHAIN, u_ref.shape[3]).astype(BF16)
    for kc in range(nk):
        bu_ref[...] = _dot(u2[:, kc * ks:(kc + 1) * ks], wb_ref[kc]).reshape(tt, NCHAIN, sw2)
        for lo in range(0, sw, S5_LANE_BLOCK):
            lb = min(S5_LANE_BLOCK, sw - lo)
            re_sl = slice(lo, lo + lb)
            im_sl = slice(sw + lo, sw + lo + lb)
            ar = are_ref[kc, :, re_sl]
            ai = aim_ref[kc, :, re_sl]

            def step(j, carry, re_sl=re_sl, im_sl=im_sl, ar=ar, ai=ai):
                sre, sim = carry
                for q in range(S5_UNROLL):
                    jj = j * S5_UNROLL + q
                    tloc = tt - 1 - jj if reverse else jj
                    nre = ar * sre - ai * sim + bu_ref[tloc, :, re_sl]
                    nim = ar * sim + ai * sre + bu_ref[tloc, :, im_sl]
                    if emit_y:
                        st_ref[tloc, :, re_sl] = nre
                        st_ref[tloc, :, im_sl] = nim
                    sre, sim = nre, nim
                return sre, sim

            sre, sim = lax.fori_loop(0, tt // S5_UNROLL, step,
                                     (carry_ref[kc, :, re_sl], carry_ref[kc, :, im_sl]))
            carry_ref[kc, :, re_sl] = sre
            carry_ref[kc, :, im_sl] = sim
        if emit_y:
            st = st_ref[...].reshape(tt * NCHAIN, sw2).astype(BF16)
            y_ref[0, :, :, kc * ks:(kc + 1) * ks] = _dot(st, wc_ref[kc]).reshape(tt, NCHAIN, ks)
    sfin_ref[0] = carry_ref[...]


def _s5_pass(s_in, ngroups, wb, wc, are, aim, s0, tt, reverse, emit_y):
    _, ls, _, cs = s_in.shape
    nk, ks, sw2 = wb.shape
    assert tt % S5_UNROLL == 0
    nt = ls // tt
    tmap = (lambda i: nt - 1 - i) if reverse else (lambda i: i)
    c3 = lambda g, i: (0, 0, 0)
    state_spec = pl.BlockSpec((1, nk, NCHAIN, sw2), lambda g, i: (g, 0, 0, 0))
    out_specs = [state_spec]
    out_shape = [jax.ShapeDtypeStruct((ngroups, nk, NCHAIN, sw2), F32)]
    scratch = [pltpu.VMEM((tt, NCHAIN, sw2), F32)]
    if emit_y:
        out_specs.insert(0, pl.BlockSpec((1, tt, NCHAIN, cs), lambda g, i: (g, tmap(i), 0, 0)))
        out_shape.insert(0, jax.ShapeDtypeStruct((ngroups, ls, NCHAIN, cs), F32))
        scratch.append(pltpu.VMEM((tt, NCHAIN, sw2), F32))
    scratch.append(pltpu.VMEM((nk, NCHAIN, sw2), F32))
    res = pl.pallas_call(
        functools.partial(_s5_kernel, reverse=reverse, emit_y=emit_y),
        grid=(ngroups, nt),
        in_specs=[pl.BlockSpec((1, tt, NCHAIN, cs), lambda g, i: (g, tmap(i), 0, 0)),
                  pl.BlockSpec((nk, ks, sw2), c3), pl.BlockSpec((nk, sw2, ks), c3),
                  pl.BlockSpec((nk, NCHAIN, sw2 // 2), c3), pl.BlockSpec((nk, NCHAIN, sw2 // 2), c3),
                  state_spec],
        out_specs=out_specs,
        out_shape=out_shape,
        scratch_shapes=scratch,
        compiler_params=_cparams(("arbitrary", "arbitrary")),
        name="s5_" + ("bwd" if reverse else "fwd") + ("" if emit_y else "_state"),
    )(s_in, wb, wc, are, aim, s0)
    return (res[0], res[1]) if emit_y else (None, res[0])


def _cmul(are, aim, bre, bim):
    return are * bre - aim * bim, are * bim + aim * bre


def _cpow(re, im, n):
    rre, rim = jnp.ones_like(re), jnp.zeros_like(im)
    while n:
        if n & 1:
            rre, rim = _cmul(rre, rim, re, im)
        re, im = _cmul(re, im, re, im)
        n >>= 1
    return rre, rim


def _s5_params(b_re, b_im, a_re, a_im, log_dt, c_re, c_im):
    ng, ns, nh = b_re.shape
    cs = ng * nh
    ks = min(MXU_DIM, cs)
    gps = ks // nh
    nk = cs // ks
    sw = gps * ns
    a_re, a_im = a_re.astype(F32), a_im.astype(F32)
    dt = jnp.exp(log_dt.astype(F32))[:, None]
    mag = jnp.exp(a_re * dt)
    lre, lim = mag * jnp.cos(a_im * dt), mag * jnp.sin(a_im * dt)
    den = a_re * a_re + a_im * a_im
    zre = ((lre - 1.0) * a_re + lim * a_im) / den
    zim = (lim * a_re - (lre - 1.0) * a_im) / den
    cre, cim = _cmul(c_re.astype(F32), c_im.astype(F32), zre[:, None, :], zim[:, None, :])

    eye = jnp.eye(gps, dtype=F32)

    def in_block(b):
        b = b.astype(F32).reshape(nk, gps, ns, nh)
        return jnp.einsum("kgph,gj->kghjp", b, eye).reshape(nk, ks, sw)

    def out_block(c):
        c = c.reshape(nk, gps, nh, ns)
        return jnp.einsum("kghp,gj->kgpjh", c, eye).reshape(nk, sw, ks)

    wb = jnp.concatenate([in_block(b_re), in_block(b_im)], axis=2).astype(BF16)
    wc = jnp.concatenate([out_block(cre), out_block(-cim)], axis=1).astype(BF16)
    bc = lambda v: jnp.broadcast_to(v.reshape(nk, 1, sw), (nk, NCHAIN, sw))
    return dict(wb=wb, wc=wc, are=bc(lre), aim=bc(lim), lre=lre.reshape(nk, sw), lim=lim.reshape(nk, sw))


def _chain_states(local_end, p, seg, ls, reverse):
    nk, _, sw2 = local_end.shape
    sw = sw2 // 2
    pre, pim = _cpow(p["lre"], p["lim"], ls)
    e = local_end.reshape(nk, NCHAIN // seg, seg, sw2)
    ere, eim = e[..., :sw], e[..., sw:]
    zero = jnp.zeros_like(ere[:, :, 0])
    order = range(seg - 1, -1, -1) if reverse else range(seg)
    sre, sim = zero, zero
    out = [None] * seg
    for k in order:
        out[k] = jnp.concatenate([sre, sim], axis=-1)
        mre, mim = _cmul(pre[:, None, :], pim[:, None, :], sre, sim)
        sre, sim = mre + ere[:, :, k], mim + eim[:, :, k]
    return jnp.stack(out, axis=2).reshape(nk, NCHAIN, sw2)


def _s5_both(s_in, seg, pf, pb, tt):
    ls = s_in.shape[1]
    ys = []
    for p, reverse in ((pf, False), (pb, True)):
        zero = jnp.zeros((1, p["wb"].shape[0], NCHAIN, p["wb"].shape[2]), F32)
        s0 = zero
        if seg > 1:
            _, local_end = _s5_pass(s_in, 1, p["wb"], p["wc"], p["are"], p["aim"], zero, tt, reverse, False)
            s0 = _chain_states(local_end[0], p, seg, ls, reverse)[None]
        y, _ = _s5_pass(s_in, 2, p["wb"], p["wc"], p["are"], p["aim"], jnp.concatenate([s0, zero]), tt,
                        reverse, True)
        ys.append(y)
    return ys


def _glu_kernel(yf_ref, yb_ref, s_ref, d_ref, w_ref, b_ref, o_ref):
    y = yf_ref[...] + yb_ref[...] + d_ref[...] * s_ref[...]
    g = jax.nn.gelu(y)
    o_ref[...] = (g * _sigmoid(_dot(g.astype(BF16), w_ref[...]) + b_ref[...])).astype(o_ref.dtype)


def _glu(yf, yb, s_in, d_skip, glu_w, glu_b, tm):
    rows, cs = yf.shape
    rspec = pl.BlockSpec((tm, cs), lambda i: (i, 0))
    c2 = lambda i: (0, 0)
    return pl.pallas_call(
        _glu_kernel,
        grid=(rows // tm,),
        in_specs=[rspec, rspec, rspec, pl.BlockSpec((1, cs), c2), pl.BlockSpec((cs, cs), c2),
                  pl.BlockSpec((1, cs), c2)],
        out_specs=rspec,
        out_shape=jax.ShapeDtypeStruct((rows, cs), BF16),
        compiler_params=_cparams(("arbitrary",)),
        name="s5_glu",
    )(yf, yb, s_in, d_skip, glu_w, glu_b)


def _pack_halves(v):
    n = v.shape[-1] // 2
    bits = lax.bitcast_convert_type(v.astype(BF16).astype(F32), U32)
    return (bits[..., :n] >> 16) | (bits[..., n:] & jnp.uint32(0xFFFF0000))


def _unpack_halves(p):
    return (lax.bitcast_convert_type(p << 16, F32),
            lax.bitcast_convert_type(p & jnp.uint32(0xFFFF0000), F32))


def _outproj_kernel(xp_ref, xs_ref, co_ref, so_ref, g1_ref, sc2_ref, sh2_ref, wo1_ref, wo2_ref, bo_ref,
                    lg_ref, lb_ref, rwh_ref, rwl_ref, x1_ref, hb_ref, hp_ref, lo_ref, *, alpha):
    g = pl.program_id(0)
    x = jnp.where(g == 0, xp_ref[...], xs_ref[...])
    tt, d = x.shape[1], x.shape[2]
    mix = _dot(co_ref[0], wo1_ref[...]) + _dot(so_ref[0], wo2_ref[...]) + bo_ref[...]
    mix = pltpu.einshape("tcd->ctd", mix.reshape(tt, NCHAIN, d))
    x1 = _layer_norm(alpha * x + g1_ref[0] * mix, lg_ref[...], lb_ref[...])
    h2 = x1 * (1.0 + sc2_ref[0]) + sh2_ref[0]
    x1_ref[0] = x1
    hb = h2.astype(BF16)
    hb_ref[0] = hb
    h2f = h2.reshape(NCHAIN * tt, d)
    packed = _pack_halves(h2f)
    nj = packed.shape[-1] // LANES
    for j in range(nj):
        hp_ref[0, :, :, j, :] = packed[:, j * LANES:(j + 1) * LANES].reshape(NCHAIN, tt, LANES)
    hi = hb.reshape(NCHAIN * tt, d)
    lo = (h2f - hi.astype(F32)).astype(BF16)
    logits = _dot(hi, rwh_ref[...]) + (_dot(hi, rwl_ref[...]) + _dot(lo, rwh_ref[...]))
    lo_ref[0] = logits.reshape(NCHAIN, tt, logits.shape[-1])


def _outproj(xp, xs, conv_out, ssm_out, gate1, scale2, shift2, wo1, wo2, b_out, ln_g, ln_b, rw_hi, rw_lo,
             alpha, tt):
    _, ls, d = xp.shape
    cc, cs = wo1.shape[0], wo2.shape[0]
    ne = rw_hi.shape[1]
    nt = ls // tt
    nj = d // 2 // LANES
    xp_spec, xs_spec = _x_specs(tt, d, nt)
    c2 = lambda g, t: (0, 0)
    nat = lambda w: pl.BlockSpec((1, NCHAIN, tt, w), lambda g, t: (g, 0, t, 0))
    return pl.pallas_call(
        functools.partial(_outproj_kernel, alpha=alpha),
        grid=(2, nt),
        in_specs=[xp_spec, xs_spec,
                  pl.BlockSpec((1, tt * NCHAIN, cc), lambda g, t: (g, t, 0)),
                  pl.BlockSpec((1, tt * NCHAIN, cs), lambda g, t: (g, t, 0)),
                  _mod_spec(d), _mod_spec(d), _mod_spec(d),
                  pl.BlockSpec((cc, d), c2), pl.BlockSpec((cs, d), c2), pl.BlockSpec((1, d), c2),
                  pl.BlockSpec((1, d), c2), pl.BlockSpec((1, d), c2),
                  pl.BlockSpec((d, ne), c2), pl.BlockSpec((d, ne), c2)],
        out_specs=[nat(d), nat(d),
                   pl.BlockSpec((1, NCHAIN, tt, nj, LANES), lambda g, t: (g, 0, t, 0, 0)),
                   nat(ne)],
        out_shape=[jax.ShapeDtypeStruct((2, NCHAIN, ls, d), F32),
                   jax.ShapeDtypeStruct((2, NCHAIN, ls, d), BF16),
                   jax.ShapeDtypeStruct((2, NCHAIN, ls, nj, LANES), U32),
                   jax.ShapeDtypeStruct((2, NCHAIN, ls, ne), F32)],
        compiler_params=_cparams(("arbitrary", "arbitrary")),
        name="outproj",
    )(xp, xs, conv_out, ssm_out, gate1, scale2, shift2, wo1, wo2, b_out, ln_g, ln_b, rw_hi, rw_lo)


def _route_kernel(lg_ref, bias_ref, e_ref, w_ref, r_ref, cnt_ref, carry_ref):
    i = pl.program_id(0)
    tm, ne = lg_ref.shape
    gs = ne // N_EXPERT_GROUPS
    neg = jnp.float32(-jnp.inf)

    @pl.when(i == 0)
    def _():
        carry_ref[...] = jnp.zeros_like(carry_ref)

    scores = _sigmoid(lg_ref[...])
    biased = scores + bias_ref[...]
    lane_i = lax.broadcasted_iota(I32, (tm, ne), 1)
    grp = lane_i // gs
    lane = lane_i.astype(F32)

    def first_max(v):
        m = jnp.max(v, axis=-1, keepdims=True)
        idx = jnp.min(jnp.where(v == m, lane, float(ne)), axis=-1, keepdims=True)
        return m, idx

    gscore = []
    for q in range(N_EXPERT_GROUPS):
        mg = jnp.where(grp == q, biased, neg)
        m1, i1 = first_max(mg)
        m2 = jnp.max(jnp.where(lane == i1, neg, mg), axis=-1, keepdims=True)
        gscore.append(m1 + m2)
    e_mask = jnp.zeros((tm, ne), jnp.bool_)
    for q in range(N_EXPERT_GROUPS):
        beaten = jnp.zeros((tm, 1), I32)
        for o in range(N_EXPERT_GROUPS):
            if o != q:
                wins = (gscore[o] >= gscore[q]) if o < q else (gscore[o] > gscore[q])
                beaten = beaten + wins.astype(I32)
        e_mask = e_mask | ((grp == q) & (beaten < TOPK_GROUPS))
    masked = jnp.where(e_mask, biased, neg)

    lane_k = lax.broadcasted_iota(I32, (tm, TOP_K), 1)
    e_out = jnp.zeros((tm, TOP_K), I32)
    w_out = jnp.zeros((tm, TOP_K), F32)
    chosen = jnp.zeros((tm, ne), F32)
    hits = []
    for k in range(TOP_K):
        _, idx = first_max(masked)
        hit = lane == idx
        hits.append(hit)
        wk = jnp.sum(jnp.where(hit, scores, 0.0), axis=-1, keepdims=True)
        masked = jnp.where(hit, neg, masked)
        chosen = jnp.where(hit, 1.0, chosen)
        e_out = jnp.where(lane_k == k, idx.astype(I32), e_out)
        w_out = jnp.where(lane_k == k, wk, w_out)
    w_ref[...] = w_out / jnp.sum(w_out, axis=-1, keepdims=True) * ROUTED_SCALE
    e_ref[...] = e_out

    rr = lax.broadcasted_iota(I32, (tm, tm), 0)
    cc = lax.broadcasted_iota(I32, (tm, tm), 1)
    before = (cc < rr).astype(BF16)
    prefix = _dot(before, chosen.astype(BF16)) + carry_ref[...]
    r_out = jnp.zeros((tm, TOP_K), I32)
    for k in range(TOP_K):
        rk = jnp.sum(jnp.where(hits[k], prefix, 0.0), axis=-1, keepdims=True)
        r_out = jnp.where(lane_k == k, rk.astype(I32), r_out)
    r_ref[...] = r_out
    carry_ref[...] = carry_ref[...] + jnp.sum(chosen, axis=0, keepdims=True)
    cnt_ref[...] = carry_ref[...]


def _route(logits, bias, tm):
    t, ne = logits.shape
    kspec = pl.BlockSpec((tm, TOP_K), lambda i: (i, 0))
    return pl.pallas_call(
        _route_kernel,
        grid=(t // tm,),
        in_specs=[pl.BlockSpec((tm, ne), lambda i: (i, 0)), pl.BlockSpec((1, ne), lambda i: (0, 0))],
        out_specs=[kspec, kspec, kspec, pl.BlockSpec((1, ne), lambda i: (0, 0))],
        out_shape=[jax.ShapeDtypeStruct((t, TOP_K), I32), jax.ShapeDtypeStruct((t, TOP_K), F32),
                   jax.ShapeDtypeStruct((t, TOP_K), I32), jax.ShapeDtypeStruct((1, ne), F32)],
        scratch_shapes=[pltpu.VMEM((1, ne), F32)],
        compiler_params=_cparams(("arbitrary",)),
        name="route",
    )(logits, bias)


def _dispatch_kernel(dest_ref, h_ref, xs_in_ref, xs_ref, sem):
    del xs_in_ref
    tm = h_ref.shape[0]

    def issue(r, c):
        for k in range(TOP_K):
            pltpu.make_async_copy(h_ref.at[r], xs_ref.at[dest_ref[r * TOP_K + k]], sem).start()
        return c

    lax.fori_loop(0, tm, issue, 0)
    for _ in range(TOP_K):
        pltpu.make_async_copy(h_ref, xs_ref.at[pl.ds(0, tm)], sem).wait()


def _dispatch(dest_flat, h_packed, n_rows, tm):
    t, nj, _ = h_packed.shape
    xs0 = jnp.zeros((n_rows, nj, LANES), U32)
    return pl.pallas_call(
        _dispatch_kernel,
        grid=(t // tm,),
        in_specs=[pl.BlockSpec((tm * TOP_K,), lambda i: (i,), memory_space=pltpu.SMEM),
                  pl.BlockSpec((tm, nj, LANES), lambda i: (i, 0, 0)),
                  pl.BlockSpec(memory_space=pl.ANY)],
        out_specs=pl.BlockSpec(memory_space=pl.ANY),
        out_shape=jax.ShapeDtypeStruct((n_rows, nj, LANES), U32),
        scratch_shapes=[pltpu.SemaphoreType.DMA],
        input_output_aliases={2: 0},
        compiler_params=_cparams(("arbitrary",)),
        name="dispatch",
    )(dest_flat, h_packed, xs0)


def _unpack_rows(p_ref):
    nj = p_ref.shape[1]
    lo, hi = [], []
    for j in range(nj):
        l, h = _unpack_halves(p_ref[:, j, :])
        lo.append(l.astype(BF16))
        hi.append(h.astype(BF16))
    return jnp.concatenate(lo + hi, axis=-1)


def _experts_kernel(te_ref, nu_ref, x_ref, wg_ref, wu_ref, wd_ref, y_ref, wgb_ref, wub_ref, wdb_ref):
    i = pl.program_id(0)
    prev = te_ref[jnp.maximum(i - 1, 0)]

    @pl.when((i == 0) | (te_ref[i] != prev))
    def _():
        wgb_ref[...] = wg_ref[0].astype(BF16)
        wub_ref[...] = wu_ref[0].astype(BF16)
        wdb_ref[...] = wd_ref[0].astype(BF16)

    @pl.when(i < nu_ref[0])
    def _():
        x = _unpack_rows(x_ref)
        gate = _dot(x, wgb_ref[...])
        up = _dot(x, wub_ref[...])
        act = (gate * _sigmoid(gate) * up).astype(BF16)
        y = _dot(act, wdb_ref[...])
        packed = _pack_halves(y)
        for j in range(y_ref.shape[1]):
            y_ref[:, j, :] = packed[:, j * LANES:(j + 1) * LANES]

    @pl.when(i >= nu_ref[0])
    def _():
        y_ref[...] = jnp.zeros_like(y_ref)


def _experts(tile_e, n_used, x_sorted, w_gate, w_up, w_down, tm):
    n_rows, nj, _ = x_sorted.shape
    ne, d, ff = w_gate.shape
    n_tiles = n_rows // tm
    rows = lambda i, te, nu: (jnp.minimum(i, nu[0] - 1), 0, 0)
    wmap = lambda i, te, nu: (te[i], 0, 0)
    grid_spec = pltpu.PrefetchScalarGridSpec(
        num_scalar_prefetch=2,
        grid=(n_tiles,),
        in_specs=[pl.BlockSpec((tm, nj, LANES), rows),
                  pl.BlockSpec((1, d, ff), wmap), pl.BlockSpec((1, d, ff), wmap),
                  pl.BlockSpec((1, ff, d), wmap)],
        out_specs=pl.BlockSpec((tm, nj, LANES), lambda i, te, nu: (i, 0, 0)),
        scratch_shapes=[pltpu.VMEM((d, ff), BF16), pltpu.VMEM((d, ff), BF16), pltpu.VMEM((ff, d), BF16)],
    )
    return pl.pallas_call(
        _experts_kernel,
        grid_spec=grid_spec,
        out_shape=jax.ShapeDtypeStruct((n_rows, nj, LANES), U32),
        compiler_params=_cparams(("arbitrary",)),
        name="experts",
    )(tile_e, n_used, x_sorted, w_gate, w_up, w_down)


def _combine_kernel(dest_ref, w_ref, x1_ref, hb_ref, g2_ref, sg_ref, su_ref, sd_ref, lg_ref, lb_ref,
                    ys_ref, op_ref, os_ref, buf_ref, sem, *, alpha, half):
    i = pl.program_id(0)
    tm, d = hb_ref.shape
    nj = buf_ref.shape[2]

    def issue(r, c):
        for k in range(TOP_K):
            pltpu.make_async_copy(ys_ref.at[dest_ref[r * TOP_K + k]], buf_ref.at[k, r], sem).start()
        return c

    lax.fori_loop(0, tm, issue, 0)

    hb = hb_ref[...]
    gate = _dot(hb, sg_ref[...])
    up = _dot(hb, su_ref[...])
    shared = _dot((gate * _sigmoid(gate) * up).astype(BF16), sd_ref[...])

    for k in range(TOP_K):
        pltpu.make_async_copy(ys_ref.at[pl.ds(0, tm)], buf_ref.at[k], sem).wait()

    w = w_ref[...]
    lo = [None] * nj
    hi = [None] * nj
    for k in range(TOP_K):
        wk = w[:, k:k + 1]
        for j in range(nj):
            l, h = _unpack_halves(buf_ref[k, :, j, :])
            lo[j] = l * wk if k == 0 else lo[j] + l * wk
            hi[j] = h * wk if k == 0 else hi[j] + h * wk
    ffn = jnp.concatenate(lo + hi, axis=-1) + shared
    out = _layer_norm(alpha * x1_ref[...] + g2_ref[0] * ffn, lg_ref[...], lb_ref[...])

    @pl.when(i < half)
    def _():
        op_ref[...] = out

    @pl.when(i >= half)
    def _():
        os_ref[...] = out


def _combine(dest_flat, w, x1, hb, gate2_rows, y_sorted, sh_gate, sh_up, sh_down, ln_g, ln_b, alpha, tm, ls):
    t, d = hb.shape
    nj = y_sorted.shape[1]
    ff = sh_gate.shape[1]
    half = t // 2 // tm
    per_chain = ls // tm
    c2 = lambda i: (0, 0)
    pmap = lambda i: (jnp.minimum(i, half - 1), 0)
    smap = lambda i: (jnp.maximum(i - half, 0), 0)
    return pl.pallas_call(
        functools.partial(_combine_kernel, alpha=alpha, half=half),
        grid=(t // tm,),
        in_specs=[pl.BlockSpec((tm * TOP_K,), lambda i: (i,), memory_space=pltpu.SMEM),
                  pl.BlockSpec((tm, TOP_K), lambda i: (i, 0)),
                  pl.BlockSpec((tm, d), lambda i: (i, 0)),
                  pl.BlockSpec((tm, d), lambda i: (i, 0)),
                  pl.BlockSpec((1, 1, d), lambda i: (i // per_chain, 0, 0)),
                  pl.BlockSpec((d, ff), c2), pl.BlockSpec((d, ff), c2), pl.BlockSpec((ff, d), c2),
                  pl.BlockSpec((1, d), c2), pl.BlockSpec((1, d), c2),
                  pl.BlockSpec(memory_space=pl.ANY)],
        out_specs=[pl.BlockSpec((tm, d), pmap), pl.BlockSpec((tm, d), smap)],
        out_shape=[jax.ShapeDtypeStruct((t // 2, d), F32), jax.ShapeDtypeStruct((t // 2, d), F32)],
        scratch_shapes=[pltpu.VMEM((TOP_K, tm, nj, LANES), U32), pltpu.SemaphoreType.DMA],
        compiler_params=_cparams(("arbitrary",)),
        name="combine",
    )(dest_flat, w, x1, hb, gate2_rows, sh_gate, sh_up, sh_down, ln_g, ln_b, y_sorted)


def _tile(pref, n, mult=8):
    t = min(pref, n)
    while n % t or t % mult:
        t -= 1
    return t


def _encoder_layer(xp, xs, c_all, chain_seq, seg, alpha, p):
    _, ls, d = xp.shape
    cc = p["conv_w"].shape[-1]

    ada = _ada(c_all, p["w_ada"], p["b_ada"])
    mods = ada.reshape(ada.shape[0], N_ADA, d)[chain_seq].reshape(2, NCHAIN, N_ADA, 1, d)
    shift1, scale1, gate1, shift2, scale2, gate2 = (mods[:, :, k] for k in range(N_ADA))

    w_in = p["w_in"].astype(BF16)
    b_in = p["b_in"].reshape(1, -1)
    u, s_in = _inproj(xp, xs, scale1, shift1, w_in[:, :cc], w_in[:, cc:2 * cc], w_in[:, 2 * cc:],
                      b_in[:, :cc], b_in[:, cc:2 * cc], b_in[:, 2 * cc:], _tile(TILES["inproj"], ls))

    chain = np.arange(NCHAIN)
    prev_ok = jnp.asarray(np.stack([(chain % seg != 0), np.zeros(NCHAIN, bool)]).astype(np.float32)[..., None])
    next_ok = jnp.asarray(np.stack([(chain % seg != seg - 1), np.zeros(NCHAIN, bool)]).astype(np.float32)[..., None])
    conv_out = _conv(u, prev_ok, next_ok, p["conv_w"], p["conv_b"], p["conv_ln_g"], p["conv_ln_b"],
                     _tile(TILES["conv"], ls, CONV_HALO))

    pf = _s5_params(p["ssm_b_re"], p["ssm_b_im"], p["ssm_a_re_f"], p["ssm_a_im_f"], p["ssm_log_dt_f"],
                    p["ssm_c_re_f"], p["ssm_c_im_f"])
    pb = _s5_params(p["ssm_b_re"], p["ssm_b_im"], p["ssm_a_re_b"], p["ssm_a_im_b"], p["ssm_log_dt_b"],
                    p["ssm_c_re_b"], p["ssm_c_im_b"])
    yf, yb = _s5_both(s_in, seg, pf, pb, _tile(TILES["s5"], ls))
    cs = s_in.shape[-1]
    rows_tm = 2 * ls * NCHAIN
    flat = lambda a: a.reshape(rows_tm, cs)
    ssm_out = _glu(flat(yf), flat(yb), flat(s_in), p["ssm_d"].reshape(1, cs),
                   p["ssm_glu_w"].astype(BF16), p["ssm_glu_b"].reshape(1, cs), _tile(TILES["glu"], rows_tm))
    ssm_out = ssm_out.reshape(2, ls * NCHAIN, cs)

    w_out = p["w_out"].astype(BF16)
    rw = p["router_w"].astype(F32)
    rw_hi = rw.astype(BF16)
    rw_lo = (rw - rw_hi.astype(F32)).astype(BF16)
    row = lambda v: v.reshape(1, -1)
    x1, hb, hp, logits = _outproj(xp, xs, conv_out, ssm_out, gate1, scale2, shift2, w_out[:cc], w_out[cc:],
                                  row(p["b_out"]), row(p["ln1_g"]), row(p["ln1_b"]), rw_hi, rw_lo,
                                  alpha, _tile(TILES["outproj"], ls, 16))

    t = 2 * NCHAIN * ls
    ne = rw.shape[1]
    nj = hp.shape[-2]
    e_idx, e_w, rank, counts = _route(logits.reshape(t, ne), row(p["router_bias"]).astype(F32),
                                      _tile(TILES["route"], t))

    tm_e = _tile(TILES["expert"], t)
    n_rows = t * TOP_K + ne * tm_e
    counts = counts.reshape(ne).astype(I32)
    padded = (counts + tm_e - 1) // tm_e * tm_e
    pad_end = jnp.cumsum(padded)
    pad_start = pad_end - padded
    dest = (pad_start[e_idx] + rank).reshape(t * TOP_K)
    n_tiles = n_rows // tm_e
    tile_e = jnp.minimum(jnp.searchsorted(pad_end, jnp.arange(n_tiles, dtype=I32) * tm_e, side="right"),
                         ne - 1).astype(I32)
    n_used = (pad_end[-1:] // tm_e).astype(I32)

    x_sorted = _dispatch(dest, hp.reshape(t, nj, LANES), n_rows, _tile(TILES["dispatch"], ls))
    y_sorted = _experts(tile_e, n_used, x_sorted, p["exp_w_gate"], p["exp_w_up"], p["exp_w_down"], tm_e)
    tm_c = _tile(TILES["combine"], ls)
    gate2_rows = gate2.reshape(2 * NCHAIN, 1, d)
    yp, ys = _combine(dest, e_w, x1.reshape(t, d), hb.reshape(t, d), gate2_rows, y_sorted,
                      p["sh_w_gate"].astype(BF16), p["sh_w_up"].astype(BF16), p["sh_w_down"].astype(BF16),
                      row(p["ln2_g"]), row(p["ln2_b"]), alpha, tm_c, ls)
    return yp.reshape(NCHAIN, ls, d), ys.reshape(NCHAIN, ls, d)


_PARAM_NAMES = ("w_ada", "b_ada", "w_in", "b_in", "conv_w", "conv_b", "conv_ln_g", "conv_ln_b",
                "ssm_b_re", "ssm_b_im", "ssm_a_re_f", "ssm_a_im_f", "ssm_log_dt_f",
                "ssm_a_re_b", "ssm_a_im_b", "ssm_log_dt_b", "ssm_c_re_f", "ssm_c_im_f",
                "ssm_c_re_b", "ssm_c_im_b", "ssm_d", "ssm_glu_w", "ssm_glu_b", "w_out", "b_out",
                "ln1_g", "ln1_b", "router_w", "router_bias", "exp_w_gate", "exp_w_up", "exp_w_down",
                "sh_w_gate", "sh_w_up", "sh_w_down", "ln2_g", "ln2_b")


def kernel(x_prompt, x_sample, c_prompt, c_sample, w_ada, b_ada, w_in, b_in, conv_w, conv_b, conv_ln_g, conv_ln_b, ssm_b_re, ssm_b_im, ssm_a_re_f, ssm_a_im_f, ssm_log_dt_f, ssm_a_re_b, ssm_a_im_b, ssm_log_dt_b, ssm_c_re_f, ssm_c_im_f, ssm_c_re_b, ssm_c_im_b, ssm_d, ssm_glu_w, ssm_glu_b, w_out, b_out, ln1_g, ln1_b, router_w, router_bias, exp_w_gate, exp_w_up, exp_w_down, sh_w_gate, sh_w_up, sh_w_down, ln2_g, ln2_b):
    stacked = (w_ada, b_ada, w_in, b_in, conv_w, conv_b, conv_ln_g, conv_ln_b, ssm_b_re, ssm_b_im,
               ssm_a_re_f, ssm_a_im_f, ssm_log_dt_f, ssm_a_re_b, ssm_a_im_b, ssm_log_dt_b, ssm_c_re_f,
               ssm_c_im_f, ssm_c_re_b, ssm_c_im_b, ssm_d, ssm_glu_w, ssm_glu_b, w_out, b_out, ln1_g, ln1_b,
               router_w, router_bias, exp_w_gate, exp_w_up, exp_w_down, sh_w_gate, sh_w_up, sh_w_down,
               ln2_g, ln2_b)
    depth = w_ada.shape[0]
    alpha = (2 * depth) ** 0.25
    bp, lp, d = x_prompt.shape
    bs, lsample, _ = x_sample.shape
    assert NCHAIN % bp == 0 and bs == NCHAIN and lp % (NCHAIN // bp) == 0
    seg = NCHAIN // bp
    ls = lp // seg
    assert ls == lsample

    c_all = jnp.concatenate([c_prompt, c_sample], axis=0)
    c_all = jnp.pad(c_all, ((0, -c_all.shape[0] % 8), (0, 0)))
    chain_seq = np.concatenate([np.arange(NCHAIN) // seg, bp + np.arange(NCHAIN)])

    xp = x_prompt.reshape(NCHAIN, ls, d)
    xs = x_sample.reshape(NCHAIN, ls, d)
    for l in range(depth):
        params = {n: v[l] for n, v in zip(_PARAM_NAMES, stacked)}
        xp, xs = _encoder_layer(xp, xs, c_all, chain_seq, seg, alpha, params)
    return xp.reshape(bp, lp, d), xs.reshape(bs, lsample, d)
```

```python
import functools
import math

import numpy as np
import jax
import jax.numpy as jnp
from jax import lax
from jax.experimental import pallas as pl
from jax.experimental.pallas import tpu as pltpu

F32 = jnp.float32
BF16 = jnp.bfloat16
U32 = jnp.uint32
I32 = jnp.int32

LN_EPS = 1e-5
N_ADA = 6
TOP_K = 8
N_EXPERT_GROUPS = 8
TOPK_GROUPS = 4
ROUTED_SCALE = 2.5

NCHAIN = 8
LANES = 128
MXU_DIM = 256
VMEM_LIMIT = 56 * 1024 * 1024

TILES = dict(inproj=64, conv=128, s5=128, glu=1024, outproj=32, route=512, dest=1024, expert=512,
             dispatch=512, combine=256)


def _cparams(sem):
    return pltpu.CompilerParams(dimension_semantics=sem, vmem_limit_bytes=VMEM_LIMIT)


def _resident(shape):
    zeros = (0,) * len(shape)
    return pl.BlockSpec(shape, lambda *_: zeros, pipeline_mode=pl.Buffered(1))


def _dot(a, b):
    return jnp.dot(a, b, preferred_element_type=F32)


def _sigmoid(x):
    return jax.nn.sigmoid(x)


def _layer_norm(v, g, b):
    mu = jnp.mean(v, axis=-1, keepdims=True)
    d = v - mu
    var = jnp.mean(d * d, axis=-1, keepdims=True)
    return d * lax.rsqrt(var + LN_EPS) * g + b


def _ada_kernel(c_ref, w_ref, b_ref, o_ref):
    c = c_ref[...]
    s = (c * _sigmoid(c)).astype(BF16)
    o_ref[...] = _dot(s, w_ref[...].astype(BF16)) + b_ref[...]


def _ada(c_all, w_ada, b_ada):
    rows, d = c_all.shape
    n = w_ada.shape[1]
    tn = _tile(2048, n, LANES)
    return pl.pallas_call(
        _ada_kernel,
        grid=(n // tn,),
        in_specs=[pl.BlockSpec((rows, d), lambda j: (0, 0)),
                  pl.BlockSpec((d, tn), lambda j: (0, j)),
                  pl.BlockSpec((1, tn), lambda j: (0, j))],
        out_specs=pl.BlockSpec((rows, tn), lambda j: (0, j)),
        out_shape=jax.ShapeDtypeStruct((rows, n), F32),
        compiler_params=_cparams(("arbitrary",)),
        name="ada",
    )(c_all, w_ada, b_ada.reshape(1, n))


def _x_specs(tt, d, nt):
    xp = pl.BlockSpec((NCHAIN, tt, d), lambda g, t: (0, jnp.where(g == 0, t, nt - 1), 0))
    xs = pl.BlockSpec((NCHAIN, tt, d), lambda g, t: (0, jnp.where(g == 1, t, 0), 0))
    return xp, xs


def _mod_spec(d):
    return pl.BlockSpec((1, NCHAIN, 1, d), lambda g, t: (g, 0, 0, 0))


def _inproj_kernel(xp_ref, xs_ref, sc_ref, sh_ref, wa_ref, wg_ref, ws_ref, ba_ref, bg_ref, bs_ref,
                   u_ref, s_ref):
    g = pl.program_id(0)
    x = jnp.where(g == 0, xp_ref[...], xs_ref[...])
    h = x * (1.0 + sc_ref[0]) + sh_ref[0]
    tt = h.shape[1]
    ht = pltpu.einshape("ctd->tcd", h).reshape(tt * NCHAIN, h.shape[2]).astype(BF16)
    a = _dot(ht, wa_ref[...]) + ba_ref[...]
    gt = _dot(ht, wg_ref[...]) + bg_ref[...]
    u = a * _sigmoid(gt)
    s = _dot(ht, ws_ref[...]) + bs_ref[...]
    u_ref[0] = u.reshape(tt, NCHAIN, u.shape[-1])
    s_ref[0] = s.reshape(tt, NCHAIN, s.shape[-1])


def _inproj(xp, xs, scale1, shift1, wa, wg, ws, ba, bg, bs, tt):
    _, ls, d = xp.shape
    cc, cs = wa.shape[1], ws.shape[1]
    nt = ls // tt
    xp_spec, xs_spec = _x_specs(tt, d, nt)
    const2 = lambda g, t: (0, 0)
    return pl.pallas_call(
        _inproj_kernel,
        grid=(2, nt),
        in_specs=[xp_spec, xs_spec, _mod_spec(d), _mod_spec(d),
                  _resident((d, cc)), _resident((d, cc)), _resident((d, cs)),
                  pl.BlockSpec((1, cc), const2), pl.BlockSpec((1, cc), const2), pl.BlockSpec((1, cs), const2)],
        out_specs=[pl.BlockSpec((1, tt, NCHAIN, cc), lambda g, t: (g, t, 0, 0)),
                   pl.BlockSpec((1, tt, NCHAIN, cs), lambda g, t: (g, t, 0, 0))],
        out_shape=[jax.ShapeDtypeStruct((2, ls, NCHAIN, cc), F32),
                   jax.ShapeDtypeStruct((2, ls, NCHAIN, cs), F32)],
        compiler_params=_cparams(("arbitrary", "arbitrary")),
        name="inproj",
    )(xp, xs, scale1, shift1, wa, wg, ws, ba, bg, bs)


CONV_HALO = 16


def _conv_kernel(u_ref, up_ref, un_ref, pok_ref, nok_ref, w_ref, cb_ref, g_ref, b_ref, o_ref,
                 buf_ref, acc_ref, *, width):
    t = pl.program_id(1)
    nt = pl.num_programs(1)
    tt = u_ref.shape[1]
    pad = (width - 1) // 2
    prev = up_ref[0]
    prev_wrapped = pltpu.roll(prev, shift=1, axis=1) * pok_ref[0]
    buf_ref[0:CONV_HALO] = jnp.where(t == 0, prev_wrapped, prev)
    buf_ref[CONV_HALO:CONV_HALO + tt] = u_ref[0]
    nxt = un_ref[0]
    next_wrapped = pltpu.roll(nxt, shift=NCHAIN - 1, axis=1) * nok_ref[0]
    buf_ref[CONV_HALO + tt:2 * CONV_HALO + tt] = jnp.where(t == nt - 1, next_wrapped, nxt)

    base = CONV_HALO - pad

    def body(j, carry):
        acc = cb_ref[...]
        for k in range(width):
            acc = acc + buf_ref[j + base + k] * w_ref[k]
        acc_ref[j] = acc
        return carry

    lax.fori_loop(0, tt, body, 0)
    v = _layer_norm(acc_ref[...], g_ref[...], b_ref[...])
    v = v * _sigmoid(v)
    o_ref[0] = v.reshape(tt * NCHAIN, v.shape[-1]).astype(o_ref.dtype)


def _conv(u, prev_ok, next_ok, conv_w, conv_b, ln_g, ln_b, tt):
    _, ls, _, cc = u.shape
    width = conv_w.shape[0]
    assert (width - 1) // 2 <= CONV_HALO and tt % CONV_HALO == 0
    nt = ls // tt
    hb = tt // CONV_HALO
    nh = ls // CONV_HALO
    wb = jnp.broadcast_to(conv_w[:, None, :], (width, NCHAIN, cc))
    row = lambda v: jnp.broadcast_to(v[None, :], (NCHAIN, cc))
    const2 = lambda g, t: (0, 0)
    return pl.pallas_call(
        functools.partial(_conv_kernel, width=width),
        grid=(2, nt),
        in_specs=[pl.BlockSpec((1, tt, NCHAIN, cc), lambda g, t: (g, t, 0, 0)),
                  pl.BlockSpec((1, CONV_HALO, NCHAIN, cc),
                               lambda g, t: (g, jnp.where(t == 0, nh - 1, t * hb - 1), 0, 0)),
                  pl.BlockSpec((1, CONV_HALO, NCHAIN, cc),
                               lambda g, t: (g, jnp.where(t == nt - 1, 0, (t + 1) * hb), 0, 0)),
                  pl.BlockSpec((1, NCHAIN, 1), lambda g, t: (g, 0, 0)),
                  pl.BlockSpec((1, NCHAIN, 1), lambda g, t: (g, 0, 0)),
                  pl.BlockSpec((width, NCHAIN, cc), lambda g, t: (0, 0, 0)),
                  pl.BlockSpec((NCHAIN, cc), const2), pl.BlockSpec((NCHAIN, cc), const2),
                  pl.BlockSpec((NCHAIN, cc), const2)],
        out_specs=pl.BlockSpec((1, tt * NCHAIN, cc), lambda g, t: (g, t, 0)),
        out_shape=jax.ShapeDtypeStruct((2, ls * NCHAIN, cc), BF16),
        scratch_shapes=[pltpu.VMEM((tt + 2 * CONV_HALO, NCHAIN, cc), F32),
                        pltpu.VMEM((tt, NCHAIN, cc), F32)],
        compiler_params=_cparams(("arbitrary", "arbitrary")),
        name="conv",
    )(u, u, u, prev_ok, next_ok, wb, row(conv_b), row(ln_g), row(ln_b))


S5_LANE_BLOCK = 512
S5_UNROLL = 4


def _s5_kernel(u_ref, wb_ref, wc_ref, are_ref, aim_ref, s0_ref, *rest, reverse, emit_y):
    if emit_y:
        y_ref, sfin_ref, bu_ref, st_ref, carry_ref = rest
    else:
        sfin_ref, bu_ref, carry_ref = rest
        st_ref = None
    i = pl.program_id(1)
    tt = u_ref.shape[1]
    nk, ks, sw2 = wb_ref.shape
    sw = sw2 // 2

    @pl.when(i == 0)
    def _():
        carry_ref[...] = s0_ref[0]

    u2 = u_ref[0].reshape(tt * NCHAIN, u_ref.shape[3]).astype(BF16)
    for kc in range(nk):
        bu_ref[...] = _dot(u2[:, kc * ks:(kc + 1) * ks], wb_ref[kc]).reshape(tt, NCHAIN, sw2)
        for lo in range(0, sw, S5_LANE_BLOCK):
            lb = min(S5_LANE_BLOCK, sw - lo)
            re_sl = slice(lo, lo + lb)
            im_sl = slice(sw + lo, sw + lo + lb)
            ar = are_ref[kc, :, re_sl]
            ai = aim_ref[kc, :, re_sl]

            def step(j, carry, re_sl=re_sl, im_sl=im_sl, ar=ar, ai=ai):
                sre, sim = carry
                for q in range(S5_UNROLL):
                    jj = j * S5_UNROLL + q
                    tloc = tt - 1 - jj if reverse else jj
                    nre = ar * sre - ai * sim + bu_ref[tloc, :, re_sl]
                    nim = ar * sim + ai * sre + bu_ref[tloc, :, im_sl]
                    if emit_y:
                        st_ref[tloc, :, re_sl] = nre
                        st_ref[tloc, :, im_sl] = nim
                    sre, sim = nre, nim
                return sre, sim

            sre, sim = lax.fori_loop(0, tt // S5_UNROLL, step,
                                     (carry_ref[kc, :, re_sl], carry_ref[kc, :, im_sl]))
            carry_ref[kc, :, re_sl] = sre
            carry_ref[kc, :, im_sl] = sim
        if emit_y:
            st = st_ref[...].reshape(tt * NCHAIN, sw2).astype(BF16)
            y_ref[0, :, :, kc * ks:(kc + 1) * ks] = _dot(st, wc_ref[kc]).reshape(tt, NCHAIN, ks)
    sfin_ref[0] = carry_ref[...]


def _s5_pass(s_in, ngroups, wb, wc, are, aim, s0, tt, reverse, emit_y):
    _, ls, _, cs = s_in.shape
    nk, ks, sw2 = wb.shape
    assert tt % S5_UNROLL == 0
    nt = ls // tt
    tmap = (lambda i: nt - 1 - i) if reverse else (lambda i: i)
    c3 = lambda g, i: (0, 0, 0)
    state_spec = pl.BlockSpec((1, nk, NCHAIN, sw2), lambda g, i: (g, 0, 0, 0))
    out_specs = [state_spec]
    out_shape = [jax.ShapeDtypeStruct((ngroups, nk, NCHAIN, sw2), F32)]
    scratch = [pltpu.VMEM((tt, NCHAIN, sw2), F32)]
    if emit_y:
        out_specs.insert(0, pl.BlockSpec((1, tt, NCHAIN, cs), lambda g, i: (g, tmap(i), 0, 0)))
        out_shape.insert(0, jax.ShapeDtypeStruct((ngroups, ls, NCHAIN, cs), F32))
        scratch.append(pltpu.VMEM((tt, NCHAIN, sw2), F32))
    scratch.append(pltpu.VMEM((nk, NCHAIN, sw2), F32))
    res = pl.pallas_call(
        functools.partial(_s5_kernel, reverse=reverse, emit_y=emit_y),
        grid=(ngroups, nt),
        in_specs=[pl.BlockSpec((1, tt, NCHAIN, cs), lambda g, i: (g, tmap(i), 0, 0)),
                  _resident((nk, ks, sw2)), _resident((nk, sw2, ks)),
                  _resident((nk, NCHAIN, sw2 // 2)), _resident((nk, NCHAIN, sw2 // 2)),
                  state_spec],
        out_specs=out_specs,
        out_shape=out_shape,
        scratch_shapes=scratch,
        compiler_params=_cparams(("arbitrary", "arbitrary")),
        name="s5_" + ("bwd" if reverse else "fwd") + ("" if emit_y else "_state"),
    )(s_in, wb, wc, are, aim, s0)
    return (res[0], res[1]) if emit_y else (None, res[0])


def _cmul(are, aim, bre, bim):
    return are * bre - aim * bim, are * bim + aim * bre


def _cpow(re, im, n):
    rre, rim = jnp.ones_like(re), jnp.zeros_like(im)
    while n:
        if n & 1:
            rre, rim = _cmul(rre, rim, re, im)
        re, im = _cmul(re, im, re, im)
        n >>= 1
    return rre, rim


def _s5_params(b_re, b_im, a_re, a_im, log_dt, c_re, c_im):
    ng, ns, nh = b_re.shape
    cs = ng * nh
    ks = min(MXU_DIM, cs)
    gps = ks // nh
    nk = cs // ks
    sw = gps * ns
    a_re, a_im = a_re.astype(F32), a_im.astype(F32)
    dt = jnp.exp(log_dt.astype(F32))[:, None]
    mag = jnp.exp(a_re * dt)
    lre, lim = mag * jnp.cos(a_im * dt), mag * jnp.sin(a_im * dt)
    den = a_re * a_re + a_im * a_im
    zre = ((lre - 1.0) * a_re + lim * a_im) / den
    zim = (lim * a_re - (lre - 1.0) * a_im) / den
    cre, cim = _cmul(c_re.astype(F32), c_im.astype(F32), zre[:, None, :], zim[:, None, :])

    eye = jnp.eye(gps, dtype=F32)

    def in_block(b):
        b = b.astype(F32).reshape(nk, gps, ns, nh)
        return jnp.einsum("kgph,gj->kghjp", b, eye).reshape(nk, ks, sw)

    def out_block(c):
        c = c.reshape(nk, gps, nh, ns)
        return jnp.einsum("kghp,gj->kgpjh", c, eye).reshape(nk, sw, ks)

    wb = jnp.concatenate([in_block(b_re), in_block(b_im)], axis=2).astype(BF16)
    wc = jnp.concatenate([out_block(cre), out_block(-cim)], axis=1).astype(BF16)
    bc = lambda v: jnp.broadcast_to(v.reshape(nk, 1, sw), (nk, NCHAIN, sw))
    return dict(wb=wb, wc=wc, are=bc(lre), aim=bc(lim), lre=lre.reshape(nk, sw), lim=lim.reshape(nk, sw))


def _chain_states(local_end, p, seg, ls, reverse):
    nk, _, sw2 = local_end.shape
    sw = sw2 // 2
    pre, pim = _cpow(p["lre"], p["lim"], ls)
    e = local_end.reshape(nk, NCHAIN // seg, seg, sw2)
    ere, eim = e[..., :sw], e[..., sw:]
    zero = jnp.zeros_like(ere[:, :, 0])
    order = range(seg - 1, -1, -1) if reverse else range(seg)
    sre, sim = zero, zero
    out = [None] * seg
    for k in order:
        out[k] = jnp.concatenate([sre, sim], axis=-1)
        mre, mim = _cmul(pre[:, None, :], pim[:, None, :], sre, sim)
        sre, sim = mre + ere[:, :, k], mim + eim[:, :, k]
    return jnp.stack(out, axis=2).reshape(nk, NCHAIN, sw2)


def _s5_both(s_in, seg, pf, pb, tt):
    ls = s_in.shape[1]
    ys = []
    for p, reverse in ((pf, False), (pb, True)):
        zero = jnp.zeros((1, p["wb"].shape[0], NCHAIN, p["wb"].shape[2]), F32)
        s0 = zero
        if seg > 1:
            _, local_end = _s5_pass(s_in, 1, p["wb"], p["wc"], p["are"], p["aim"], zero, tt, reverse, False)
            s0 = _chain_states(local_end[0], p, seg, ls, reverse)[None]
        y, _ = _s5_pass(s_in, 2, p["wb"], p["wc"], p["are"], p["aim"], jnp.concatenate([s0, zero]), tt,
                        reverse, True)
        ys.append(y)
    return ys


def _glu_kernel(yf_ref, yb_ref, s_ref, d_ref, w_ref, b_ref, o_ref):
    y = yf_ref[...] + yb_ref[...] + d_ref[...] * s_ref[...]
    g = jax.nn.gelu(y)
    o_ref[...] = (g * _sigmoid(_dot(g.astype(BF16), w_ref[...]) + b_ref[...])).astype(o_ref.dtype)


def _glu(yf, yb, s_in, d_skip, glu_w, glu_b, tm):
    rows, cs = yf.shape
    rspec = pl.BlockSpec((tm, cs), lambda i: (i, 0))
    c2 = lambda i: (0, 0)
    return pl.pallas_call(
        _glu_kernel,
        grid=(rows // tm,),
        in_specs=[rspec, rspec, rspec, pl.BlockSpec((1, cs), c2), pl.BlockSpec((cs, cs), c2),
                  pl.BlockSpec((1, cs), c2)],
        out_specs=rspec,
        out_shape=jax.ShapeDtypeStruct((rows, cs), BF16),
        compiler_params=_cparams(("arbitrary",)),
        name="s5_glu",
    )(yf, yb, s_in, d_skip, glu_w, glu_b)


def _pack_halves(v):
    n = v.shape[-1] // 2
    bits = lax.bitcast_convert_type(v.astype(BF16).astype(F32), U32)
    return (bits[..., :n] >> 16) | (bits[..., n:] & jnp.uint32(0xFFFF0000))


def _unpack_halves(p):
    return (lax.bitcast_convert_type(p << 16, F32),
            lax.bitcast_convert_type(p & jnp.uint32(0xFFFF0000), F32))


def _outproj_kernel(xp_ref, xs_ref, co_ref, so_ref, g1_ref, sc2_ref, sh2_ref, wo1_ref, wo2_ref, bo_ref,
                    lg_ref, lb_ref, rwh_ref, rwl_ref, x1_ref, hb_ref, hp_ref, lo_ref, *, alpha):
    g = pl.program_id(0)
    x = jnp.where(g == 0, xp_ref[...], xs_ref[...])
    tt, d = x.shape[1], x.shape[2]
    mix = _dot(co_ref[0], wo1_ref[...]) + _dot(so_ref[0], wo2_ref[...]) + bo_ref[...]
    mix = pltpu.einshape("tcd->ctd", mix.reshape(tt, NCHAIN, d))
    x1 = _layer_norm(alpha * x + g1_ref[0] * mix, lg_ref[...], lb_ref[...])
    h2 = x1 * (1.0 + sc2_ref[0]) + sh2_ref[0]
    x1_ref[0] = x1
    hb = h2.astype(BF16)
    hb_ref[0] = hb
    h2f = h2.reshape(NCHAIN * tt, d)
    packed = _pack_halves(h2f)
    nj = packed.shape[-1] // LANES
    for j in range(nj):
        hp_ref[0, :, pl.ds(j, tt, stride=nj), :] = (
            packed[:, j * LANES:(j + 1) * LANES].reshape(NCHAIN, tt, LANES))
    hi = hb.reshape(NCHAIN * tt, d)
    lo = (h2f - hi.astype(F32)).astype(BF16)
    logits = _dot(hi, rwh_ref[...]) + (_dot(hi, rwl_ref[...]) + _dot(lo, rwh_ref[...]))
    lo_ref[0] = logits.reshape(NCHAIN, tt, logits.shape[-1])


def _outproj(xp, xs, conv_out, ssm_out, gate1, scale2, shift2, wo1, wo2, b_out, ln_g, ln_b, rw_hi, rw_lo,
             alpha, tt):
    _, ls, d = xp.shape
    cc, cs = wo1.shape[0], wo2.shape[0]
    ne = rw_hi.shape[1]
    nt = ls // tt
    nj = d // 2 // LANES
    xp_spec, xs_spec = _x_specs(tt, d, nt)
    c2 = lambda g, t: (0, 0)
    nat = lambda w: pl.BlockSpec((1, NCHAIN, tt, w), lambda g, t: (g, 0, t, 0))
    return pl.pallas_call(
        functools.partial(_outproj_kernel, alpha=alpha),
        grid=(2, nt),
        in_specs=[xp_spec, xs_spec,
                  pl.BlockSpec((1, tt * NCHAIN, cc), lambda g, t: (g, t, 0)),
                  pl.BlockSpec((1, tt * NCHAIN, cs), lambda g, t: (g, t, 0)),
                  _mod_spec(d), _mod_spec(d), _mod_spec(d),
                  _resident((cc, d)), _resident((cs, d)), pl.BlockSpec((1, d), c2),
                  pl.BlockSpec((1, d), c2), pl.BlockSpec((1, d), c2),
                  _resident((d, ne)), _resident((d, ne))],
        out_specs=[nat(d), nat(d),
                   pl.BlockSpec((1, NCHAIN, tt * nj, LANES), lambda g, t: (g, 0, t, 0)),
                   nat(ne)],
        out_shape=[jax.ShapeDtypeStruct((2, NCHAIN, ls, d), F32),
                   jax.ShapeDtypeStruct((2, NCHAIN, ls, d), BF16),
                   jax.ShapeDtypeStruct((2, NCHAIN, ls * nj, LANES), U32),
                   jax.ShapeDtypeStruct((2, NCHAIN, ls, ne), F32)],
        compiler_params=_cparams(("arbitrary", "arbitrary")),
        name="outproj",
    )(xp, xs, conv_out, ssm_out, gate1, scale2, shift2, wo1, wo2, b_out, ln_g, ln_b, rw_hi, rw_lo)


def _route_kernel(lg_ref, bias_ref, e_ref, w_ref, r_ref, cnt_ref, carry_ref):
    i = pl.program_id(0)
    tm, ne = lg_ref.shape
    gs = ne // N_EXPERT_GROUPS
    neg = jnp.float32(-jnp.inf)

    @pl.when(i == 0)
    def _():
        carry_ref[...] = jnp.zeros_like(carry_ref)

    scores = _sigmoid(lg_ref[...])
    biased = scores + bias_ref[...]
    lane_i = lax.broadcasted_iota(I32, (tm, ne), 1)
    grp = lane_i // gs
    lane = lane_i.astype(F32)

    def first_max(v):
        m = jnp.max(v, axis=-1, keepdims=True)
        idx = jnp.min(jnp.where(v == m, lane, float(ne)), axis=-1, keepdims=True)
        return m, idx

    gscore = []
    for q in range(N_EXPERT_GROUPS):
        mg = jnp.where(grp == q, biased, neg)
        m1, i1 = first_max(mg)
        m2 = jnp.max(jnp.where(lane == i1, neg, mg), axis=-1, keepdims=True)
        gscore.append(m1 + m2)
    e_mask = jnp.zeros((tm, ne), jnp.bool_)
    for q in range(N_EXPERT_GROUPS):
        beaten = jnp.zeros((tm, 1), I32)
        for o in range(N_EXPERT_GROUPS):
            if o != q:
                wins = (gscore[o] >= gscore[q]) if o < q else (gscore[o] > gscore[q])
                beaten = beaten + wins.astype(I32)
        e_mask = e_mask | ((grp == q) & (beaten < TOPK_GROUPS))
    masked = jnp.where(e_mask, biased, neg)

    lane_k = lax.broadcasted_iota(I32, (tm, TOP_K), 1)
    e_out = jnp.zeros((tm, TOP_K), I32)
    w_out = jnp.zeros((tm, TOP_K), F32)
    chosen = jnp.zeros((tm, ne), F32)
    hits = []
    for k in range(TOP_K):
        _, idx = first_max(masked)
        hit = lane == idx
        hits.append(hit)
        wk = jnp.sum(jnp.where(hit, scores, 0.0), axis=-1, keepdims=True)
        masked = jnp.where(hit, neg, masked)
        chosen = jnp.where(hit, 1.0, chosen)
        e_out = jnp.where(lane_k == k, idx.astype(I32), e_out)
        w_out = jnp.where(lane_k == k, wk, w_out)
    w_ref[...] = w_out / jnp.sum(w_out, axis=-1, keepdims=True) * ROUTED_SCALE
    e_ref[...] = e_out

    rr = lax.broadcasted_iota(I32, (tm, tm), 0)
    cc = lax.broadcasted_iota(I32, (tm, tm), 1)
    before = (cc < rr).astype(BF16)
    prefix = _dot(before, chosen.astype(BF16)) + carry_ref[...]
    r_out = jnp.zeros((tm, TOP_K), I32)
    for k in range(TOP_K):
        rk = jnp.sum(jnp.where(hits[k], prefix, 0.0), axis=-1, keepdims=True)
        r_out = jnp.where(lane_k == k, rk.astype(I32), r_out)
    r_ref[...] = r_out
    carry_ref[...] = carry_ref[...] + jnp.sum(chosen, axis=0, keepdims=True)
    cnt_ref[...] = carry_ref[...]


def _route(logits, bias, tm):
    t, ne = logits.shape
    kspec = pl.BlockSpec((tm, TOP_K), lambda i: (i, 0))
    return pl.pallas_call(
        _route_kernel,
        grid=(t // tm,),
        in_specs=[pl.BlockSpec((tm, ne), lambda i: (i, 0)), pl.BlockSpec((1, ne), lambda i: (0, 0))],
        out_specs=[kspec, kspec, kspec, pl.BlockSpec((1, ne), lambda i: (0, 0))],
        out_shape=[jax.ShapeDtypeStruct((t, TOP_K), I32), jax.ShapeDtypeStruct((t, TOP_K), F32),
                   jax.ShapeDtypeStruct((t, TOP_K), I32), jax.ShapeDtypeStruct((1, ne), F32)],
        scratch_shapes=[pltpu.VMEM((1, ne), F32)],
        compiler_params=_cparams(("arbitrary",)),
        name="route",
    )(logits, bias)


def _dest_kernel(e_ref, r_ref, ps_ref, o_ref):
    tm, ne = e_ref.shape[0], ps_ref.shape[1]
    lane = lax.broadcasted_iota(I32, (tm, ne), 1)
    lane_k = lax.broadcasted_iota(I32, (tm, TOP_K), 1)
    e = e_ref[...]
    ps = ps_ref[...]
    out = r_ref[...]
    for k in range(TOP_K):
        start = jnp.sum(jnp.where(lane == e[:, k:k + 1], ps, 0.0), axis=-1, keepdims=True)
        out = jnp.where(lane_k == k, out + start.astype(I32), out)
    o_ref[...] = out


def _dest(e_idx, rank, pad_start, tm):
    t = e_idx.shape[0]
    ne = pad_start.shape[1]
    kspec = pl.BlockSpec((tm, TOP_K), lambda i: (i, 0))
    return pl.pallas_call(
        _dest_kernel,
        grid=(t // tm,),
        in_specs=[kspec, kspec, pl.BlockSpec((1, ne), lambda i: (0, 0))],
        out_specs=kspec,
        out_shape=jax.ShapeDtypeStruct((t, TOP_K), I32),
        compiler_params=_cparams(("arbitrary",)),
        name="dest",
    )(e_idx, rank, pad_start)


def _dispatch_kernel(last_ref, nu_ref, dest_ref, h_ref, xs_ref, zero_ref, sem, zsem, *, nj, tile_rows):
    i = pl.program_id(0)
    tm = h_ref.shape[0] // nj
    ne = last_ref.shape[0]

    @pl.when(i == 0)
    def _():
        zero_ref[...] = jnp.zeros_like(zero_ref)

        def fill(row):
            start = pl.multiple_of(row * nj, 8)
            return pltpu.make_async_copy(zero_ref, xs_ref.at[pl.ds(start, tile_rows * nj)], zsem)

        def issue(e, c):
            @pl.when(last_ref[e] >= 0)
            def _():
                fill(last_ref[e]).start()
            return c

        def drain(e, c):
            @pl.when(last_ref[e] >= 0)
            def _():
                fill(last_ref[e]).wait()
            return c

        lax.fori_loop(0, ne, issue, 0)
        lax.fori_loop(0, ne, drain, 0)
        n_tiles = xs_ref.shape[0] // (tile_rows * nj)
        lax.fori_loop(nu_ref[0], n_tiles, lambda q, c: (fill(q * tile_rows).start(), c)[1], 0)
        lax.fori_loop(nu_ref[0], n_tiles, lambda q, c: (fill(q * tile_rows).wait(), c)[1], 0)

    def issue_rows(r, c):
        src = h_ref.at[pl.ds(pl.multiple_of(r * nj, nj), nj)]
        for k in range(TOP_K):
            dst = xs_ref.at[pl.ds(pl.multiple_of(dest_ref[r * TOP_K + k] * nj, nj), nj)]
            pltpu.make_async_copy(src, dst, sem).start()
        return c

    lax.fori_loop(0, tm, issue_rows, 0)
    for _ in range(TOP_K):
        pltpu.make_async_copy(h_ref, xs_ref.at[pl.ds(0, tm * nj)], sem).wait()


def _dispatch(last_tile_row, n_used, dest_flat, h_packed, n_rows, nj, tile_rows, tm):
    t = h_packed.shape[0] // nj
    grid_spec = pltpu.PrefetchScalarGridSpec(
        num_scalar_prefetch=2,
        grid=(t // tm,),
        in_specs=[pl.BlockSpec((tm * TOP_K,), lambda i, last, nu: (i,), memory_space=pltpu.SMEM),
                  pl.BlockSpec((tm * nj, LANES), lambda i, last, nu: (i, 0))],
        out_specs=pl.BlockSpec(memory_space=pl.ANY),
        scratch_shapes=[pltpu.VMEM((tile_rows * nj, LANES), U32), pltpu.SemaphoreType.DMA,
                        pltpu.SemaphoreType.DMA],
    )
    return pl.pallas_call(
        functools.partial(_dispatch_kernel, nj=nj, tile_rows=tile_rows),
        grid_spec=grid_spec,
        out_shape=jax.ShapeDtypeStruct((n_rows * nj, LANES), U32),
        compiler_params=_cparams(("arbitrary",)),
        name="dispatch",
    )(last_tile_row, n_used, dest_flat, h_packed)


def _unpack_rows(p_ref, tm, nj):
    lo, hi = [], []
    for j in range(nj):
        l, h = _unpack_halves(p_ref[pl.ds(j, tm, stride=nj), :])
        lo.append(l.astype(BF16))
        hi.append(h.astype(BF16))
    return jnp.concatenate(lo + hi, axis=-1)


def _experts_kernel(te_ref, nu_ref, x_ref, wg_ref, wu_ref, wd_ref, y_ref, wgb_ref, wub_ref, wdb_ref, *, nj):
    i = pl.program_id(0)
    tm = x_ref.shape[0] // nj
    prev = te_ref[jnp.maximum(i - 1, 0)]

    @pl.when((i == 0) | (te_ref[i] != prev))
    def _():
        wgb_ref[...] = wg_ref[0].astype(BF16)
        wub_ref[...] = wu_ref[0].astype(BF16)
        wdb_ref[...] = wd_ref[0].astype(BF16)

    @pl.when(i < nu_ref[0])
    def _():
        x = _unpack_rows(x_ref, tm, nj)
        gate = _dot(x, wgb_ref[...])
        up = _dot(x, wub_ref[...])
        act = (gate * _sigmoid(gate) * up).astype(BF16)
        y = _dot(act, wdb_ref[...])
        packed = _pack_halves(y)
        for j in range(nj):
            y_ref[pl.ds(j, tm, stride=nj), :] = packed[:, j * LANES:(j + 1) * LANES]

    @pl.when(i >= nu_ref[0])
    def _():
        y_ref[...] = jnp.zeros_like(y_ref)


def _experts(tile_e, n_used, x_sorted, w_gate, w_up, w_down, nj, tm):
    n_rows = x_sorted.shape[0] // nj
    ne, d, ff = w_gate.shape
    n_tiles = n_rows // tm
    wmap = lambda i, te, nu: (te[i], 0, 0)
    grid_spec = pltpu.PrefetchScalarGridSpec(
        num_scalar_prefetch=2,
        grid=(n_tiles,),
        in_specs=[pl.BlockSpec((tm * nj, LANES), lambda i, te, nu: (jnp.minimum(i, nu[0] - 1), 0)),
                  pl.BlockSpec((1, d, ff), wmap), pl.BlockSpec((1, d, ff), wmap),
                  pl.BlockSpec((1, ff, d), wmap)],
        out_specs=pl.BlockSpec((tm * nj, LANES), lambda i, te, nu: (i, 0)),
        scratch_shapes=[pltpu.VMEM((d, ff), BF16), pltpu.VMEM((d, ff), BF16), pltpu.VMEM((ff, d), BF16)],
    )
    return pl.pallas_call(
        functools.partial(_experts_kernel, nj=nj),
        grid_spec=grid_spec,
        out_shape=jax.ShapeDtypeStruct((n_rows * nj, LANES), U32),
        compiler_params=_cparams(("arbitrary",)),
        name="experts",
    )(tile_e, n_used, x_sorted, w_gate, w_up, w_down)


COMBINE_ROWS = 16


def _combine_kernel(dest_ref, dnext_ref, w_ref, x1_ref, hb_ref, g2_ref, sg_ref, su_ref, sd_ref, lg_ref, lb_ref,
                    ys_ref, op_ref, os_ref, buf_ref, acc_ref, sem, *, alpha, half, nj):
    i = pl.program_id(0)
    n = pl.num_programs(0)
    tm, d = hb_ref.shape
    slot = i % 2

    def gather(d_ref, s):
        def issue(r, c):
            for k in range(TOP_K):
                src = ys_ref.at[pl.ds(pl.multiple_of(d_ref[r * TOP_K + k] * nj, nj), nj)]
                dst = buf_ref.at[s, k, pl.ds(pl.multiple_of(r * nj, nj), nj)]
                pltpu.make_async_copy(src, dst, sem.at[s]).start()
            return c

        lax.fori_loop(0, tm, issue, 0)

    @pl.when(i == 0)
    def _():
        gather(dest_ref, 0)

    @pl.when(i + 1 < n)
    def _():
        gather(dnext_ref, 1 - slot)

    hb = hb_ref[...]
    gate = _dot(hb, sg_ref[...])
    up = _dot(hb, su_ref[...])
    shared = _dot((gate * _sigmoid(gate) * up).astype(BF16), sd_ref[...])

    for k in range(TOP_K):
        pltpu.make_async_copy(ys_ref.at[pl.ds(0, tm * nj)], buf_ref.at[slot, k], sem.at[slot]).wait()

    def reduce_rows(c, carry):
        r0 = pl.multiple_of(c * COMBINE_ROWS, COMBINE_ROWS)
        w = w_ref[pl.ds(r0, COMBINE_ROWS), :]
        lo = [None] * nj
        hi = [None] * nj
        for k in range(TOP_K):
            wk = jnp.broadcast_to(w[:, k:k + 1], (COMBINE_ROWS, LANES))
            for j in range(nj):
                l, h = _unpack_halves(buf_ref[slot, k, pl.ds(r0 * nj + j, COMBINE_ROWS, stride=nj), :])
                lo[j] = l * wk if k == 0 else lo[j] + l * wk
                hi[j] = h * wk if k == 0 else hi[j] + h * wk
        acc_ref[pl.ds(r0, COMBINE_ROWS), :] = jnp.concatenate(lo + hi, axis=-1)
        return carry

    lax.fori_loop(0, tm // COMBINE_ROWS, reduce_rows, 0)
    ffn = acc_ref[...] + shared
    out = _layer_norm(alpha * x1_ref[...] + g2_ref[0] * ffn, lg_ref[...], lb_ref[...])

    @pl.when(i < half)
    def _():
        op_ref[...] = out

    @pl.when(i >= half)
    def _():
        os_ref[...] = out


def _combine(dest_flat, w, x1, hb, gate2_rows, y_sorted, sh_gate, sh_up, sh_down, ln_g, ln_b, alpha, nj, tm, ls):
    t, d = hb.shape
    ff = sh_gate.shape[1]
    n = t // tm
    half = n // 2
    per_chain = ls // tm
    assert tm % COMBINE_ROWS == 0
    c2 = lambda i: (0, 0)
    pmap = lambda i: (jnp.minimum(i, half - 1), 0)
    smap = lambda i: (jnp.maximum(i - half, 0), 0)
    return pl.pallas_call(
        functools.partial(_combine_kernel, alpha=alpha, half=half, nj=nj),
        grid=(n,),
        in_specs=[pl.BlockSpec((tm * TOP_K,), lambda i: (i,), memory_space=pltpu.SMEM),
                  pl.BlockSpec((tm * TOP_K,), lambda i: (jnp.minimum(i + 1, n - 1),), memory_space=pltpu.SMEM),
                  pl.BlockSpec((tm, TOP_K), lambda i: (i, 0)),
                  pl.BlockSpec((tm, d), lambda i: (i, 0)),
                  pl.BlockSpec((tm, d), lambda i: (i, 0)),
                  pl.BlockSpec((1, 1, d), lambda i: (i // per_chain, 0, 0)),
                  _resident((d, ff)), _resident((d, ff)), _resident((ff, d)),
                  pl.BlockSpec((1, d), c2), pl.BlockSpec((1, d), c2),
                  pl.BlockSpec(memory_space=pl.ANY)],
        out_specs=[pl.BlockSpec((tm, d), pmap), pl.BlockSpec((tm, d), smap)],
        out_shape=[jax.ShapeDtypeStruct((t // 2, d), F32), jax.ShapeDtypeStruct((t // 2, d), F32)],
        scratch_shapes=[pltpu.VMEM((2, TOP_K, tm * nj, LANES), U32), pltpu.VMEM((tm, d), F32),
                        pltpu.SemaphoreType.DMA((2,))],
        compiler_params=_cparams(("arbitrary",)),
        name="combine",
    )(dest_flat, dest_flat, w, x1, hb, gate2_rows, sh_gate, sh_up, sh_down, ln_g, ln_b, y_sorted)


def _tile(pref, n, mult=8):
    t = min(pref, n)
    while n % t or t % mult:
        t -= 1
    return t


def _encoder_layer(xp, xs, c_all, chain_seq, seg, alpha, p):
    _, ls, d = xp.shape
    cc = p["conv_w"].shape[-1]

    ada = _ada(c_all, p["w_ada"], p["b_ada"])
    mods = ada.reshape(ada.shape[0], N_ADA, d)[chain_seq].reshape(2, NCHAIN, N_ADA, 1, d)
    shift1, scale1, gate1, shift2, scale2, gate2 = (mods[:, :, k] for k in range(N_ADA))

    w_in = p["w_in"].astype(BF16)
    b_in = p["b_in"].reshape(1, -1)
    u, s_in = _inproj(xp, xs, scale1, shift1, w_in[:, :cc], w_in[:, cc:2 * cc], w_in[:, 2 * cc:],
                      b_in[:, :cc], b_in[:, cc:2 * cc], b_in[:, 2 * cc:], _tile(TILES["inproj"], ls))

    chain = np.arange(NCHAIN)
    prev_ok = jnp.asarray(np.stack([(chain % seg != 0), np.zeros(NCHAIN, bool)]).astype(np.float32)[..., None])
    next_ok = jnp.asarray(np.stack([(chain % seg != seg - 1), np.zeros(NCHAIN, bool)]).astype(np.float32)[..., None])
    conv_out = _conv(u, prev_ok, next_ok, p["conv_w"], p["conv_b"], p["conv_ln_g"], p["conv_ln_b"],
                     _tile(TILES["conv"], ls, CONV_HALO))

    pf = _s5_params(p["ssm_b_re"], p["ssm_b_im"], p["ssm_a_re_f"], p["ssm_a_im_f"], p["ssm_log_dt_f"],
                    p["ssm_c_re_f"], p["ssm_c_im_f"])
    pb = _s5_params(p["ssm_b_re"], p["ssm_b_im"], p["ssm_a_re_b"], p["ssm_a_im_b"], p["ssm_log_dt_b"],
                    p["ssm_c_re_b"], p["ssm_c_im_b"])
    yf, yb = _s5_both(s_in, seg, pf, pb, _tile(TILES["s5"], ls))
    cs = s_in.shape[-1]
    rows_tm = 2 * ls * NCHAIN
    flat = lambda a: a.reshape(rows_tm, cs)
    ssm_out = _glu(flat(yf), flat(yb), flat(s_in), p["ssm_d"].reshape(1, cs),
                   p["ssm_glu_w"].astype(BF16), p["ssm_glu_b"].reshape(1, cs), _tile(TILES["glu"], rows_tm))
    ssm_out = ssm_out.reshape(2, ls * NCHAIN, cs)

    w_out = p["w_out"].astype(BF16)
    rw = p["router_w"].astype(F32)
    rw_hi = rw.astype(BF16)
    rw_lo = (rw - rw_hi.astype(F32)).astype(BF16)
    row = lambda v: v.reshape(1, -1)
    x1, hb, hp, logits = _outproj(xp, xs, conv_out, ssm_out, gate1, scale2, shift2, w_out[:cc], w_out[cc:],
                                  row(p["b_out"]), row(p["ln1_g"]), row(p["ln1_b"]), rw_hi, rw_lo,
                                  alpha, _tile(TILES["outproj"], ls, 16))

    t = 2 * NCHAIN * ls
    ne = rw.shape[1]
    nj = d // 2 // LANES
    e_idx, e_w, rank, counts = _route(logits.reshape(t, ne), row(p["router_bias"]).astype(F32),
                                      _tile(TILES["route"], t))

    tm_e = _tile(TILES["expert"], t)
    n_rows = t * TOP_K + ne * tm_e
    counts = counts.reshape(ne).astype(I32)
    padded = (counts + tm_e - 1) // tm_e * tm_e
    pad_end = jnp.cumsum(padded)
    pad_start = pad_end - padded
    n_tiles = n_rows // tm_e
    tile_e = jnp.minimum(jnp.searchsorted(pad_end, jnp.arange(n_tiles, dtype=I32) * tm_e, side="right"),
                         ne - 1).astype(I32)
    n_used = (pad_end[-1:] // tm_e).astype(I32)
    last_tile_row = jnp.where(counts > 0, pad_end - tm_e, -1).astype(I32)
    dest = _dest(e_idx, rank, pad_start.astype(F32).reshape(1, ne), _tile(TILES["dest"], t)).reshape(t * TOP_K)

    x_sorted = _dispatch(last_tile_row, n_used, dest, hp.reshape(t * nj, LANES), n_rows, nj, tm_e,
                         _tile(TILES["dispatch"], ls))
    y_sorted = _experts(tile_e, n_used, x_sorted, p["exp_w_gate"], p["exp_w_up"], p["exp_w_down"], nj, tm_e)
    tm_c = _tile(TILES["combine"], ls, COMBINE_ROWS)
    gate2_rows = gate2.reshape(2 * NCHAIN, 1, d)
    yp, ys = _combine(dest, e_w, x1.reshape(t, d), hb.reshape(t, d), gate2_rows, y_sorted,
                      p["sh_w_gate"].astype(BF16), p["sh_w_up"].astype(BF16), p["sh_w_down"].astype(BF16),
                      row(p["ln2_g"]), row(p["ln2_b"]), alpha, nj, tm_c, ls)
    return yp.reshape(NCHAIN, ls, d), ys.reshape(NCHAIN, ls, d)


_PARAM_NAMES = ("w_ada", "b_ada", "w_in", "b_in", "conv_w", "conv_b", "conv_ln_g", "conv_ln_b",
                "ssm_b_re", "ssm_b_im", "ssm_a_re_f", "ssm_a_im_f", "ssm_log_dt_f",
                "ssm_a_re_b", "ssm_a_im_b", "ssm_log_dt_b", "ssm_c_re_f", "ssm_c_im_f",
                "ssm_c_re_b", "ssm_c_im_b", "ssm_d", "ssm_glu_w", "ssm_glu_b", "w_out", "b_out",
                "ln1_g", "ln1_b", "router_w", "router_bias", "exp_w_gate", "exp_w_up", "exp_w_down",
                "sh_w_gate", "sh_w_up", "sh_w_down", "ln2_g", "ln2_b")


def kernel(x_prompt, x_sample, c_prompt, c_sample, w_ada, b_ada, w_in, b_in, conv_w, conv_b, conv_ln_g, conv_ln_b, ssm_b_re, ssm_b_im, ssm_a_re_f, ssm_a_im_f, ssm_log_dt_f, ssm_a_re_b, ssm_a_im_b, ssm_log_dt_b, ssm_c_re_f, ssm_c_im_f, ssm_c_re_b, ssm_c_im_b, ssm_d, ssm_glu_w, ssm_glu_b, w_out, b_out, ln1_g, ln1_b, router_w, router_bias, exp_w_gate, exp_w_up, exp_w_down, sh_w_gate, sh_w_up, sh_w_down, ln2_g, ln2_b):
    stacked = (w_ada, b_ada, w_in, b_in, conv_w, conv_b, conv_ln_g, conv_ln_b, ssm_b_re, ssm_b_im,
               ssm_a_re_f, ssm_a_im_f, ssm_log_dt_f, ssm_a_re_b, ssm_a_im_b, ssm_log_dt_b, ssm_c_re_f,
               ssm_c_im_f, ssm_c_re_b, ssm_c_im_b, ssm_d, ssm_glu_w, ssm_glu_b, w_out, b_out, ln1_g, ln1_b,
               router_w, router_bias, exp_w_gate, exp_w_up, exp_w_down, sh_w_gate, sh_w_up, sh_w_down,
               ln2_g, ln2_b)
    depth = w_ada.shape[0]
    alpha = (2 * depth) ** 0.25
    bp, lp, d = x_prompt.shape
    bs, lsample, _ = x_sample.shape
    assert NCHAIN % bp == 0 and bs == NCHAIN and lp % (NCHAIN // bp) == 0
    seg = NCHAIN // bp
    ls = lp // seg
    assert ls == lsample

    c_all = jnp.concatenate([c_prompt, c_sample], axis=0)
    c_all = jnp.pad(c_all, ((0, -c_all.shape[0] % 8), (0, 0)))
    chain_seq = np.concatenate([np.arange(NCHAIN) // seg, bp + np.arange(NCHAIN)])

    xp = x_prompt.reshape(NCHAIN, ls, d)
    xs = x_sample.reshape(NCHAIN, ls, d)
    for l in range(depth):
        params = {n: v[l] for n, v in zip(_PARAM_NAMES, stacked)}
        xp, xs = _encoder_layer(xp, xs, c_all, chain_seq, seg, alpha, params)
    return xp.reshape(bp, lp, d), xs.reshape(bs, lsample, d)
```

```python
import functools
import math

import numpy as np
import jax
import jax.numpy as jnp
from jax import lax
from jax.experimental import pallas as pl
from jax.experimental.pallas import tpu as pltpu

F32 = jnp.float32
BF16 = jnp.bfloat16
U32 = jnp.uint32
I32 = jnp.int32

LN_EPS = 1e-5
N_ADA = 6
TOP_K = 8
N_EXPERT_GROUPS = 8
TOPK_GROUPS = 4
ROUTED_SCALE = 2.5

NCHAIN = 8
LANES = 128
MXU_DIM = 256
VMEM_LIMIT = 56 * 1024 * 1024

TILES = dict(inproj=64, conv=128, s5=128, glu=1024, outproj=32, route=512, dest=1024, expert=512,
             dispatch=512, combine=256)


def _cparams(sem):
    return pltpu.CompilerParams(dimension_semantics=sem, vmem_limit_bytes=VMEM_LIMIT)


def _resident(shape):
    zeros = (0,) * len(shape)
    return pl.BlockSpec(shape, lambda *_: zeros, pipeline_mode=pl.Buffered(1))


def _dot(a, b):
    return jnp.dot(a, b, preferred_element_type=F32)


def _sigmoid(x):
    return jax.nn.sigmoid(x)


def _layer_norm(v, g, b):
    mu = jnp.mean(v, axis=-1, keepdims=True)
    d = v - mu
    var = jnp.mean(d * d, axis=-1, keepdims=True)
    return d * lax.rsqrt(var + LN_EPS) * g + b


def _ada_kernel(c_ref, w_ref, b_ref, o_ref):
    c = c_ref[...]
    s = (c * _sigmoid(c)).astype(BF16)
    o_ref[...] = _dot(s, w_ref[...].astype(BF16)) + b_ref[...]


def _ada(c_all, w_ada, b_ada):
    rows, d = c_all.shape
    n = w_ada.shape[1]
    tn = _tile(2048, n, LANES)
    return pl.pallas_call(
        _ada_kernel,
        grid=(n // tn,),
        in_specs=[pl.BlockSpec((rows, d), lambda j: (0, 0)),
                  pl.BlockSpec((d, tn), lambda j: (0, j)),
                  pl.BlockSpec((1, tn), lambda j: (0, j))],
        out_specs=pl.BlockSpec((rows, tn), lambda j: (0, j)),
        out_shape=jax.ShapeDtypeStruct((rows, n), F32),
        compiler_params=_cparams(("arbitrary",)),
        name="ada",
    )(c_all, w_ada, b_ada.reshape(1, n))


def _x_specs(tt, d, nt):
    xp = pl.BlockSpec((NCHAIN, tt, d), lambda g, t: (0, jnp.where(g == 0, t, nt - 1), 0))
    xs = pl.BlockSpec((NCHAIN, tt, d), lambda g, t: (0, jnp.where(g == 1, t, 0), 0))
    return xp, xs


def _mod_spec(d):
    return pl.BlockSpec((1, NCHAIN, 1, d), lambda g, t: (g, 0, 0, 0))


def _inproj_kernel(xp_ref, xs_ref, sc_ref, sh_ref, wa_ref, wg_ref, ws_ref, ba_ref, bg_ref, bs_ref,
                   u_ref, s_ref):
    g = pl.program_id(0)
    x = jnp.where(g == 0, xp_ref[...], xs_ref[...])
    h = x * (1.0 + sc_ref[0]) + sh_ref[0]
    tt = h.shape[1]
    ht = pltpu.einshape("ctd->tcd", h).reshape(tt * NCHAIN, h.shape[2]).astype(BF16)
    a = _dot(ht, wa_ref[...]) + ba_ref[...]
    gt = _dot(ht, wg_ref[...]) + bg_ref[...]
    u = a * _sigmoid(gt)
    s = _dot(ht, ws_ref[...]) + bs_ref[...]
    u_ref[0] = u.reshape(tt, NCHAIN, u.shape[-1])
    s_ref[0] = s.reshape(tt, NCHAIN, s.shape[-1])


def _inproj(xp, xs, scale1, shift1, wa, wg, ws, ba, bg, bs, tt):
    _, ls, d = xp.shape
    cc, cs = wa.shape[1], ws.shape[1]
    nt = ls // tt
    xp_spec, xs_spec = _x_specs(tt, d, nt)
    const2 = lambda g, t: (0, 0)
    return pl.pallas_call(
        _inproj_kernel,
        grid=(2, nt),
        in_specs=[xp_spec, xs_spec, _mod_spec(d), _mod_spec(d),
                  _resident((d, cc)), _resident((d, cc)), _resident((d, cs)),
                  pl.BlockSpec((1, cc), const2), pl.BlockSpec((1, cc), const2), pl.BlockSpec((1, cs), const2)],
        out_specs=[pl.BlockSpec((1, tt, NCHAIN, cc), lambda g, t: (g, t, 0, 0)),
                   pl.BlockSpec((1, tt, NCHAIN, cs), lambda g, t: (g, t, 0, 0))],
        out_shape=[jax.ShapeDtypeStruct((2, ls, NCHAIN, cc), F32),
                   jax.ShapeDtypeStruct((2, ls, NCHAIN, cs), F32)],
        compiler_params=_cparams(("arbitrary", "arbitrary")),
        name="inproj",
    )(xp, xs, scale1, shift1, wa, wg, ws, ba, bg, bs)


CONV_HALO = 16
CONV_BLOCK = 8


def _conv_kernel(u_ref, up_ref, un_ref, pok_ref, nok_ref, w_ref, cb_ref, g_ref, b_ref, o_ref,
                 buf_ref, acc_ref, *, width):
    t = pl.program_id(1)
    nt = pl.num_programs(1)
    tt = u_ref.shape[1]
    pad = (width - 1) // 2
    prev = up_ref[0]
    prev_wrapped = pltpu.roll(prev, shift=1, axis=1) * pok_ref[0]
    buf_ref[0:CONV_HALO] = jnp.where(t == 0, prev_wrapped, prev)
    buf_ref[CONV_HALO:CONV_HALO + tt] = u_ref[0]
    nxt = un_ref[0]
    next_wrapped = pltpu.roll(nxt, shift=NCHAIN - 1, axis=1) * nok_ref[0]
    buf_ref[CONV_HALO + tt:2 * CONV_HALO + tt] = jnp.where(t == nt - 1, next_wrapped, nxt)

    base = CONV_HALO - pad

    for c0 in range(0, u_ref.shape[3], LANES):
        lanes = slice(c0, c0 + LANES)
        taps = [w_ref[k, :, lanes] for k in range(width)]
        bias = cb_ref[:, lanes]

        def body(b, carry, lanes=lanes, taps=taps, bias=bias):
            t0 = b * CONV_BLOCK
            acc = [bias] * CONV_BLOCK
            for s in range(CONV_BLOCK + width - 1):
                x = buf_ref[t0 + base + s, :, lanes]
                for i in range(CONV_BLOCK):
                    if 0 <= s - i < width:
                        acc[i] = acc[i] + x * taps[s - i]
            for i in range(CONV_BLOCK):
                acc_ref[t0 + i, :, lanes] = acc[i]
            return carry

        lax.fori_loop(0, tt // CONV_BLOCK, body, 0)
    v = _layer_norm(acc_ref[...], g_ref[...], b_ref[...])
    v = v * _sigmoid(v)
    o_ref[0] = v.reshape(tt * NCHAIN, v.shape[-1]).astype(o_ref.dtype)


def _conv(u, prev_ok, next_ok, conv_w, conv_b, ln_g, ln_b, tt):
    _, ls, _, cc = u.shape
    width = conv_w.shape[0]
    assert (width - 1) // 2 <= CONV_HALO and tt % CONV_HALO == 0 and cc % LANES == 0
    nt = ls // tt
    hb = tt // CONV_HALO
    nh = ls // CONV_HALO
    wb = jnp.broadcast_to(conv_w[:, None, :], (width, NCHAIN, cc))
    row = lambda v: jnp.broadcast_to(v[None, :], (NCHAIN, cc))
    const2 = lambda g, t: (0, 0)
    return pl.pallas_call(
        functools.partial(_conv_kernel, width=width),
        grid=(2, nt),
        in_specs=[pl.BlockSpec((1, tt, NCHAIN, cc), lambda g, t: (g, t, 0, 0)),
                  pl.BlockSpec((1, CONV_HALO, NCHAIN, cc),
                               lambda g, t: (g, jnp.where(t == 0, nh - 1, t * hb - 1), 0, 0)),
                  pl.BlockSpec((1, CONV_HALO, NCHAIN, cc),
                               lambda g, t: (g, jnp.where(t == nt - 1, 0, (t + 1) * hb), 0, 0)),
                  pl.BlockSpec((1, NCHAIN, 1), lambda g, t: (g, 0, 0)),
                  pl.BlockSpec((1, NCHAIN, 1), lambda g, t: (g, 0, 0)),
                  pl.BlockSpec((width, NCHAIN, cc), lambda g, t: (0, 0, 0)),
                  pl.BlockSpec((NCHAIN, cc), const2), pl.BlockSpec((NCHAIN, cc), const2),
                  pl.BlockSpec((NCHAIN, cc), const2)],
        out_specs=pl.BlockSpec((1, tt * NCHAIN, cc), lambda g, t: (g, t, 0)),
        out_shape=jax.ShapeDtypeStruct((2, ls * NCHAIN, cc), BF16),
        scratch_shapes=[pltpu.VMEM((tt + 2 * CONV_HALO, NCHAIN, cc), F32),
                        pltpu.VMEM((tt, NCHAIN, cc), F32)],
        compiler_params=_cparams(("arbitrary", "arbitrary")),
        name="conv",
    )(u, u, u, prev_ok, next_ok, wb, row(conv_b), row(ln_g), row(ln_b))


S5_LANE_BLOCK = 512
S5_UNROLL = 4


def _s5_kernel(u_ref, wb_ref, wc_ref, are_ref, aim_ref, s0_ref, *rest, reverse, emit_y):
    if emit_y:
        y_ref, sfin_ref, bu_ref, st_ref, carry_ref = rest
    else:
        sfin_ref, bu_ref, carry_ref = rest
        st_ref = None
    i = pl.program_id(1)
    tt = u_ref.shape[1]
    nk, ks, sw2 = wb_ref.shape
    sw = sw2 // 2

    @pl.when(i == 0)
    def _():
        carry_ref[...] = s0_ref[0]

    u2 = u_ref[0].reshape(tt * NCHAIN, u_ref.shape[3]).astype(BF16)
    for kc in range(nk):
        bu_ref[...] = _dot(u2[:, kc * ks:(kc + 1) * ks], wb_ref[kc]).reshape(tt, NCHAIN, sw2)
        for lo in range(0, sw, S5_LANE_BLOCK):
            lb = min(S5_LANE_BLOCK, sw - lo)
            re_sl = slice(lo, lo + lb)
            im_sl = slice(sw + lo, sw + lo + lb)
            ar = are_ref[kc, :, re_sl]
            ai = aim_ref[kc, :, re_sl]

            def step(j, carry, re_sl=re_sl, im_sl=im_sl, ar=ar, ai=ai):
                sre, sim = carry
                for q in range(S5_UNROLL):
                    jj = j * S5_UNROLL + q
                    tloc = tt - 1 - jj if reverse else jj
                    nre = ar * sre - ai * sim + bu_ref[tloc, :, re_sl]
                    nim = ar * sim + ai * sre + bu_ref[tloc, :, im_sl]
                    if emit_y:
                        st_ref[tloc, :, re_sl] = nre
                        st_ref[tloc, :, im_sl] = nim
                    sre, sim = nre, nim
                return sre, sim

            sre, sim = lax.fori_loop(0, tt // S5_UNROLL, step,
                                     (carry_ref[kc, :, re_sl], carry_ref[kc, :, im_sl]))
            carry_ref[kc, :, re_sl] = sre
            carry_ref[kc, :, im_sl] = sim
        if emit_y:
            st = st_ref[...].reshape(tt * NCHAIN, sw2).astype(BF16)
            y_ref[0, :, :, kc * ks:(kc + 1) * ks] = _dot(st, wc_ref[kc]).reshape(tt, NCHAIN, ks)
    sfin_ref[0] = carry_ref[...]


def _s5_pass(s_in, ngroups, wb, wc, are, aim, s0, tt, reverse, emit_y):
    _, ls, _, cs = s_in.shape
    nk, ks, sw2 = wb.shape
    assert tt % S5_UNROLL == 0
    nt = ls // tt
    tmap = (lambda i: nt - 1 - i) if reverse else (lambda i: i)
    c3 = lambda g, i: (0, 0, 0)
    state_spec = pl.BlockSpec((1, nk, NCHAIN, sw2), lambda g, i: (g, 0, 0, 0))
    out_specs = [state_spec]
    out_shape = [jax.ShapeDtypeStruct((ngroups, nk, NCHAIN, sw2), F32)]
    scratch = [pltpu.VMEM((tt, NCHAIN, sw2), F32)]
    if emit_y:
        out_specs.insert(0, pl.BlockSpec((1, tt, NCHAIN, cs), lambda g, i: (g, tmap(i), 0, 0)))
        out_shape.insert(0, jax.ShapeDtypeStruct((ngroups, ls, NCHAIN, cs), F32))
        scratch.append(pltpu.VMEM((tt, NCHAIN, sw2), F32))
    scratch.append(pltpu.VMEM((nk, NCHAIN, sw2), F32))
    res = pl.pallas_call(
        functools.partial(_s5_kernel, reverse=reverse, emit_y=emit_y),
        grid=(ngroups, nt),
        in_specs=[pl.BlockSpec((1, tt, NCHAIN, cs), lambda g, i: (g, tmap(i), 0, 0)),
                  _resident((nk, ks, sw2)), _resident((nk, sw2, ks)),
                  _resident((nk, NCHAIN, sw2 // 2)), _resident((nk, NCHAIN, sw2 // 2)),
                  state_spec],
        out_specs=out_specs,
        out_shape=out_shape,
        scratch_shapes=scratch,
        compiler_params=_cparams(("arbitrary", "arbitrary")),
        name="s5_" + ("bwd" if reverse else "fwd") + ("" if emit_y else "_state"),
    )(s_in, wb, wc, are, aim, s0)
    return (res[0], res[1]) if emit_y else (None, res[0])


def _cmul(are, aim, bre, bim):
    return are * bre - aim * bim, are * bim + aim * bre


def _cpow(re, im, n):
    rre, rim = jnp.ones_like(re), jnp.zeros_like(im)
    while n:
        if n & 1:
            rre, rim = _cmul(rre, rim, re, im)
        re, im = _cmul(re, im, re, im)
        n >>= 1
    return rre, rim


def _s5_params(b_re, b_im, a_re, a_im, log_dt, c_re, c_im):
    ng, ns, nh = b_re.shape
    cs = ng * nh
    ks = min(MXU_DIM, cs)
    gps = ks // nh
    nk = cs // ks
    sw = gps * ns
    a_re, a_im = a_re.astype(F32), a_im.astype(F32)
    dt = jnp.exp(log_dt.astype(F32))[:, None]
    mag = jnp.exp(a_re * dt)
    lre, lim = mag * jnp.cos(a_im * dt), mag * jnp.sin(a_im * dt)
    den = a_re * a_re + a_im * a_im
    zre = ((lre - 1.0) * a_re + lim * a_im) / den
    zim = (lim * a_re - (lre - 1.0) * a_im) / den
    cre, cim = _cmul(c_re.astype(F32), c_im.astype(F32), zre[:, None, :], zim[:, None, :])

    eye = jnp.eye(gps, dtype=F32)

    def in_block(b):
        b = b.astype(F32).reshape(nk, gps, ns, nh)
        return jnp.einsum("kgph,gj->kghjp", b, eye).reshape(nk, ks, sw)

    def out_block(c):
        c = c.reshape(nk, gps, nh, ns)
        return jnp.einsum("kghp,gj->kgpjh", c, eye).reshape(nk, sw, ks)

    wb = jnp.concatenate([in_block(b_re), in_block(b_im)], axis=2).astype(BF16)
    wc = jnp.concatenate([out_block(cre), out_block(-cim)], axis=1).astype(BF16)
    bc = lambda v: jnp.broadcast_to(v.reshape(nk, 1, sw), (nk, NCHAIN, sw))
    return dict(wb=wb, wc=wc, are=bc(lre), aim=bc(lim), lre=lre.reshape(nk, sw), lim=lim.reshape(nk, sw))


def _chain_states(local_end, p, seg, ls, reverse):
    nk, _, sw2 = local_end.shape
    sw = sw2 // 2
    pre, pim = _cpow(p["lre"], p["lim"], ls)
    e = local_end.reshape(nk, NCHAIN // seg, seg, sw2)
    ere, eim = e[..., :sw], e[..., sw:]
    zero = jnp.zeros_like(ere[:, :, 0])
    order = range(seg - 1, -1, -1) if reverse else range(seg)
    sre, sim = zero, zero
    out = [None] * seg
    for k in order:
        out[k] = jnp.concatenate([sre, sim], axis=-1)
        mre, mim = _cmul(pre[:, None, :], pim[:, None, :], sre, sim)
        sre, sim = mre + ere[:, :, k], mim + eim[:, :, k]
    return jnp.stack(out, axis=2).reshape(nk, NCHAIN, sw2)


def _s5_both(s_in, seg, pf, pb, tt):
    ls = s_in.shape[1]
    ys = []
    for p, reverse in ((pf, False), (pb, True)):
        zero = jnp.zeros((1, p["wb"].shape[0], NCHAIN, p["wb"].shape[2]), F32)
        s0 = zero
        if seg > 1:
            _, local_end = _s5_pass(s_in, 1, p["wb"], p["wc"], p["are"], p["aim"], zero, tt, reverse, False)
            s0 = _chain_states(local_end[0], p, seg, ls, reverse)[None]
        y, _ = _s5_pass(s_in, 2, p["wb"], p["wc"], p["are"], p["aim"], jnp.concatenate([s0, zero]), tt,
                        reverse, True)
        ys.append(y)
    return ys


def _glu_kernel(yf_ref, yb_ref, s_ref, d_ref, w_ref, b_ref, o_ref):
    y = yf_ref[...] + yb_ref[...] + d_ref[...] * s_ref[...]
    g = jax.nn.gelu(y)
    o_ref[...] = (g * _sigmoid(_dot(g.astype(BF16), w_ref[...]) + b_ref[...])).astype(o_ref.dtype)


def _glu(yf, yb, s_in, d_skip, glu_w, glu_b, tm):
    rows, cs = yf.shape
    rspec = pl.BlockSpec((tm, cs), lambda i: (i, 0))
    c2 = lambda i: (0, 0)
    return pl.pallas_call(
        _glu_kernel,
        grid=(rows // tm,),
        in_specs=[rspec, rspec, rspec, pl.BlockSpec((1, cs), c2), pl.BlockSpec((cs, cs), c2),
                  pl.BlockSpec((1, cs), c2)],
        out_specs=rspec,
        out_shape=jax.ShapeDtypeStruct((rows, cs), BF16),
        compiler_params=_cparams(("arbitrary",)),
        name="s5_glu",
    )(yf, yb, s_in, d_skip, glu_w, glu_b)


def _pack_halves(v):
    n = v.shape[-1] // 2
    bits = lax.bitcast_convert_type(v.astype(BF16).astype(F32), U32)
    return (bits[..., :n] >> 16) | (bits[..., n:] & jnp.uint32(0xFFFF0000))


def _unpack_halves(p):
    return (lax.bitcast_convert_type(p << 16, F32),
            lax.bitcast_convert_type(p & jnp.uint32(0xFFFF0000), F32))


def _outproj_kernel(xp_ref, xs_ref, co_ref, so_ref, g1_ref, sc2_ref, sh2_ref, wo1_ref, wo2_ref, bo_ref,
                    lg_ref, lb_ref, rwh_ref, rwl_ref, x1_ref, hp_ref, lo_ref, *, alpha):
    g = pl.program_id(0)
    x = jnp.where(g == 0, xp_ref[...], xs_ref[...])
    tt, d = x.shape[1], x.shape[2]
    mix = _dot(co_ref[0], wo1_ref[...]) + _dot(so_ref[0], wo2_ref[...]) + bo_ref[...]
    mix = pltpu.einshape("tcd->ctd", mix.reshape(tt, NCHAIN, d))
    x1 = _layer_norm(alpha * x + g1_ref[0] * mix, lg_ref[...], lb_ref[...])
    h2 = x1 * (1.0 + sc2_ref[0]) + sh2_ref[0]
    x1_ref[0] = x1
    hb = h2.astype(BF16)
    h2f = h2.reshape(NCHAIN * tt, d)
    packed = _pack_halves(h2f)
    nj = packed.shape[-1] // LANES
    for j in range(nj):
        hp_ref[0, :, pl.ds(j, tt, stride=nj), :] = (
            packed[:, j * LANES:(j + 1) * LANES].reshape(NCHAIN, tt, LANES))
    hi = hb.reshape(NCHAIN * tt, d)
    lo = (h2f - hi.astype(F32)).astype(BF16)
    logits = _dot(hi, rwh_ref[...]) + (_dot(hi, rwl_ref[...]) + _dot(lo, rwh_ref[...]))
    lo_ref[0] = logits.reshape(NCHAIN, tt, logits.shape[-1])


def _outproj(xp, xs, conv_out, ssm_out, gate1, scale2, shift2, wo1, wo2, b_out, ln_g, ln_b, rw_hi, rw_lo,
             alpha, tt):
    _, ls, d = xp.shape
    cc, cs = wo1.shape[0], wo2.shape[0]
    ne = rw_hi.shape[1]
    nt = ls // tt
    nj = d // 2 // LANES
    xp_spec, xs_spec = _x_specs(tt, d, nt)
    c2 = lambda g, t: (0, 0)
    nat = lambda w: pl.BlockSpec((1, NCHAIN, tt, w), lambda g, t: (g, 0, t, 0))
    return pl.pallas_call(
        functools.partial(_outproj_kernel, alpha=alpha),
        grid=(2, nt),
        in_specs=[xp_spec, xs_spec,
                  pl.BlockSpec((1, tt * NCHAIN, cc), lambda g, t: (g, t, 0)),
                  pl.BlockSpec((1, tt * NCHAIN, cs), lambda g, t: (g, t, 0)),
                  _mod_spec(d), _mod_spec(d), _mod_spec(d),
                  _resident((cc, d)), _resident((cs, d)), pl.BlockSpec((1, d), c2),
                  pl.BlockSpec((1, d), c2), pl.BlockSpec((1, d), c2),
                  _resident((d, ne)), _resident((d, ne))],
        out_specs=[nat(d),
                   pl.BlockSpec((1, NCHAIN, tt * nj, LANES), lambda g, t: (g, 0, t, 0)),
                   nat(ne)],
        out_shape=[jax.ShapeDtypeStruct((2, NCHAIN, ls, d), F32),
                   jax.ShapeDtypeStruct((2, NCHAIN, ls * nj, LANES), U32),
                   jax.ShapeDtypeStruct((2, NCHAIN, ls, ne), F32)],
        compiler_params=_cparams(("arbitrary", "arbitrary")),
        name="outproj",
    )(xp, xs, conv_out, ssm_out, gate1, scale2, shift2, wo1, wo2, b_out, ln_g, ln_b, rw_hi, rw_lo)


def _route_kernel(lg_ref, bias_ref, eid_ref, e_ref, w_ref, r_ref, cnt_ref, carry_ref):
    i = pl.program_id(0)
    tm, ne = lg_ref.shape
    gs = ne // N_EXPERT_GROUPS
    neg = jnp.float32(-jnp.inf)

    @pl.when(i == 0)
    def _():
        carry_ref[...] = jnp.zeros_like(carry_ref)

    scores = _sigmoid(lg_ref[...]).T
    biased = scores + bias_ref[...]
    eid = eid_ref[...]

    def first_max(v, ids):
        m = jnp.max(v, axis=0, keepdims=True)
        idx = jnp.min(jnp.where(v == m, ids, float(ne)), axis=0, keepdims=True)
        return m, idx

    gscore = []
    for q in range(N_EXPERT_GROUPS):
        vg, ids = biased[q * gs:(q + 1) * gs], eid[q * gs:(q + 1) * gs]
        m1, i1 = first_max(vg, ids)
        m2 = jnp.max(jnp.where(ids == i1, neg, vg), axis=0, keepdims=True)
        gscore.append(m1 + m2)
    parts = []
    for q in range(N_EXPERT_GROUPS):
        beaten = jnp.zeros((1, tm), F32)
        for o in range(N_EXPERT_GROUPS):
            if o != q:
                wins = (gscore[o] >= gscore[q]) if o < q else (gscore[o] > gscore[q])
                beaten = beaten + wins.astype(F32)
        parts.append(jnp.where(beaten < TOPK_GROUPS, biased[q * gs:(q + 1) * gs], neg))
    masked = jnp.concatenate(parts, axis=0)

    chosen = jnp.zeros((ne, tm), F32)
    hits, e_rows, w_rows = [], [], []
    for k in range(TOP_K):
        _, idx = first_max(masked, eid)
        hit = eid == idx
        hits.append(hit)
        w_rows.append(jnp.sum(jnp.where(hit, scores, 0.0), axis=0, keepdims=True))
        e_rows.append(idx)
        masked = jnp.where(hit, neg, masked)
        chosen = jnp.where(hit, 1.0, chosen)
    w_t = jnp.concatenate(w_rows, axis=0)
    w_ref[...] = w_t / jnp.sum(w_t, axis=0, keepdims=True) * ROUTED_SCALE
    e_ref[...] = jnp.concatenate(e_rows, axis=0).astype(I32)

    rr = lax.broadcasted_iota(I32, (tm, tm), 0)
    cc = lax.broadcasted_iota(I32, (tm, tm), 1)
    before = (rr < cc).astype(BF16)
    prefix = _dot(chosen.astype(BF16), before) + carry_ref[...]
    r_rows = [jnp.sum(jnp.where(hits[k], prefix, 0.0), axis=0, keepdims=True) for k in range(TOP_K)]
    r_ref[...] = jnp.concatenate(r_rows, axis=0).astype(I32)
    carry_ref[...] = carry_ref[...] + jnp.sum(chosen, axis=1, keepdims=True)
    cnt_ref[...] = carry_ref[...]


def _route(logits, bias, tm):
    t, ne = logits.shape
    kspec = pl.BlockSpec((TOP_K, tm), lambda i: (0, i))
    col = pl.BlockSpec((ne, 1), lambda i: (0, 0))
    eid = jnp.broadcast_to(jnp.arange(ne, dtype=F32)[:, None], (ne, tm))
    return pl.pallas_call(
        _route_kernel,
        grid=(t // tm,),
        in_specs=[pl.BlockSpec((tm, ne), lambda i: (i, 0)), col, _resident((ne, tm))],
        out_specs=[kspec, kspec, kspec, col],
        out_shape=[jax.ShapeDtypeStruct((TOP_K, t), I32), jax.ShapeDtypeStruct((TOP_K, t), F32),
                   jax.ShapeDtypeStruct((TOP_K, t), I32), jax.ShapeDtypeStruct((ne, 1), F32)],
        scratch_shapes=[pltpu.VMEM((ne, 1), F32)],
        compiler_params=_cparams(("arbitrary",)),
        name="route",
    )(logits, bias.reshape(ne, 1), eid)


def _dest_kernel(e_ref, r_ref, ps_ref, eid_ref, o_ref):
    e = e_ref[...].astype(F32)
    starts = [jnp.sum(jnp.where(eid_ref[...] == e[k:k + 1], ps_ref[...], 0.0), axis=0, keepdims=True)
              for k in range(TOP_K)]
    o_ref[...] = r_ref[...] + jnp.concatenate(starts, axis=0).astype(I32)


def _dest(e_idx, rank, pad_start, tm):
    t = e_idx.shape[1]
    ne = pad_start.shape[0]
    kspec = pl.BlockSpec((TOP_K, tm), lambda i: (0, i))
    eid = jnp.broadcast_to(jnp.arange(ne, dtype=F32)[:, None], (ne, tm))
    return pl.pallas_call(
        _dest_kernel,
        grid=(t // tm,),
        in_specs=[kspec, kspec, pl.BlockSpec((ne, 1), lambda i: (0, 0)), _resident((ne, tm))],
        out_specs=kspec,
        out_shape=jax.ShapeDtypeStruct((TOP_K, t), I32),
        compiler_params=_cparams(("arbitrary",)),
        name="dest",
    )(e_idx, rank, pad_start, eid)


def _dispatch_kernel(last_ref, nu_ref, dest_ref, h_ref, sg_ref, su_ref, sd_ref, xs_ref, sh_ref, zero_ref, sem, zsem,
                     *, nj, tile_rows):
    i = pl.program_id(0)
    tm = h_ref.shape[0] // nj
    ne = last_ref.shape[0]

    @pl.when(i == 0)
    def _():
        zero_ref[...] = jnp.zeros_like(zero_ref)

        def fill(row):
            start = pl.multiple_of(row * nj, 8)
            return pltpu.make_async_copy(zero_ref, xs_ref.at[pl.ds(start, tile_rows * nj)], zsem)

        def issue(e, c):
            @pl.when(last_ref[e] >= 0)
            def _():
                fill(last_ref[e]).start()
            return c

        def drain(e, c):
            @pl.when(last_ref[e] >= 0)
            def _():
                fill(last_ref[e]).wait()
            return c

        lax.fori_loop(0, ne, issue, 0)
        lax.fori_loop(0, ne, drain, 0)
        n_tiles = xs_ref.shape[0] // (tile_rows * nj)
        lax.fori_loop(nu_ref[0], n_tiles, lambda q, c: (fill(q * tile_rows).start(), c)[1], 0)
        lax.fori_loop(nu_ref[0], n_tiles, lambda q, c: (fill(q * tile_rows).wait(), c)[1], 0)

    def issue_rows(r, c):
        src = h_ref.at[pl.ds(pl.multiple_of(r * nj, nj), nj)]
        for k in range(TOP_K):
            dst = xs_ref.at[pl.ds(pl.multiple_of(dest_ref[r * TOP_K + k] * nj, nj), nj)]
            pltpu.make_async_copy(src, dst, sem).start()
        return c

    lax.fori_loop(0, tm, issue_rows, 0)

    x = _unpack_rows(h_ref, tm, nj)
    gate = _dot(x, sg_ref[...])
    up = _dot(x, su_ref[...])
    sh_ref[...] = _dot((gate * _sigmoid(gate) * up).astype(BF16), sd_ref[...]).astype(sh_ref.dtype)

    for _ in range(TOP_K):
        pltpu.make_async_copy(h_ref, xs_ref.at[pl.ds(0, tm * nj)], sem).wait()


def _dispatch(last_tile_row, n_used, dest_flat, h_packed, sh_gate, sh_up, sh_down, n_rows, nj, tile_rows, tm):
    t = h_packed.shape[0] // nj
    d, ff = sh_gate.shape
    grid_spec = pltpu.PrefetchScalarGridSpec(
        num_scalar_prefetch=2,
        grid=(t // tm,),
        in_specs=[pl.BlockSpec((tm * TOP_K,), lambda i, last, nu: (i,), memory_space=pltpu.SMEM),
                  pl.BlockSpec((tm * nj, LANES), lambda i, last, nu: (i, 0)),
                  _resident((d, ff)), _resident((d, ff)), _resident((ff, d))],
        out_specs=[pl.BlockSpec(memory_space=pl.ANY), pl.BlockSpec((tm, d), lambda i, last, nu: (i, 0))],
        scratch_shapes=[pltpu.VMEM((tile_rows * nj, LANES), U32), pltpu.SemaphoreType.DMA,
                        pltpu.SemaphoreType.DMA],
    )
    return pl.pallas_call(
        functools.partial(_dispatch_kernel, nj=nj, tile_rows=tile_rows),
        grid_spec=grid_spec,
        out_shape=[jax.ShapeDtypeStruct((n_rows * nj, LANES), U32), jax.ShapeDtypeStruct((t, d), BF16)],
        compiler_params=_cparams(("arbitrary",)),
        name="dispatch",
    )(last_tile_row, n_used, dest_flat, h_packed, sh_gate, sh_up, sh_down)


def _unpack_rows(p_ref, tm, nj):
    lo, hi = [], []
    for j in range(nj):
        l, h = _unpack_halves(p_ref[pl.ds(j, tm, stride=nj), :])
        lo.append(l.astype(BF16))
        hi.append(h.astype(BF16))
    return jnp.concatenate(lo + hi, axis=-1)


def _experts_kernel(te_ref, nu_ref, x_ref, wg_ref, wu_ref, wd_ref, y_ref, wgb_ref, wub_ref, wdb_ref, *, nj):
    i = pl.program_id(0)
    tm = x_ref.shape[0] // nj
    prev = te_ref[jnp.maximum(i - 1, 0)]

    @pl.when((i == 0) | (te_ref[i] != prev))
    def _():
        wgb_ref[...] = wg_ref[0].astype(BF16)
        wub_ref[...] = wu_ref[0].astype(BF16)
        wdb_ref[...] = wd_ref[0].astype(BF16)

    @pl.when(i < nu_ref[0])
    def _():
        x = _unpack_rows(x_ref, tm, nj)
        gate = _dot(x, wgb_ref[...])
        up = _dot(x, wub_ref[...])
        act = (gate * _sigmoid(gate) * up).astype(BF16)
        y = _dot(act, wdb_ref[...])
        packed = _pack_halves(y)
        for j in range(nj):
            y_ref[pl.ds(j, tm, stride=nj), :] = packed[:, j * LANES:(j + 1) * LANES]

    @pl.when(i >= nu_ref[0])
    def _():
        y_ref[...] = jnp.zeros_like(y_ref)


def _experts(tile_e, n_used, x_sorted, w_gate, w_up, w_down, nj, tm):
    n_rows = x_sorted.shape[0] // nj
    ne, d, ff = w_gate.shape
    n_tiles = n_rows // tm
    wmap = lambda i, te, nu: (te[i], 0, 0)
    grid_spec = pltpu.PrefetchScalarGridSpec(
        num_scalar_prefetch=2,
        grid=(n_tiles,),
        in_specs=[pl.BlockSpec((tm * nj, LANES), lambda i, te, nu: (jnp.minimum(i, nu[0] - 1), 0)),
                  pl.BlockSpec((1, d, ff), wmap), pl.BlockSpec((1, d, ff), wmap),
                  pl.BlockSpec((1, ff, d), wmap)],
        out_specs=pl.BlockSpec((tm * nj, LANES), lambda i, te, nu: (i, 0)),
        scratch_shapes=[pltpu.VMEM((d, ff), BF16), pltpu.VMEM((d, ff), BF16), pltpu.VMEM((ff, d), BF16)],
    )
    return pl.pallas_call(
        functools.partial(_experts_kernel, nj=nj),
        grid_spec=grid_spec,
        out_shape=jax.ShapeDtypeStruct((n_rows * nj, LANES), U32),
        compiler_params=_cparams(("arbitrary",)),
        name="experts",
    )(tile_e, n_used, x_sorted, w_gate, w_up, w_down)


COMBINE_ROWS = 16


def _combine_kernel(dest_ref, dnext_ref, w_ref, x1_ref, sh_ref, g2_ref, lg_ref, lb_ref,
                    ys_ref, op_ref, os_ref, buf_ref, acc_ref, sem, *, alpha, half, nj):
    i = pl.program_id(0)
    n = pl.num_programs(0)
    tm, d = sh_ref.shape
    slot = i % 2

    def gather(d_ref, s):
        def issue(r, c):
            for k in range(TOP_K):
                src = ys_ref.at[pl.ds(pl.multiple_of(d_ref[r * TOP_K + k] * nj, nj), nj)]
                dst = buf_ref.at[s, k, pl.ds(pl.multiple_of(r * nj, nj), nj)]
                pltpu.make_async_copy(src, dst, sem.at[s]).start()
            return c

        lax.fori_loop(0, tm, issue, 0)

    @pl.when(i == 0)
    def _():
        gather(dest_ref, 0)

    @pl.when(i + 1 < n)
    def _():
        gather(dnext_ref, 1 - slot)

    for k in range(TOP_K):
        pltpu.make_async_copy(ys_ref.at[pl.ds(0, tm * nj)], buf_ref.at[slot, k], sem.at[slot]).wait()

    def reduce_rows(c, carry):
        r0 = pl.multiple_of(c * COMBINE_ROWS, COMBINE_ROWS)
        w = w_ref[pl.ds(r0, COMBINE_ROWS), :]
        lo = [None] * nj
        hi = [None] * nj
        for k in range(TOP_K):
            wk = jnp.broadcast_to(w[:, k:k + 1], (COMBINE_ROWS, LANES))
            for j in range(nj):
                l, h = _unpack_halves(buf_ref[slot, k, pl.ds(r0 * nj + j, COMBINE_ROWS, stride=nj), :])
                lo[j] = l * wk if k == 0 else lo[j] + l * wk
                hi[j] = h * wk if k == 0 else hi[j] + h * wk
        acc_ref[pl.ds(r0, COMBINE_ROWS), :] = jnp.concatenate(lo + hi, axis=-1)
        return carry

    lax.fori_loop(0, tm // COMBINE_ROWS, reduce_rows, 0)
    ffn = acc_ref[...] + sh_ref[...].astype(F32)
    out = _layer_norm(alpha * x1_ref[...] + g2_ref[0] * ffn, lg_ref[...], lb_ref[...])

    @pl.when(i < half)
    def _():
        op_ref[...] = out

    @pl.when(i >= half)
    def _():
        os_ref[...] = out


def _combine(dest_flat, w, x1, shared, gate2_rows, y_sorted, ln_g, ln_b, alpha, nj, tm, ls):
    t, d = shared.shape
    n = t // tm
    half = n // 2
    per_chain = ls // tm
    assert tm % COMBINE_ROWS == 0
    c2 = lambda i: (0, 0)
    pmap = lambda i: (jnp.minimum(i, half - 1), 0)
    smap = lambda i: (jnp.maximum(i - half, 0), 0)
    return pl.pallas_call(
        functools.partial(_combine_kernel, alpha=alpha, half=half, nj=nj),
        grid=(n,),
        in_specs=[pl.BlockSpec((tm * TOP_K,), lambda i: (i,), memory_space=pltpu.SMEM),
                  pl.BlockSpec((tm * TOP_K,), lambda i: (jnp.minimum(i + 1, n - 1),), memory_space=pltpu.SMEM),
                  pl.BlockSpec((tm, TOP_K), lambda i: (i, 0)),
                  pl.BlockSpec((tm, d), lambda i: (i, 0)),
                  pl.BlockSpec((tm, d), lambda i: (i, 0)),
                  pl.BlockSpec((1, 1, d), lambda i: (i // per_chain, 0, 0)),
                  pl.BlockSpec((1, d), c2), pl.BlockSpec((1, d), c2),
                  pl.BlockSpec(memory_space=pl.ANY)],
        out_specs=[pl.BlockSpec((tm, d), pmap), pl.BlockSpec((tm, d), smap)],
        out_shape=[jax.ShapeDtypeStruct((t // 2, d), F32), jax.ShapeDtypeStruct((t // 2, d), F32)],
        scratch_shapes=[pltpu.VMEM((2, TOP_K, tm * nj, LANES), U32), pltpu.VMEM((tm, d), F32),
                        pltpu.SemaphoreType.DMA((2,))],
        compiler_params=_cparams(("arbitrary",)),
        name="combine",
    )(dest_flat, dest_flat, w, x1, shared, gate2_rows, ln_g, ln_b, y_sorted)


def _tile(pref, n, mult=8):
    t = min(pref, n)
    while n % t or t % mult:
        t -= 1
    return t


def _encoder_layer(xp, xs, c_all, chain_seq, seg, alpha, p):
    _, ls, d = xp.shape
    cc = p["conv_w"].shape[-1]

    ada = _ada(c_all, p["w_ada"], p["b_ada"])
    mods = ada.reshape(ada.shape[0], N_ADA, d)[chain_seq].reshape(2, NCHAIN, N_ADA, 1, d)
    shift1, scale1, gate1, shift2, scale2, gate2 = (mods[:, :, k] for k in range(N_ADA))

    w_in = p["w_in"].astype(BF16)
    b_in = p["b_in"].reshape(1, -1)
    u, s_in = _inproj(xp, xs, scale1, shift1, w_in[:, :cc], w_in[:, cc:2 * cc], w_in[:, 2 * cc:],
                      b_in[:, :cc], b_in[:, cc:2 * cc], b_in[:, 2 * cc:], _tile(TILES["inproj"], ls))

    chain = np.arange(NCHAIN)
    prev_ok = jnp.asarray(np.stack([(chain % seg != 0), np.zeros(NCHAIN, bool)]).astype(np.float32)[..., None])
    next_ok = jnp.asarray(np.stack([(chain % seg != seg - 1), np.zeros(NCHAIN, bool)]).astype(np.float32)[..., None])
    conv_out = _conv(u, prev_ok, next_ok, p["conv_w"], p["conv_b"], p["conv_ln_g"], p["conv_ln_b"],
                     _tile(TILES["conv"], ls, CONV_HALO))

    pf = _s5_params(p["ssm_b_re"], p["ssm_b_im"], p["ssm_a_re_f"], p["ssm_a_im_f"], p["ssm_log_dt_f"],
                    p["ssm_c_re_f"], p["ssm_c_im_f"])
    pb = _s5_params(p["ssm_b_re"], p["ssm_b_im"], p["ssm_a_re_b"], p["ssm_a_im_b"], p["ssm_log_dt_b"],
                    p["ssm_c_re_b"], p["ssm_c_im_b"])
    yf, yb = _s5_both(s_in, seg, pf, pb, _tile(TILES["s5"], ls))
    cs = s_in.shape[-1]
    rows_tm = 2 * ls * NCHAIN
    flat = lambda a: a.reshape(rows_tm, cs)
    ssm_out = _glu(flat(yf), flat(yb), flat(s_in), p["ssm_d"].reshape(1, cs),
                   p["ssm_glu_w"].astype(BF16), p["ssm_glu_b"].reshape(1, cs), _tile(TILES["glu"], rows_tm))
    ssm_out = ssm_out.reshape(2, ls * NCHAIN, cs)

    w_out = p["w_out"].astype(BF16)
    rw = p["router_w"].astype(F32)
    rw_hi = rw.astype(BF16)
    rw_lo = (rw - rw_hi.astype(F32)).astype(BF16)
    row = lambda v: v.reshape(1, -1)
    x1, hp, logits = _outproj(xp, xs, conv_out, ssm_out, gate1, scale2, shift2, w_out[:cc], w_out[cc:],
                              row(p["b_out"]), row(p["ln1_g"]), row(p["ln1_b"]), rw_hi, rw_lo,
                              alpha, _tile(TILES["outproj"], ls, 16))

    t = 2 * NCHAIN * ls
    ne = rw.shape[1]
    nj = d // 2 // LANES
    e_idx, e_w, rank, counts = _route(logits.reshape(t, ne), row(p["router_bias"]).astype(F32),
                                      _tile(TILES["route"], t, LANES))

    tm_e = _tile(TILES["expert"], t)
    n_rows = t * TOP_K + ne * tm_e
    counts = counts.reshape(ne).astype(I32)
    e_w = e_w.T
    padded = (counts + tm_e - 1) // tm_e * tm_e
    pad_end = jnp.cumsum(padded)
    pad_start = pad_end - padded
    n_tiles = n_rows // tm_e
    tile_e = jnp.minimum(jnp.searchsorted(pad_end, jnp.arange(n_tiles, dtype=I32) * tm_e, side="right"),
                         ne - 1).astype(I32)
    n_used = (pad_end[-1:] // tm_e).astype(I32)
    last_tile_row = jnp.where(counts > 0, pad_end - tm_e, -1).astype(I32)
    dest = _dest(e_idx, rank, pad_start.astype(F32).reshape(ne, 1), _tile(TILES["dest"], t, LANES))
    dest = dest.T.reshape(t * TOP_K)

    x_sorted, shared = _dispatch(last_tile_row, n_used, dest, hp.reshape(t * nj, LANES),
                                 p["sh_w_gate"].astype(BF16), p["sh_w_up"].astype(BF16),
                                 p["sh_w_down"].astype(BF16), n_rows, nj, tm_e, _tile(TILES["dispatch"], ls))
    y_sorted = _experts(tile_e, n_used, x_sorted, p["exp_w_gate"], p["exp_w_up"], p["exp_w_down"], nj, tm_e)
    tm_c = _tile(TILES["combine"], ls, COMBINE_ROWS)
    gate2_rows = gate2.reshape(2 * NCHAIN, 1, d)
    yp, ys = _combine(dest, e_w, x1.reshape(t, d), shared, gate2_rows, y_sorted,
                      row(p["ln2_g"]), row(p["ln2_b"]), alpha, nj, tm_c, ls)
    return yp.reshape(NCHAIN, ls, d), ys.reshape(NCHAIN, ls, d)


_PARAM_NAMES = ("w_ada", "b_ada", "w_in", "b_in", "conv_w", "conv_b", "conv_ln_g", "conv_ln_b",
                "ssm_b_re", "ssm_b_im", "ssm_a_re_f", "ssm_a_im_f", "ssm_log_dt_f",
                "ssm_a_re_b", "ssm_a_im_b", "ssm_log_dt_b", "ssm_c_re_f", "ssm_c_im_f",
                "ssm_c_re_b", "ssm_c_im_b", "ssm_d", "ssm_glu_w", "ssm_glu_b", "w_out", "b_out",
                "ln1_g", "ln1_b", "router_w", "router_bias", "exp_w_gate", "exp_w_up", "exp_w_down",
                "sh_w_gate", "sh_w_up", "sh_w_down", "ln2_g", "ln2_b")


def kernel(x_prompt, x_sample, c_prompt, c_sample, w_ada, b_ada, w_in, b_in, conv_w, conv_b, conv_ln_g, conv_ln_b, ssm_b_re, ssm_b_im, ssm_a_re_f, ssm_a_im_f, ssm_log_dt_f, ssm_a_re_b, ssm_a_im_b, ssm_log_dt_b, ssm_c_re_f, ssm_c_im_f, ssm_c_re_b, ssm_c_im_b, ssm_d, ssm_glu_w, ssm_glu_b, w_out, b_out, ln1_g, ln1_b, router_w, router_bias, exp_w_gate, exp_w_up, exp_w_down, sh_w_gate, sh_w_up, sh_w_down, ln2_g, ln2_b):
    stacked = (w_ada, b_ada, w_in, b_in, conv_w, conv_b, conv_ln_g, conv_ln_b, ssm_b_re, ssm_b_im,
               ssm_a_re_f, ssm_a_im_f, ssm_log_dt_f, ssm_a_re_b, ssm_a_im_b, ssm_log_dt_b, ssm_c_re_f,
               ssm_c_im_f, ssm_c_re_b, ssm_c_im_b, ssm_d, ssm_glu_w, ssm_glu_b, w_out, b_out, ln1_g, ln1_b,
               router_w, router_bias, exp_w_gate, exp_w_up, exp_w_down, sh_w_gate, sh_w_up, sh_w_down,
               ln2_g, ln2_b)
    depth = w_ada.shape[0]
    alpha = (2 * depth) ** 0.25
    bp, lp, d = x_prompt.shape
    bs, lsample, _ = x_sample.shape
    assert NCHAIN % bp == 0 and bs == NCHAIN and lp % (NCHAIN // bp) == 0
    seg = NCHAIN // bp
    ls = lp // seg
    assert ls == lsample

    c_all = jnp.concatenate([c_prompt, c_sample], axis=0)
    c_all = jnp.pad(c_all, ((0, -c_all.shape[0] % 8), (0, 0)))
    chain_seq = np.concatenate([np.arange(NCHAIN) // seg, bp + np.arange(NCHAIN)])

    xp = x_prompt.reshape(NCHAIN, ls, d)
    xs = x_sample.reshape(NCHAIN, ls, d)
    for l in range(depth):
        params = {n: v[l] for n, v in zip(_PARAM_NAMES, stacked)}
        xp, xs = _encoder_layer(xp, xs, c_all, chain_seq, seg, alpha, params)
    return xp.reshape(bp, lp, d), xs.reshape(bs, lsample, d)
```

```python
import functools
import math

import numpy as np
import jax
import jax.numpy as jnp
from jax import lax
from jax.experimental import pallas as pl
from jax.experimental.pallas import tpu as pltpu

F32 = jnp.float32
BF16 = jnp.bfloat16
U32 = jnp.uint32
I32 = jnp.int32

LN_EPS = 1e-5
N_ADA = 6
TOP_K = 8
N_EXPERT_GROUPS = 8
TOPK_GROUPS = 4
ROUTED_SCALE = 2.5

NCHAIN = 8
LANES = 128
MXU_DIM = 256
VMEM_LIMIT = 56 * 1024 * 1024

TILES = dict(inproj=64, conv=128, s5=128, glu=1024, outproj=32, route=512, dest=1024, expert=512,
             dispatch=512, combine=256)


def _cparams(sem):
    return pltpu.CompilerParams(dimension_semantics=sem, vmem_limit_bytes=VMEM_LIMIT)


def _resident(shape):
    zeros = (0,) * len(shape)
    return pl.BlockSpec(shape, lambda *_: zeros, pipeline_mode=pl.Buffered(1))


def _dot(a, b):
    return jnp.dot(a, b, preferred_element_type=F32)


def _sigmoid(x):
    return jax.nn.sigmoid(x)


def _layer_norm(v, g, b):
    mu = jnp.mean(v, axis=-1, keepdims=True)
    d = v - mu
    var = jnp.mean(d * d, axis=-1, keepdims=True)
    return d * lax.rsqrt(var + LN_EPS) * g + b


def _ada_kernel(c_ref, w_ref, b_ref, o_ref):
    c = c_ref[...]
    s = (c * _sigmoid(c)).astype(BF16)
    o_ref[...] = _dot(s, w_ref[...].astype(BF16)) + b_ref[...]


def _ada(c_all, w_ada, b_ada):
    rows, d = c_all.shape
    n = w_ada.shape[1]
    tn = _tile(2048, n, LANES)
    return pl.pallas_call(
        _ada_kernel,
        grid=(n // tn,),
        in_specs=[pl.BlockSpec((rows, d), lambda j: (0, 0)),
                  pl.BlockSpec((d, tn), lambda j: (0, j)),
                  pl.BlockSpec((1, tn), lambda j: (0, j))],
        out_specs=pl.BlockSpec((rows, tn), lambda j: (0, j)),
        out_shape=jax.ShapeDtypeStruct((rows, n), F32),
        compiler_params=_cparams(("arbitrary",)),
        name="ada",
    )(c_all, w_ada, b_ada.reshape(1, n))


def _x_specs(tt, d, nt):
    xp = pl.BlockSpec((NCHAIN, tt, d), lambda g, t: (0, jnp.where(g == 0, t, nt - 1), 0))
    xs = pl.BlockSpec((NCHAIN, tt, d), lambda g, t: (0, jnp.where(g == 1, t, 0), 0))
    return xp, xs


def _mod_spec(d):
    return pl.BlockSpec((1, NCHAIN, 1, d), lambda g, t: (g, 0, 0, 0))


def _inproj_kernel(xp_ref, xs_ref, sc_ref, sh_ref, wa_ref, wg_ref, ws_ref, ba_ref, bg_ref, bs_ref,
                   u_ref, s_ref):
    g = pl.program_id(0)
    x = jnp.where(g == 0, xp_ref[...], xs_ref[...])
    h = x * (1.0 + sc_ref[0]) + sh_ref[0]
    tt = h.shape[1]
    ht = pltpu.einshape("ctd->tcd", h).reshape(tt * NCHAIN, h.shape[2]).astype(BF16)
    a = _dot(ht, wa_ref[...]) + ba_ref[...]
    gt = _dot(ht, wg_ref[...]) + bg_ref[...]
    u = a * _sigmoid(gt)
    s = _dot(ht, ws_ref[...]) + bs_ref[...]
    u_ref[0] = u.reshape(tt, NCHAIN, u.shape[-1])
    s_ref[0] = s.reshape(tt, NCHAIN, s.shape[-1])


def _inproj(xp, xs, scale1, shift1, wa, wg, ws, ba, bg, bs, tt):
    _, ls, d = xp.shape
    cc, cs = wa.shape[1], ws.shape[1]
    nt = ls // tt
    xp_spec, xs_spec = _x_specs(tt, d, nt)
    const2 = lambda g, t: (0, 0)
    return pl.pallas_call(
        _inproj_kernel,
        grid=(2, nt),
        in_specs=[xp_spec, xs_spec, _mod_spec(d), _mod_spec(d),
                  _resident((d, cc)), _resident((d, cc)), _resident((d, cs)),
                  pl.BlockSpec((1, cc), const2), pl.BlockSpec((1, cc), const2), pl.BlockSpec((1, cs), const2)],
        out_specs=[pl.BlockSpec((1, tt, NCHAIN, cc), lambda g, t: (g, t, 0, 0)),
                   pl.BlockSpec((1, tt, NCHAIN, cs), lambda g, t: (g, t, 0, 0))],
        out_shape=[jax.ShapeDtypeStruct((2, ls, NCHAIN, cc), F32),
                   jax.ShapeDtypeStruct((2, ls, NCHAIN, cs), F32)],
        compiler_params=_cparams(("arbitrary", "arbitrary")),
        name="inproj",
    )(xp, xs, scale1, shift1, wa, wg, ws, ba, bg, bs)


CONV_HALO = 16
CONV_BLOCK = 8


def _conv_kernel(u_ref, up_ref, un_ref, pok_ref, nok_ref, w_ref, cb_ref, g_ref, b_ref, o_ref,
                 buf_ref, acc_ref, *, width):
    t = pl.program_id(1)
    nt = pl.num_programs(1)
    tt = u_ref.shape[1]
    pad = (width - 1) // 2
    prev = up_ref[0]
    prev_wrapped = pltpu.roll(prev, shift=1, axis=1) * pok_ref[0]
    buf_ref[0:CONV_HALO] = jnp.where(t == 0, prev_wrapped, prev)
    buf_ref[CONV_HALO:CONV_HALO + tt] = u_ref[0]
    nxt = un_ref[0]
    next_wrapped = pltpu.roll(nxt, shift=NCHAIN - 1, axis=1) * nok_ref[0]
    buf_ref[CONV_HALO + tt:2 * CONV_HALO + tt] = jnp.where(t == nt - 1, next_wrapped, nxt)

    base = CONV_HALO - pad

    for c0 in range(0, u_ref.shape[3], LANES):
        lanes = slice(c0, c0 + LANES)
        taps = [w_ref[k, :, lanes] for k in range(width)]
        bias = cb_ref[:, lanes]

        def body(b, carry, lanes=lanes, taps=taps, bias=bias):
            t0 = b * CONV_BLOCK
            acc = [bias] * CONV_BLOCK
            for s in range(CONV_BLOCK + width - 1):
                x = buf_ref[t0 + base + s, :, lanes]
                for i in range(CONV_BLOCK):
                    if 0 <= s - i < width:
                        acc[i] = acc[i] + x * taps[s - i]
            for i in range(CONV_BLOCK):
                acc_ref[t0 + i, :, lanes] = acc[i]
            return carry

        lax.fori_loop(0, tt // CONV_BLOCK, body, 0)
    v = _layer_norm(acc_ref[...], g_ref[...], b_ref[...])
    v = v * _sigmoid(v)
    o_ref[0] = v.reshape(tt * NCHAIN, v.shape[-1]).astype(o_ref.dtype)


def _conv(u, prev_ok, next_ok, conv_w, conv_b, ln_g, ln_b, tt):
    _, ls, _, cc = u.shape
    width = conv_w.shape[0]
    assert (width - 1) // 2 <= CONV_HALO and tt % CONV_HALO == 0 and cc % LANES == 0
    nt = ls // tt
    hb = tt // CONV_HALO
    nh = ls // CONV_HALO
    wb = jnp.broadcast_to(conv_w[:, None, :], (width, NCHAIN, cc))
    row = lambda v: jnp.broadcast_to(v[None, :], (NCHAIN, cc))
    const2 = lambda g, t: (0, 0)
    return pl.pallas_call(
        functools.partial(_conv_kernel, width=width),
        grid=(2, nt),
        in_specs=[pl.BlockSpec((1, tt, NCHAIN, cc), lambda g, t: (g, t, 0, 0)),
                  pl.BlockSpec((1, CONV_HALO, NCHAIN, cc),
                               lambda g, t: (g, jnp.where(t == 0, nh - 1, t * hb - 1), 0, 0)),
                  pl.BlockSpec((1, CONV_HALO, NCHAIN, cc),
                               lambda g, t: (g, jnp.where(t == nt - 1, 0, (t + 1) * hb), 0, 0)),
                  pl.BlockSpec((1, NCHAIN, 1), lambda g, t: (g, 0, 0)),
                  pl.BlockSpec((1, NCHAIN, 1), lambda g, t: (g, 0, 0)),
                  pl.BlockSpec((width, NCHAIN, cc), lambda g, t: (0, 0, 0)),
                  pl.BlockSpec((NCHAIN, cc), const2), pl.BlockSpec((NCHAIN, cc), const2),
                  pl.BlockSpec((NCHAIN, cc), const2)],
        out_specs=pl.BlockSpec((1, tt * NCHAIN, cc), lambda g, t: (g, t, 0)),
        out_shape=jax.ShapeDtypeStruct((2, ls * NCHAIN, cc), BF16),
        scratch_shapes=[pltpu.VMEM((tt + 2 * CONV_HALO, NCHAIN, cc), F32),
                        pltpu.VMEM((tt, NCHAIN, cc), F32)],
        compiler_params=_cparams(("arbitrary", "arbitrary")),
        name="conv",
    )(u, u, u, prev_ok, next_ok, wb, row(conv_b), row(ln_g), row(ln_b))


S5_LANE_BLOCK = 512
S5_UNROLL = 4


def _s5_kernel(u_ref, wb_ref, wc_ref, are_ref, aim_ref, s0_ref, *rest, reverse, emit_y):
    if emit_y:
        y_ref, sfin_ref, bu_ref, st_ref, carry_ref = rest
    else:
        sfin_ref, bu_ref, carry_ref = rest
        st_ref = None
    i = pl.program_id(1)
    tt = u_ref.shape[1]
    nk, ks, sw2 = wb_ref.shape
    sw = sw2 // 2

    @pl.when(i == 0)
    def _():
        carry_ref[...] = s0_ref[0]

    u2 = u_ref[0].reshape(tt * NCHAIN, u_ref.shape[3]).astype(BF16)
    for kc in range(nk):
        bu_ref[...] = _dot(u2[:, kc * ks:(kc + 1) * ks], wb_ref[kc]).reshape(tt, NCHAIN, sw2)
        for lo in range(0, sw, S5_LANE_BLOCK):
            lb = min(S5_LANE_BLOCK, sw - lo)
            re_sl = slice(lo, lo + lb)
            im_sl = slice(sw + lo, sw + lo + lb)
            ar = are_ref[kc, :, re_sl]
            ai = aim_ref[kc, :, re_sl]

            def step(j, carry, re_sl=re_sl, im_sl=im_sl, ar=ar, ai=ai):
                sre, sim = carry
                for q in range(S5_UNROLL):
                    jj = j * S5_UNROLL + q
                    tloc = tt - 1 - jj if reverse else jj
                    nre = ar * sre - ai * sim + bu_ref[tloc, :, re_sl]
                    nim = ar * sim + ai * sre + bu_ref[tloc, :, im_sl]
                    if emit_y:
                        st_ref[tloc, :, re_sl] = nre
                        st_ref[tloc, :, im_sl] = nim
                    sre, sim = nre, nim
                return sre, sim

            sre, sim = lax.fori_loop(0, tt // S5_UNROLL, step,
                                     (carry_ref[kc, :, re_sl], carry_ref[kc, :, im_sl]))
            carry_ref[kc, :, re_sl] = sre
            carry_ref[kc, :, im_sl] = sim
        if emit_y:
            st = st_ref[...].reshape(tt * NCHAIN, sw2).astype(BF16)
            y_ref[0, :, :, kc * ks:(kc + 1) * ks] = _dot(st, wc_ref[kc]).reshape(tt, NCHAIN, ks)
    sfin_ref[0] = carry_ref[...]


def _s5_pass(s_in, ngroups, wb, wc, are, aim, s0, tt, reverse, emit_y):
    _, ls, _, cs = s_in.shape
    nk, ks, sw2 = wb.shape
    assert tt % S5_UNROLL == 0
    nt = ls // tt
    tmap = (lambda i: nt - 1 - i) if reverse else (lambda i: i)
    c3 = lambda g, i: (0, 0, 0)
    state_spec = pl.BlockSpec((1, nk, NCHAIN, sw2), lambda g, i: (g, 0, 0, 0))
    out_specs = [state_spec]
    out_shape = [jax.ShapeDtypeStruct((ngroups, nk, NCHAIN, sw2), F32)]
    scratch = [pltpu.VMEM((tt, NCHAIN, sw2), F32)]
    if emit_y:
        out_specs.insert(0, pl.BlockSpec((1, tt, NCHAIN, cs), lambda g, i: (g, tmap(i), 0, 0)))
        out_shape.insert(0, jax.ShapeDtypeStruct((ngroups, ls, NCHAIN, cs), F32))
        scratch.append(pltpu.VMEM((tt, NCHAIN, sw2), F32))
    scratch.append(pltpu.VMEM((nk, NCHAIN, sw2), F32))
    res = pl.pallas_call(
        functools.partial(_s5_kernel, reverse=reverse, emit_y=emit_y),
        grid=(ngroups, nt),
        in_specs=[pl.BlockSpec((1, tt, NCHAIN, cs), lambda g, i: (g, tmap(i), 0, 0)),
                  _resident((nk, ks, sw2)), _resident((nk, sw2, ks)),
                  _resident((nk, NCHAIN, sw2 // 2)), _resident((nk, NCHAIN, sw2 // 2)),
                  state_spec],
        out_specs=out_specs,
        out_shape=out_shape,
        scratch_shapes=scratch,
        compiler_params=_cparams(("arbitrary", "arbitrary")),
        name="s5_" + ("bwd" if reverse else "fwd") + ("" if emit_y else "_state"),
    )(s_in, wb, wc, are, aim, s0)
    return (res[0], res[1]) if emit_y else (None, res[0])


def _cmul(are, aim, bre, bim):
    return are * bre - aim * bim, are * bim + aim * bre


def _cpow(re, im, n):
    rre, rim = jnp.ones_like(re), jnp.zeros_like(im)
    while n:
        if n & 1:
            rre, rim = _cmul(rre, rim, re, im)
        re, im = _cmul(re, im, re, im)
        n >>= 1
    return rre, rim


def _s5_params(b_re, b_im, a_re, a_im, log_dt, c_re, c_im):
    ng, ns, nh = b_re.shape
    cs = ng * nh
    ks = min(MXU_DIM, cs)
    gps = ks // nh
    nk = cs // ks
    sw = gps * ns
    a_re, a_im = a_re.astype(F32), a_im.astype(F32)
    dt = jnp.exp(log_dt.astype(F32))[:, None]
    mag = jnp.exp(a_re * dt)
    lre, lim = mag * jnp.cos(a_im * dt), mag * jnp.sin(a_im * dt)
    den = a_re * a_re + a_im * a_im
    zre = ((lre - 1.0) * a_re + lim * a_im) / den
    zim = (lim * a_re - (lre - 1.0) * a_im) / den
    cre, cim = _cmul(c_re.astype(F32), c_im.astype(F32), zre[:, None, :], zim[:, None, :])

    eye = jnp.eye(gps, dtype=F32)

    def in_block(b):
        b = b.astype(F32).reshape(nk, gps, ns, nh)
        return jnp.einsum("kgph,gj->kghjp", b, eye).reshape(nk, ks, sw)

    def out_block(c):
        c = c.reshape(nk, gps, nh, ns)
        return jnp.einsum("kghp,gj->kgpjh", c, eye).reshape(nk, sw, ks)

    wb = jnp.concatenate([in_block(b_re), in_block(b_im)], axis=2).astype(BF16)
    wc = jnp.concatenate([out_block(cre), out_block(-cim)], axis=1).astype(BF16)
    bc = lambda v: jnp.broadcast_to(v.reshape(nk, 1, sw), (nk, NCHAIN, sw))
    return dict(wb=wb, wc=wc, are=bc(lre), aim=bc(lim), lre=lre.reshape(nk, sw), lim=lim.reshape(nk, sw))


def _chain_states(local_end, p, seg, ls, reverse):
    nk, _, sw2 = local_end.shape
    sw = sw2 // 2
    pre, pim = _cpow(p["lre"], p["lim"], ls)
    e = local_end.reshape(nk, NCHAIN // seg, seg, sw2)
    ere, eim = e[..., :sw], e[..., sw:]
    zero = jnp.zeros_like(ere[:, :, 0])
    order = range(seg - 1, -1, -1) if reverse else range(seg)
    sre, sim = zero, zero
    out = [None] * seg
    for k in order:
        out[k] = jnp.concatenate([sre, sim], axis=-1)
        mre, mim = _cmul(pre[:, None, :], pim[:, None, :], sre, sim)
        sre, sim = mre + ere[:, :, k], mim + eim[:, :, k]
    return jnp.stack(out, axis=2).reshape(nk, NCHAIN, sw2)


def _s5_both(s_in, seg, pf, pb, tt):
    ls = s_in.shape[1]
    ys = []
    for p, reverse in ((pf, False), (pb, True)):
        zero = jnp.zeros((1, p["wb"].shape[0], NCHAIN, p["wb"].shape[2]), F32)
        s0 = zero
        if seg > 1:
            _, local_end = _s5_pass(s_in, 1, p["wb"], p["wc"], p["are"], p["aim"], zero, tt, reverse, False)
            s0 = _chain_states(local_end[0], p, seg, ls, reverse)[None]
        y, _ = _s5_pass(s_in, 2, p["wb"], p["wc"], p["are"], p["aim"], jnp.concatenate([s0, zero]), tt,
                        reverse, True)
        ys.append(y)
    return ys


def _glu_kernel(yf_ref, yb_ref, s_ref, d_ref, w_ref, b_ref, o_ref):
    y = yf_ref[...] + yb_ref[...] + d_ref[...] * s_ref[...]
    g = jax.nn.gelu(y)
    o_ref[...] = (g * _sigmoid(_dot(g.astype(BF16), w_ref[...]) + b_ref[...])).astype(o_ref.dtype)


def _glu(yf, yb, s_in, d_skip, glu_w, glu_b, tm):
    rows, cs = yf.shape
    rspec = pl.BlockSpec((tm, cs), lambda i: (i, 0))
    c2 = lambda i: (0, 0)
    return pl.pallas_call(
        _glu_kernel,
        grid=(rows // tm,),
        in_specs=[rspec, rspec, rspec, pl.BlockSpec((1, cs), c2), pl.BlockSpec((cs, cs), c2),
                  pl.BlockSpec((1, cs), c2)],
        out_specs=rspec,
        out_shape=jax.ShapeDtypeStruct((rows, cs), BF16),
        compiler_params=_cparams(("arbitrary",)),
        name="s5_glu",
    )(yf, yb, s_in, d_skip, glu_w, glu_b)


def _pack_halves(v):
    n = v.shape[-1] // 2
    bits = lax.bitcast_convert_type(v.astype(BF16).astype(F32), U32)
    return (bits[..., :n] >> 16) | (bits[..., n:] & jnp.uint32(0xFFFF0000))


def _unpack_halves(p):
    return (lax.bitcast_convert_type(p << 16, F32),
            lax.bitcast_convert_type(p & jnp.uint32(0xFFFF0000), F32))


OUTPROJ_SPLIT = 2


def _outproj_kernel(xp_ref, xs_ref, co_ref, so_ref, g1_ref, sc2_ref, sh2_ref, wo1_ref, wo2_ref, bo_ref,
                    lg_ref, lb_ref, rwh_ref, rwl_ref, x1_ref, hp_ref, lo_ref, *, alpha):
    g = pl.program_id(0)
    tt, d = xp_ref.shape[1], xp_ref.shape[2]
    nj = d // 2 // LANES
    th = tt // OUTPROJ_SPLIT
    for h in range(OUTPROJ_SPLIT):
        ts = slice(h * th, (h + 1) * th)
        rows = slice(h * th * NCHAIN, (h + 1) * th * NCHAIN)
        x = jnp.where(g == 0, xp_ref[:, ts, :], xs_ref[:, ts, :])
        mix = _dot(co_ref[0, rows, :], wo1_ref[...]) + _dot(so_ref[0, rows, :], wo2_ref[...]) + bo_ref[...]
        mix = pltpu.einshape("tcd->ctd", mix.reshape(th, NCHAIN, d))
        x1 = _layer_norm(alpha * x + g1_ref[0] * mix, lg_ref[...], lb_ref[...])
        h2 = x1 * (1.0 + sc2_ref[0]) + sh2_ref[0]
        x1_ref[0, :, ts, :] = x1
        hb = h2.astype(BF16)
        h2f = h2.reshape(NCHAIN * th, d)
        packed = _pack_halves(h2f)
        for j in range(nj):
            hp_ref[0, :, pl.ds(h * th * nj + j, th, stride=nj), :] = (
                packed[:, j * LANES:(j + 1) * LANES].reshape(NCHAIN, th, LANES))
        hi = hb.reshape(NCHAIN * th, d)
        lo = (h2f - hi.astype(F32)).astype(BF16)
        logits = _dot(hi, rwh_ref[...]) + (_dot(hi, rwl_ref[...]) + _dot(lo, rwh_ref[...]))
        lo_ref[0, :, ts, :] = logits.reshape(NCHAIN, th, logits.shape[-1])


def _outproj(xp, xs, conv_out, ssm_out, gate1, scale2, shift2, wo1, wo2, b_out, ln_g, ln_b, rw_hi, rw_lo,
             alpha, tt):
    _, ls, d = xp.shape
    cc, cs = wo1.shape[0], wo2.shape[0]
    ne = rw_hi.shape[1]
    nt = ls // tt
    nj = d // 2 // LANES
    xp_spec, xs_spec = _x_specs(tt, d, nt)
    c2 = lambda g, t: (0, 0)
    nat = lambda w: pl.BlockSpec((1, NCHAIN, tt, w), lambda g, t: (g, 0, t, 0))
    return pl.pallas_call(
        functools.partial(_outproj_kernel, alpha=alpha),
        grid=(2, nt),
        in_specs=[xp_spec, xs_spec,
                  pl.BlockSpec((1, tt * NCHAIN, cc), lambda g, t: (g, t, 0)),
                  pl.BlockSpec((1, tt * NCHAIN, cs), lambda g, t: (g, t, 0)),
                  _mod_spec(d), _mod_spec(d), _mod_spec(d),
                  _resident((cc, d)), _resident((cs, d)), pl.BlockSpec((1, d), c2),
                  pl.BlockSpec((1, d), c2), pl.BlockSpec((1, d), c2),
                  _resident((d, ne)), _resident((d, ne))],
        out_specs=[nat(d),
                   pl.BlockSpec((1, NCHAIN, tt * nj, LANES), lambda g, t: (g, 0, t, 0)),
                   nat(ne)],
        out_shape=[jax.ShapeDtypeStruct((2, NCHAIN, ls, d), F32),
                   jax.ShapeDtypeStruct((2, NCHAIN, ls * nj, LANES), U32),
                   jax.ShapeDtypeStruct((2, NCHAIN, ls, ne), F32)],
        compiler_params=_cparams(("arbitrary", "arbitrary")),
        name="outproj",
    )(xp, xs, conv_out, ssm_out, gate1, scale2, shift2, wo1, wo2, b_out, ln_g, ln_b, rw_hi, rw_lo)


def _route_kernel(lg_ref, bias_ref, eid_ref, e_ref, w_ref, r_ref, cnt_ref, carry_ref):
    i = pl.program_id(0)
    tm, ne = lg_ref.shape
    gs = ne // N_EXPERT_GROUPS
    neg = jnp.float32(-jnp.inf)

    @pl.when(i == 0)
    def _():
        carry_ref[...] = jnp.zeros_like(carry_ref)

    scores = _sigmoid(lg_ref[...]).T
    biased = scores + bias_ref[...]
    eid = eid_ref[...]

    def first_max(v, ids):
        m = jnp.max(v, axis=0, keepdims=True)
        idx = jnp.min(jnp.where(v == m, ids, float(ne)), axis=0, keepdims=True)
        return m, idx

    gscore = []
    for q in range(N_EXPERT_GROUPS):
        vg, ids = biased[q * gs:(q + 1) * gs], eid[q * gs:(q + 1) * gs]
        m1, i1 = first_max(vg, ids)
        m2 = jnp.max(jnp.where(ids == i1, neg, vg), axis=0, keepdims=True)
        gscore.append(m1 + m2)
    parts = []
    for q in range(N_EXPERT_GROUPS):
        beaten = jnp.zeros((1, tm), F32)
        for o in range(N_EXPERT_GROUPS):
            if o != q:
                wins = (gscore[o] >= gscore[q]) if o < q else (gscore[o] > gscore[q])
                beaten = beaten + wins.astype(F32)
        parts.append(jnp.where(beaten < TOPK_GROUPS, biased[q * gs:(q + 1) * gs], neg))
    masked = jnp.concatenate(parts, axis=0)

    chosen = jnp.zeros((ne, tm), F32)
    hits, e_rows, w_rows = [], [], []
    for k in range(TOP_K):
        _, idx = first_max(masked, eid)
        hit = eid == idx
        hits.append(hit)
        w_rows.append(jnp.sum(jnp.where(hit, scores, 0.0), axis=0, keepdims=True))
        e_rows.append(idx)
        masked = jnp.where(hit, neg, masked)
        chosen = jnp.where(hit, 1.0, chosen)
    w_t = jnp.concatenate(w_rows, axis=0)
    w_ref[...] = w_t / jnp.sum(w_t, axis=0, keepdims=True) * ROUTED_SCALE
    e_ref[...] = jnp.concatenate(e_rows, axis=0).astype(I32)

    rr = lax.broadcasted_iota(I32, (tm, tm), 0)
    cc = lax.broadcasted_iota(I32, (tm, tm), 1)
    before = (rr < cc).astype(BF16)
    prefix = _dot(chosen.astype(BF16), before) + carry_ref[...]
    r_rows = [jnp.sum(jnp.where(hits[k], prefix, 0.0), axis=0, keepdims=True) for k in range(TOP_K)]
    r_ref[...] = jnp.concatenate(r_rows, axis=0).astype(I32)
    carry_ref[...] = carry_ref[...] + jnp.sum(chosen, axis=1, keepdims=True)
    cnt_ref[...] = carry_ref[...]


def _route(logits, bias, tm):
    t, ne = logits.shape
    kspec = pl.BlockSpec((TOP_K, tm), lambda i: (0, i))
    col = pl.BlockSpec((ne, 1), lambda i: (0, 0))
    eid = jnp.broadcast_to(jnp.arange(ne, dtype=F32)[:, None], (ne, tm))
    return pl.pallas_call(
        _route_kernel,
        grid=(t // tm,),
        in_specs=[pl.BlockSpec((tm, ne), lambda i: (i, 0)), col, _resident((ne, tm))],
        out_specs=[kspec, kspec, kspec, col],
        out_shape=[jax.ShapeDtypeStruct((TOP_K, t), I32), jax.ShapeDtypeStruct((TOP_K, t), F32),
                   jax.ShapeDtypeStruct((TOP_K, t), I32), jax.ShapeDtypeStruct((ne, 1), F32)],
        scratch_shapes=[pltpu.VMEM((ne, 1), F32)],
        compiler_params=_cparams(("arbitrary",)),
        name="route",
    )(logits, bias.reshape(ne, 1), eid)


def _dest_kernel(e_ref, r_ref, ps_ref, eid_ref, o_ref):
    e = e_ref[...].astype(F32)
    starts = [jnp.sum(jnp.where(eid_ref[...] == e[k:k + 1], ps_ref[...], 0.0), axis=0, keepdims=True)
              for k in range(TOP_K)]
    o_ref[...] = r_ref[...] + jnp.concatenate(starts, axis=0).astype(I32)


def _dest(e_idx, rank, pad_start, tm):
    t = e_idx.shape[1]
    ne = pad_start.shape[0]
    kspec = pl.BlockSpec((TOP_K, tm), lambda i: (0, i))
    eid = jnp.broadcast_to(jnp.arange(ne, dtype=F32)[:, None], (ne, tm))
    return pl.pallas_call(
        _dest_kernel,
        grid=(t // tm,),
        in_specs=[kspec, kspec, pl.BlockSpec((ne, 1), lambda i: (0, 0)), _resident((ne, tm))],
        out_specs=kspec,
        out_shape=jax.ShapeDtypeStruct((TOP_K, t), I32),
        compiler_params=_cparams(("arbitrary",)),
        name="dest",
    )(e_idx, rank, pad_start, eid)


def _dispatch_kernel(last_ref, nu_ref, dest_ref, h_ref, sg_ref, su_ref, sd_ref, xs_ref, sh_ref, zero_ref, sem, zsem,
                     *, nj, tile_rows):
    i = pl.program_id(0)
    tm = h_ref.shape[0] // nj
    ne = last_ref.shape[0]

    @pl.when(i == 0)
    def _():
        zero_ref[...] = jnp.zeros_like(zero_ref)

        def fill(row):
            start = pl.multiple_of(row * nj, 8)
            return pltpu.make_async_copy(zero_ref, xs_ref.at[pl.ds(start, tile_rows * nj)], zsem)

        def issue(e, c):
            @pl.when(last_ref[e] >= 0)
            def _():
                fill(last_ref[e]).start()
            return c

        def drain(e, c):
            @pl.when(last_ref[e] >= 0)
            def _():
                fill(last_ref[e]).wait()
            return c

        lax.fori_loop(0, ne, issue, 0)
        lax.fori_loop(0, ne, drain, 0)
        n_tiles = xs_ref.shape[0] // (tile_rows * nj)
        lax.fori_loop(nu_ref[0], n_tiles, lambda q, c: (fill(q * tile_rows).start(), c)[1], 0)
        lax.fori_loop(nu_ref[0], n_tiles, lambda q, c: (fill(q * tile_rows).wait(), c)[1], 0)

    def issue_rows(r, c):
        src = h_ref.at[pl.ds(pl.multiple_of(r * nj, nj), nj)]
        for k in range(TOP_K):
            dst = xs_ref.at[pl.ds(pl.multiple_of(dest_ref[r * TOP_K + k] * nj, nj), nj)]
            pltpu.make_async_copy(src, dst, sem).start()
        return c

    lax.fori_loop(0, tm, issue_rows, 0)

    x = _unpack_rows(h_ref, tm, nj)
    gate = _dot(x, sg_ref[...])
    up = _dot(x, su_ref[...])
    sh_ref[...] = _dot((gate * _sigmoid(gate) * up).astype(BF16), sd_ref[...]).astype(sh_ref.dtype)

    for _ in range(TOP_K):
        pltpu.make_async_copy(h_ref, xs_ref.at[pl.ds(0, tm * nj)], sem).wait()


def _dispatch(last_tile_row, n_used, dest_flat, h_packed, sh_gate, sh_up, sh_down, n_rows, nj, tile_rows, tm):
    t = h_packed.shape[0] // nj
    d, ff = sh_gate.shape
    grid_spec = pltpu.PrefetchScalarGridSpec(
        num_scalar_prefetch=2,
        grid=(t // tm,),
        in_specs=[pl.BlockSpec((tm * TOP_K,), lambda i, last, nu: (i,), memory_space=pltpu.SMEM),
                  pl.BlockSpec((tm * nj, LANES), lambda i, last, nu: (i, 0)),
                  _resident((d, ff)), _resident((d, ff)), _resident((ff, d))],
        out_specs=[pl.BlockSpec(memory_space=pl.ANY), pl.BlockSpec((tm, d), lambda i, last, nu: (i, 0))],
        scratch_shapes=[pltpu.VMEM((tile_rows * nj, LANES), U32), pltpu.SemaphoreType.DMA,
                        pltpu.SemaphoreType.DMA],
    )
    return pl.pallas_call(
        functools.partial(_dispatch_kernel, nj=nj, tile_rows=tile_rows),
        grid_spec=grid_spec,
        out_shape=[jax.ShapeDtypeStruct((n_rows * nj, LANES), U32), jax.ShapeDtypeStruct((t, d), BF16)],
        compiler_params=_cparams(("arbitrary",)),
        name="dispatch",
    )(last_tile_row, n_used, dest_flat, h_packed, sh_gate, sh_up, sh_down)


def _unpack_rows(p_ref, tm, nj):
    lo, hi = [], []
    for j in range(nj):
        l, h = _unpack_halves(p_ref[pl.ds(j, tm, stride=nj), :])
        lo.append(l.astype(BF16))
        hi.append(h.astype(BF16))
    return jnp.concatenate(lo + hi, axis=-1)


CAST_ROWS = 128


def _experts_kernel(te_ref, nu_ref, nx_ref, x_ref, wg_hbm, wu_hbm, wd_hbm, y_ref,
                    wgf_ref, wuf_ref, wdf_ref, wgb_ref, wub_ref, wdb_ref, sem, *, nj):
    i = pl.program_id(0)
    tm = x_ref.shape[0] // nj
    live = i < nu_ref[0]
    e = te_ref[i]
    first = live & ((i == 0) | (e != te_ref[jnp.maximum(i - 1, 0)]))

    def fetch(expert):
        return (pltpu.make_async_copy(wg_hbm.at[expert], wgf_ref, sem.at[0]),
                pltpu.make_async_copy(wu_hbm.at[expert], wuf_ref, sem.at[1]),
                pltpu.make_async_copy(wd_hbm.at[expert], wdf_ref, sem.at[2]))

    @pl.when(live & (i == 0))
    def _():
        for c in fetch(e):
            c.start()

    @pl.when(first)
    def _():
        for c in fetch(e):
            c.wait()
        for src, dst in ((wgf_ref, wgb_ref), (wuf_ref, wub_ref), (wdf_ref, wdb_ref)):
            for r in range(0, src.shape[0], CAST_ROWS):
                dst[r:r + CAST_ROWS] = src[r:r + CAST_ROWS].astype(BF16)

        @pl.when(nx_ref[i] >= 0)
        def _():
            for c in fetch(nx_ref[i]):
                c.start()

    @pl.when(live)
    def _():
        x = _unpack_rows(x_ref, tm, nj)
        gate = _dot(x, wgb_ref[...])
        up = _dot(x, wub_ref[...])
        act = (gate * _sigmoid(gate) * up).astype(BF16)
        y = _dot(act, wdb_ref[...])
        packed = _pack_halves(y)
        for j in range(nj):
            y_ref[pl.ds(j, tm, stride=nj), :] = packed[:, j * LANES:(j + 1) * LANES]

    @pl.when(i >= nu_ref[0])
    def _():
        y_ref[...] = jnp.zeros_like(y_ref)


def _experts(tile_e, n_used, next_e, x_sorted, w_gate, w_up, w_down, nj, tm):
    n_rows = x_sorted.shape[0] // nj
    ne, d, ff = w_gate.shape
    n_tiles = n_rows // tm
    hbm = pl.BlockSpec(memory_space=pl.ANY)
    grid_spec = pltpu.PrefetchScalarGridSpec(
        num_scalar_prefetch=3,
        grid=(n_tiles,),
        in_specs=[pl.BlockSpec((tm * nj, LANES), lambda i, te, nu, nx: (jnp.minimum(i, nu[0] - 1), 0)),
                  hbm, hbm, hbm],
        out_specs=pl.BlockSpec((tm * nj, LANES), lambda i, te, nu, nx: (i, 0)),
        scratch_shapes=[pltpu.VMEM((d, ff), F32), pltpu.VMEM((d, ff), F32), pltpu.VMEM((ff, d), F32),
                        pltpu.VMEM((d, ff), BF16), pltpu.VMEM((d, ff), BF16), pltpu.VMEM((ff, d), BF16),
                        pltpu.SemaphoreType.DMA((3,))],
    )
    return pl.pallas_call(
        functools.partial(_experts_kernel, nj=nj),
        grid_spec=grid_spec,
        out_shape=jax.ShapeDtypeStruct((n_rows * nj, LANES), U32),
        compiler_params=_cparams(("arbitrary",)),
        name="experts",
    )(tile_e, n_used, next_e, x_sorted, w_gate, w_up, w_down)


COMBINE_ROWS = 16


def _combine_kernel(dest_ref, dnext_ref, w_ref, x1_ref, sh_ref, g2_ref, lg_ref, lb_ref,
                    ys_ref, op_ref, os_ref, buf_ref, acc_ref, sem, *, alpha, half, nj):
    i = pl.program_id(0)
    n = pl.num_programs(0)
    tm, d = sh_ref.shape
    slot = i % 2

    def gather(d_ref, s):
        def issue(r, c):
            for k in range(TOP_K):
                src = ys_ref.at[pl.ds(pl.multiple_of(d_ref[r * TOP_K + k] * nj, nj), nj)]
                dst = buf_ref.at[s, k, pl.ds(pl.multiple_of(r * nj, nj), nj)]
                pltpu.make_async_copy(src, dst, sem.at[s]).start()
            return c

        lax.fori_loop(0, tm, issue, 0)

    @pl.when(i == 0)
    def _():
        gather(dest_ref, 0)

    @pl.when(i + 1 < n)
    def _():
        gather(dnext_ref, 1 - slot)

    for k in range(TOP_K):
        pltpu.make_async_copy(ys_ref.at[pl.ds(0, tm * nj)], buf_ref.at[slot, k], sem.at[slot]).wait()

    def reduce_rows(c, carry):
        r0 = pl.multiple_of(c * COMBINE_ROWS, COMBINE_ROWS)
        w = w_ref[pl.ds(r0, COMBINE_ROWS), :]
        lo = [None] * nj
        hi = [None] * nj
        for k in range(TOP_K):
            wk = jnp.broadcast_to(w[:, k:k + 1], (COMBINE_ROWS, LANES))
            for j in range(nj):
                l, h = _unpack_halves(buf_ref[slot, k, pl.ds(r0 * nj + j, COMBINE_ROWS, stride=nj), :])
                lo[j] = l * wk if k == 0 else lo[j] + l * wk
                hi[j] = h * wk if k == 0 else hi[j] + h * wk
        acc_ref[pl.ds(r0, COMBINE_ROWS), :] = jnp.concatenate(lo + hi, axis=-1)
        return carry

    lax.fori_loop(0, tm // COMBINE_ROWS, reduce_rows, 0)
    ffn = acc_ref[...] + sh_ref[...].astype(F32)
    out = _layer_norm(alpha * x1_ref[...] + g2_ref[0] * ffn, lg_ref[...], lb_ref[...])

    @pl.when(i < half)
    def _():
        op_ref[...] = out

    @pl.when(i >= half)
    def _():
        os_ref[...] = out


def _combine(dest_flat, w, x1, shared, gate2_rows, y_sorted, ln_g, ln_b, alpha, nj, tm, ls):
    t, d = shared.shape
    n = t // tm
    half = n // 2
    per_chain = ls // tm
    assert tm % COMBINE_ROWS == 0
    c2 = lambda i: (0, 0)
    pmap = lambda i: (jnp.minimum(i, half - 1), 0)
    smap = lambda i: (jnp.maximum(i - half, 0), 0)
    return pl.pallas_call(
        functools.partial(_combine_kernel, alpha=alpha, half=half, nj=nj),
        grid=(n,),
        in_specs=[pl.BlockSpec((tm * TOP_K,), lambda i: (i,), memory_space=pltpu.SMEM),
                  pl.BlockSpec((tm * TOP_K,), lambda i: (jnp.minimum(i + 1, n - 1),), memory_space=pltpu.SMEM),
                  pl.BlockSpec((tm, TOP_K), lambda i: (i, 0)),
                  pl.BlockSpec((tm, d), lambda i: (i, 0)),
                  pl.BlockSpec((tm, d), lambda i: (i, 0)),
                  pl.BlockSpec((1, 1, d), lambda i: (i // per_chain, 0, 0)),
                  pl.BlockSpec((1, d), c2), pl.BlockSpec((1, d), c2),
                  pl.BlockSpec(memory_space=pl.ANY)],
        out_specs=[pl.BlockSpec((tm, d), pmap), pl.BlockSpec((tm, d), smap)],
        out_shape=[jax.ShapeDtypeStruct((t // 2, d), F32), jax.ShapeDtypeStruct((t // 2, d), F32)],
        scratch_shapes=[pltpu.VMEM((2, TOP_K, tm * nj, LANES), U32), pltpu.VMEM((tm, d), F32),
                        pltpu.SemaphoreType.DMA((2,))],
        compiler_params=_cparams(("arbitrary",)),
        name="combine",
    )(dest_flat, dest_flat, w, x1, shared, gate2_rows, ln_g, ln_b, y_sorted)


def _tile(pref, n, mult=8):
    t = min(pref, n)
    while n % t or t % mult:
        t -= 1
    return t


def _encoder_layer(xp, xs, c_all, chain_seq, seg, alpha, p):
    _, ls, d = xp.shape
    cc = p["conv_w"].shape[-1]

    ada = _ada(c_all, p["w_ada"], p["b_ada"])
    mods = ada.reshape(ada.shape[0], N_ADA, d)[chain_seq].reshape(2, NCHAIN, N_ADA, 1, d)
    shift1, scale1, gate1, shift2, scale2, gate2 = (mods[:, :, k] for k in range(N_ADA))

    w_in = p["w_in"].astype(BF16)
    b_in = p["b_in"].reshape(1, -1)
    u, s_in = _inproj(xp, xs, scale1, shift1, w_in[:, :cc], w_in[:, cc:2 * cc], w_in[:, 2 * cc:],
                      b_in[:, :cc], b_in[:, cc:2 * cc], b_in[:, 2 * cc:], _tile(TILES["inproj"], ls))

    chain = np.arange(NCHAIN)
    prev_ok = jnp.asarray(np.stack([(chain % seg != 0), np.zeros(NCHAIN, bool)]).astype(np.float32)[..., None])
    next_ok = jnp.asarray(np.stack([(chain % seg != seg - 1), np.zeros(NCHAIN, bool)]).astype(np.float32)[..., None])
    conv_out = _conv(u, prev_ok, next_ok, p["conv_w"], p["conv_b"], p["conv_ln_g"], p["conv_ln_b"],
                     _tile(TILES["conv"], ls, CONV_HALO))

    pf = _s5_params(p["ssm_b_re"], p["ssm_b_im"], p["ssm_a_re_f"], p["ssm_a_im_f"], p["ssm_log_dt_f"],
                    p["ssm_c_re_f"], p["ssm_c_im_f"])
    pb = _s5_params(p["ssm_b_re"], p["ssm_b_im"], p["ssm_a_re_b"], p["ssm_a_im_b"], p["ssm_log_dt_b"],
                    p["ssm_c_re_b"], p["ssm_c_im_b"])
    yf, yb = _s5_both(s_in, seg, pf, pb, _tile(TILES["s5"], ls))
    cs = s_in.shape[-1]
    rows_tm = 2 * ls * NCHAIN
    flat = lambda a: a.reshape(rows_tm, cs)
    ssm_out = _glu(flat(yf), flat(yb), flat(s_in), p["ssm_d"].reshape(1, cs),
                   p["ssm_glu_w"].astype(BF16), p["ssm_glu_b"].reshape(1, cs), _tile(TILES["glu"], rows_tm))
    ssm_out = ssm_out.reshape(2, ls * NCHAIN, cs)

    w_out = p["w_out"].astype(BF16)
    rw = p["router_w"].astype(F32)
    rw_hi = rw.astype(BF16)
    rw_lo = (rw - rw_hi.astype(F32)).astype(BF16)
    row = lambda v: v.reshape(1, -1)
    x1, hp, logits = _outproj(xp, xs, conv_out, ssm_out, gate1, scale2, shift2, w_out[:cc], w_out[cc:],
                              row(p["b_out"]), row(p["ln1_g"]), row(p["ln1_b"]), rw_hi, rw_lo,
                              alpha, _tile(TILES["outproj"], ls, 16))

    t = 2 * NCHAIN * ls
    ne = rw.shape[1]
    nj = d // 2 // LANES
    e_idx, e_w, rank, counts = _route(logits.reshape(t, ne), row(p["router_bias"]).astype(F32),
                                      _tile(TILES["route"], t, LANES))

    tm_e = _tile(TILES["expert"], t)
    n_rows = t * TOP_K + ne * tm_e
    counts = counts.reshape(ne).astype(I32)
    e_w = e_w.T
    padded = (counts + tm_e - 1) // tm_e * tm_e
    pad_end = jnp.cumsum(padded)
    pad_start = pad_end - padded
    n_tiles = n_rows // tm_e
    tile_e = jnp.minimum(jnp.searchsorted(pad_end, jnp.arange(n_tiles, dtype=I32) * tm_e, side="right"),
                         ne - 1).astype(I32)
    n_used = (pad_end[-1:] // tm_e).astype(I32)
    last_tile_row = jnp.where(counts > 0, pad_end - tm_e, -1).astype(I32)
    following = pad_end[tile_e] // tm_e
    next_e = jnp.where(following < n_used[0], tile_e[jnp.minimum(following, n_tiles - 1)], -1).astype(I32)
    dest = _dest(e_idx, rank, pad_start.astype(F32).reshape(ne, 1), _tile(TILES["dest"], t, LANES))
    dest = dest.T.reshape(t * TOP_K)

    x_sorted, shared = _dispatch(last_tile_row, n_used, dest, hp.reshape(t * nj, LANES),
                                 p["sh_w_gate"].astype(BF16), p["sh_w_up"].astype(BF16),
                                 p["sh_w_down"].astype(BF16), n_rows, nj, tm_e, _tile(TILES["dispatch"], ls))
    y_sorted = _experts(tile_e, n_used, next_e, x_sorted, p["exp_w_gate"], p["exp_w_up"], p["exp_w_down"],
                        nj, tm_e)
    tm_c = _tile(TILES["combine"], ls, COMBINE_ROWS)
    gate2_rows = gate2.reshape(2 * NCHAIN, 1, d)
    yp, ys = _combine(dest, e_w, x1.reshape(t, d), shared, gate2_rows, y_sorted,
                      row(p["ln2_g"]), row(p["ln2_b"]), alpha, nj, tm_c, ls)
    return yp.reshape(NCHAIN, ls, d), ys.reshape(NCHAIN, ls, d)


_PARAM_NAMES = ("w_ada", "b_ada", "w_in", "b_in", "conv_w", "conv_b", "conv_ln_g", "conv_ln_b",
                "ssm_b_re", "ssm_b_im", "ssm_a_re_f", "ssm_a_im_f", "ssm_log_dt_f",
                "ssm_a_re_b", "ssm_a_im_b", "ssm_log_dt_b", "ssm_c_re_f", "ssm_c_im_f",
                "ssm_c_re_b", "ssm_c_im_b", "ssm_d", "ssm_glu_w", "ssm_glu_b", "w_out", "b_out",
                "ln1_g", "ln1_b", "router_w", "router_bias", "exp_w_gate", "exp_w_up", "exp_w_down",
                "sh_w_gate", "sh_w_up", "sh_w_down", "ln2_g", "ln2_b")


def kernel(x_prompt, x_sample, c_prompt, c_sample, w_ada, b_ada, w_in, b_in, conv_w, conv_b, conv_ln_g, conv_ln_b, ssm_b_re, ssm_b_im, ssm_a_re_f, ssm_a_im_f, ssm_log_dt_f, ssm_a_re_b, ssm_a_im_b, ssm_log_dt_b, ssm_c_re_f, ssm_c_im_f, ssm_c_re_b, ssm_c_im_b, ssm_d, ssm_glu_w, ssm_glu_b, w_out, b_out, ln1_g, ln1_b, router_w, router_bias, exp_w_gate, exp_w_up, exp_w_down, sh_w_gate, sh_w_up, sh_w_down, ln2_g, ln2_b):
    stacked = (w_ada, b_ada, w_in, b_in, conv_w, conv_b, conv_ln_g, conv_ln_b, ssm_b_re, ssm_b_im,
               ssm_a_re_f, ssm_a_im_f, ssm_log_dt_f, ssm_a_re_b, ssm_a_im_b, ssm_log_dt_b, ssm_c_re_f,
               ssm_c_im_f, ssm_c_re_b, ssm_c_im_b, ssm_d, ssm_glu_w, ssm_glu_b, w_out, b_out, ln1_g, ln1_b,
               router_w, router_bias, exp_w_gate, exp_w_up, exp_w_down, sh_w_gate, sh_w_up, sh_w_down,
               ln2_g, ln2_b)
    depth = w_ada.shape[0]
    alpha = (2 * depth) ** 0.25
    bp, lp, d = x_prompt.shape
    bs, lsample, _ = x_sample.shape
    assert NCHAIN % bp == 0 and bs == NCHAIN and lp % (NCHAIN // bp) == 0
    seg = NCHAIN // bp
    ls = lp // seg
    assert ls == lsample

    c_all = jnp.concatenate([c_prompt, c_sample], axis=0)
    c_all = jnp.pad(c_all, ((0, -c_all.shape[0] % 8), (0, 0)))
    chain_seq = np.concatenate([np.arange(NCHAIN) // seg, bp + np.arange(NCHAIN)])

    xp = x_prompt.reshape(NCHAIN, ls, d)
    xs = x_sample.reshape(NCHAIN, ls, d)
    for l in range(depth):
        params = {n: v[l] for n, v in zip(_PARAM_NAMES, stacked)}
        xp, xs = _encoder_layer(xp, xs, c_all, chain_seq, seg, alpha, params)
    return xp.reshape(bp, lp, d), xs.reshape(bs, lsample, d)
```

```python
import functools
import math

import numpy as np
import jax
import jax.numpy as jnp
from jax import lax
from jax.experimental import pallas as pl
from jax.experimental.pallas import tpu as pltpu

F32 = jnp.float32
BF16 = jnp.bfloat16
U32 = jnp.uint32
I32 = jnp.int32

LN_EPS = 1e-5
N_ADA = 6
TOP_K = 8
N_EXPERT_GROUPS = 8
TOPK_GROUPS = 4
ROUTED_SCALE = 2.5

NCHAIN = 8
LANES = 128
MXU_DIM = 256
VMEM_LIMIT = 56 * 1024 * 1024

TILES = dict(inproj=64, conv=128, s5=128, glu=1024, outproj=32, route=512, dest=1024, expert=512,
             dispatch=512, combine=256)


def _cparams(sem):
    return pltpu.CompilerParams(dimension_semantics=sem, vmem_limit_bytes=VMEM_LIMIT)


def _resident(shape):
    zeros = (0,) * len(shape)
    return pl.BlockSpec(shape, lambda *_: zeros, pipeline_mode=pl.Buffered(1))


def _dot(a, b):
    return jnp.dot(a, b, preferred_element_type=F32)


def _sigmoid(x):
    return jax.nn.sigmoid(x)


def _layer_norm(v, g, b):
    mu = jnp.mean(v, axis=-1, keepdims=True)
    d = v - mu
    var = jnp.mean(d * d, axis=-1, keepdims=True)
    return d * lax.rsqrt(var + LN_EPS) * g + b


def _ada_kernel(c_ref, w_ref, b_ref, o_ref):
    c = c_ref[...]
    s = (c * _sigmoid(c)).astype(BF16)
    o_ref[...] = _dot(s, w_ref[...].astype(BF16)) + b_ref[...]


def _ada(c_all, w_ada, b_ada):
    rows, d = c_all.shape
    n = w_ada.shape[1]
    tn = _tile(2048, n, LANES)
    return pl.pallas_call(
        _ada_kernel,
        grid=(n // tn,),
        in_specs=[pl.BlockSpec((rows, d), lambda j: (0, 0)),
                  pl.BlockSpec((d, tn), lambda j: (0, j)),
                  pl.BlockSpec((1, tn), lambda j: (0, j))],
        out_specs=pl.BlockSpec((rows, tn), lambda j: (0, j)),
        out_shape=jax.ShapeDtypeStruct((rows, n), F32),
        compiler_params=_cparams(("arbitrary",)),
        name="ada",
    )(c_all, w_ada, b_ada.reshape(1, n))


def _x_specs(tt, d, nt):
    xp = pl.BlockSpec((NCHAIN, tt, d), lambda g, t: (0, jnp.where(g == 0, t, nt - 1), 0))
    xs = pl.BlockSpec((NCHAIN, tt, d), lambda g, t: (0, jnp.where(g == 1, t, 0), 0))
    return xp, xs


def _mod_spec(d):
    return pl.BlockSpec((1, NCHAIN, 1, d), lambda g, t: (g, 0, 0, 0))


def _inproj_kernel(xp_ref, xs_ref, sc_ref, sh_ref, wa_ref, wg_ref, ws_ref, ba_ref, bg_ref, bs_ref,
                   u_ref, s_ref):
    g = pl.program_id(0)
    x = jnp.where(g == 0, xp_ref[...], xs_ref[...])
    h = x * (1.0 + sc_ref[0]) + sh_ref[0]
    tt = h.shape[1]
    ht = pltpu.einshape("ctd->tcd", h).reshape(tt * NCHAIN, h.shape[2]).astype(BF16)
    a = _dot(ht, wa_ref[...]) + ba_ref[...]
    gt = _dot(ht, wg_ref[...]) + bg_ref[...]
    u = a * _sigmoid(gt)
    s = _dot(ht, ws_ref[...]) + bs_ref[...]
    u_ref[0] = u.reshape(tt, NCHAIN, u.shape[-1])
    s_ref[0] = s.reshape(tt, NCHAIN, s.shape[-1])


def _inproj(xp, xs, scale1, shift1, wa, wg, ws, ba, bg, bs, tt):
    _, ls, d = xp.shape
    cc, cs = wa.shape[1], ws.shape[1]
    nt = ls // tt
    xp_spec, xs_spec = _x_specs(tt, d, nt)
    const2 = lambda g, t: (0, 0)
    return pl.pallas_call(
        _inproj_kernel,
        grid=(2, nt),
        in_specs=[xp_spec, xs_spec, _mod_spec(d), _mod_spec(d),
                  _resident((d, cc)), _resident((d, cc)), _resident((d, cs)),
                  pl.BlockSpec((1, cc), const2), pl.BlockSpec((1, cc), const2), pl.BlockSpec((1, cs), const2)],
        out_specs=[pl.BlockSpec((1, tt, NCHAIN, cc), lambda g, t: (g, t, 0, 0)),
                   pl.BlockSpec((1, tt, NCHAIN, cs), lambda g, t: (g, t, 0, 0))],
        out_shape=[jax.ShapeDtypeStruct((2, ls, NCHAIN, cc), F32),
                   jax.ShapeDtypeStruct((2, ls, NCHAIN, cs), F32)],
        compiler_params=_cparams(("arbitrary", "arbitrary")),
        name="inproj",
    )(xp, xs, scale1, shift1, wa, wg, ws, ba, bg, bs)


CONV_HALO = 16
CONV_BLOCK = 8


def _conv_kernel(u_ref, up_ref, un_ref, pok_ref, nok_ref, w_ref, cb_ref, g_ref, b_ref, o_ref,
                 buf_ref, acc_ref, *, width):
    t = pl.program_id(1)
    nt = pl.num_programs(1)
    tt = u_ref.shape[1]
    pad = (width - 1) // 2
    prev = up_ref[0]
    prev_wrapped = pltpu.roll(prev, shift=1, axis=1) * pok_ref[0]
    buf_ref[0:CONV_HALO] = jnp.where(t == 0, prev_wrapped, prev)
    buf_ref[CONV_HALO:CONV_HALO + tt] = u_ref[0]
    nxt = un_ref[0]
    next_wrapped = pltpu.roll(nxt, shift=NCHAIN - 1, axis=1) * nok_ref[0]
    buf_ref[CONV_HALO + tt:2 * CONV_HALO + tt] = jnp.where(t == nt - 1, next_wrapped, nxt)

    base = CONV_HALO - pad

    for c0 in range(0, u_ref.shape[3], LANES):
        lanes = slice(c0, c0 + LANES)
        taps = [w_ref[k, :, lanes] for k in range(width)]
        bias = cb_ref[:, lanes]

        def body(b, carry, lanes=lanes, taps=taps, bias=bias):
            t0 = b * CONV_BLOCK
            acc = [bias] * CONV_BLOCK
            for s in range(CONV_BLOCK + width - 1):
                x = buf_ref[t0 + base + s, :, lanes]
                for i in range(CONV_BLOCK):
                    if 0 <= s - i < width:
                        acc[i] = acc[i] + x * taps[s - i]
            for i in range(CONV_BLOCK):
                acc_ref[t0 + i, :, lanes] = acc[i]
            return carry

        lax.fori_loop(0, tt // CONV_BLOCK, body, 0)
    v = _layer_norm(acc_ref[...], g_ref[...], b_ref[...])
    v = v * _sigmoid(v)
    o_ref[0] = v.reshape(tt * NCHAIN, v.shape[-1]).astype(o_ref.dtype)


def _conv(u, prev_ok, next_ok, conv_w, conv_b, ln_g, ln_b, tt):
    _, ls, _, cc = u.shape
    width = conv_w.shape[0]
    assert (width - 1) // 2 <= CONV_HALO and tt % CONV_HALO == 0 and cc % LANES == 0
    nt = ls // tt
    hb = tt // CONV_HALO
    nh = ls // CONV_HALO
    wb = jnp.broadcast_to(conv_w[:, None, :], (width, NCHAIN, cc))
    row = lambda v: jnp.broadcast_to(v[None, :], (NCHAIN, cc))
    const2 = lambda g, t: (0, 0)
    return pl.pallas_call(
        functools.partial(_conv_kernel, width=width),
        grid=(2, nt),
        in_specs=[pl.BlockSpec((1, tt, NCHAIN, cc), lambda g, t: (g, t, 0, 0)),
                  pl.BlockSpec((1, CONV_HALO, NCHAIN, cc),
                               lambda g, t: (g, jnp.where(t == 0, nh - 1, t * hb - 1), 0, 0)),
                  pl.BlockSpec((1, CONV_HALO, NCHAIN, cc),
                               lambda g, t: (g, jnp.where(t == nt - 1, 0, (t + 1) * hb), 0, 0)),
                  pl.BlockSpec((1, NCHAIN, 1), lambda g, t: (g, 0, 0)),
                  pl.BlockSpec((1, NCHAIN, 1), lambda g, t: (g, 0, 0)),
                  pl.BlockSpec((width, NCHAIN, cc), lambda g, t: (0, 0, 0)),
                  pl.BlockSpec((NCHAIN, cc), const2), pl.BlockSpec((NCHAIN, cc), const2),
                  pl.BlockSpec((NCHAIN, cc), const2)],
        out_specs=pl.BlockSpec((1, tt * NCHAIN, cc), lambda g, t: (g, t, 0)),
        out_shape=jax.ShapeDtypeStruct((2, ls * NCHAIN, cc), BF16),
        scratch_shapes=[pltpu.VMEM((tt + 2 * CONV_HALO, NCHAIN, cc), F32),
                        pltpu.VMEM((tt, NCHAIN, cc), F32)],
        compiler_params=_cparams(("arbitrary", "arbitrary")),
        name="conv",
    )(u, u, u, prev_ok, next_ok, wb, row(conv_b), row(ln_g), row(ln_b))


S5_LANE_BLOCK = 512
S5_ROW_CHUNKS = {False: 2, True: 4}


def _s5_kernel(u_ref, wb_ref, wc_ref, are_ref, aim_ref, s0_ref, *rest, reverse, emit_y):
    if emit_y:
        y_ref, sfin_ref, bu_ref, st_ref, carry_ref = rest
    else:
        sfin_ref, bu_ref, carry_ref = rest
        st_ref = None
    i = pl.program_id(1)
    tt = u_ref.shape[1]
    nk, ks, sw2 = wb_ref.shape
    sw = sw2 // 2

    @pl.when(i == 0)
    def _():
        carry_ref[...] = s0_ref[0]

    u2 = u_ref[0].reshape(tt * NCHAIN, u_ref.shape[3]).astype(BF16)
    chunks = S5_ROW_CHUNKS[reverse]
    tc = tt // chunks
    order = range(chunks - 1, -1, -1) if reverse else range(chunks)
    for kc in range(nk):
        b = kc % 2
        for m in order:
            rows = slice(m * tc * NCHAIN, (m + 1) * tc * NCHAIN)
            bu_ref[b, m * tc:(m + 1) * tc] = _dot(u2[rows, kc * ks:(kc + 1) * ks],
                                                  wb_ref[kc]).reshape(tc, NCHAIN, sw2)
        for lo in range(0, sw, S5_LANE_BLOCK):
            lb = min(S5_LANE_BLOCK, sw - lo)
            re_sl = slice(lo, lo + lb)
            im_sl = slice(sw + lo, sw + lo + lb)
            ar = are_ref[kc, :, re_sl]
            ai = aim_ref[kc, :, re_sl]
            sre, sim = carry_ref[kc, :, re_sl], carry_ref[kc, :, im_sl]
            for jj in range(tt):
                tloc = tt - 1 - jj if reverse else jj
                nre = ar * sre - ai * sim + bu_ref[b, tloc, :, re_sl]
                nim = ar * sim + ai * sre + bu_ref[b, tloc, :, im_sl]
                if emit_y:
                    st_ref[b, tloc, :, re_sl] = nre
                    st_ref[b, tloc, :, im_sl] = nim
                sre, sim = nre, nim
            carry_ref[kc, :, re_sl] = sre
            carry_ref[kc, :, im_sl] = sim
        if emit_y:
            for m in order:
                st = st_ref[b, m * tc:(m + 1) * tc].reshape(tc * NCHAIN, sw2).astype(BF16)
                y_ref[0, m * tc:(m + 1) * tc, :, kc * ks:(kc + 1) * ks] = (
                    _dot(st, wc_ref[kc]).reshape(tc, NCHAIN, ks))
    sfin_ref[0] = carry_ref[...]


def _s5_pass(s_in, ngroups, wb, wc, are, aim, s0, tt, reverse, emit_y):
    _, ls, _, cs = s_in.shape
    nk, ks, sw2 = wb.shape
    nt = ls // tt
    tmap = (lambda i: nt - 1 - i) if reverse else (lambda i: i)
    state_spec = pl.BlockSpec((1, nk, NCHAIN, sw2), lambda g, i: (g, 0, 0, 0))
    out_specs = [state_spec]
    out_shape = [jax.ShapeDtypeStruct((ngroups, nk, NCHAIN, sw2), F32)]
    scratch = [pltpu.VMEM((2, tt, NCHAIN, sw2), F32)]
    if emit_y:
        out_specs.insert(0, pl.BlockSpec((1, tt, NCHAIN, cs), lambda g, i: (g, tmap(i), 0, 0)))
        out_shape.insert(0, jax.ShapeDtypeStruct((ngroups, ls, NCHAIN, cs), F32))
        scratch.append(pltpu.VMEM((2, tt, NCHAIN, sw2), F32))
    scratch.append(pltpu.VMEM((nk, NCHAIN, sw2), F32))
    res = pl.pallas_call(
        functools.partial(_s5_kernel, reverse=reverse, emit_y=emit_y),
        grid=(ngroups, nt),
        in_specs=[pl.BlockSpec((1, tt, NCHAIN, cs), lambda g, i: (g, tmap(i), 0, 0)),
                  _resident((nk, ks, sw2)), _resident((nk, sw2, ks)),
                  _resident((nk, NCHAIN, sw2 // 2)), _resident((nk, NCHAIN, sw2 // 2)),
                  state_spec],
        out_specs=out_specs,
        out_shape=out_shape,
        scratch_shapes=scratch,
        compiler_params=_cparams(("arbitrary", "arbitrary")),
        name="s5_" + ("bwd" if reverse else "fwd") + ("" if emit_y else "_state"),
    )(s_in, wb, wc, are, aim, s0)
    return (res[0], res[1]) if emit_y else (None, res[0])


def _cmul(are, aim, bre, bim):
    return are * bre - aim * bim, are * bim + aim * bre


def _cpow(re, im, n):
    rre, rim = jnp.ones_like(re), jnp.zeros_like(im)
    while n:
        if n & 1:
            rre, rim = _cmul(rre, rim, re, im)
        re, im = _cmul(re, im, re, im)
        n >>= 1
    return rre, rim


def _s5_params(b_re, b_im, a_re, a_im, log_dt, c_re, c_im):
    ng, ns, nh = b_re.shape
    cs = ng * nh
    ks = min(MXU_DIM, cs)
    gps = ks // nh
    nk = cs // ks
    sw = gps * ns
    a_re, a_im = a_re.astype(F32), a_im.astype(F32)
    dt = jnp.exp(log_dt.astype(F32))[:, None]
    mag = jnp.exp(a_re * dt)
    lre, lim = mag * jnp.cos(a_im * dt), mag * jnp.sin(a_im * dt)
    den = a_re * a_re + a_im * a_im
    zre = ((lre - 1.0) * a_re + lim * a_im) / den
    zim = (lim * a_re - (lre - 1.0) * a_im) / den
    cre, cim = _cmul(c_re.astype(F32), c_im.astype(F32), zre[:, None, :], zim[:, None, :])

    eye = jnp.eye(gps, dtype=F32)

    def in_block(b):
        b = b.astype(F32).reshape(nk, gps, ns, nh)
        return jnp.einsum("kgph,gj->kghjp", b, eye).reshape(nk, ks, sw)

    def out_block(c):
        c = c.reshape(nk, gps, nh, ns)
        return jnp.einsum("kghp,gj->kgpjh", c, eye).reshape(nk, sw, ks)

    wb = jnp.concatenate([in_block(b_re), in_block(b_im)], axis=2).astype(BF16)
    wc = jnp.concatenate([out_block(cre), out_block(-cim)], axis=1).astype(BF16)
    bc = lambda v: jnp.broadcast_to(v.reshape(nk, 1, sw), (nk, NCHAIN, sw))
    return dict(wb=wb, wc=wc, are=bc(lre), aim=bc(lim), lre=lre.reshape(nk, sw), lim=lim.reshape(nk, sw))


def _chain_states(local_end, p, seg, ls, reverse):
    nk, _, sw2 = local_end.shape
    sw = sw2 // 2
    pre, pim = _cpow(p["lre"], p["lim"], ls)
    e = local_end.reshape(nk, NCHAIN // seg, seg, sw2)
    ere, eim = e[..., :sw], e[..., sw:]
    zero = jnp.zeros_like(ere[:, :, 0])
    order = range(seg - 1, -1, -1) if reverse else range(seg)
    sre, sim = zero, zero
    out = [None] * seg
    for k in order:
        out[k] = jnp.concatenate([sre, sim], axis=-1)
        mre, mim = _cmul(pre[:, None, :], pim[:, None, :], sre, sim)
        sre, sim = mre + ere[:, :, k], mim + eim[:, :, k]
    return jnp.stack(out, axis=2).reshape(nk, NCHAIN, sw2)


def _s5_both(s_in, seg, pf, pb, tt):
    ls = s_in.shape[1]
    ys = []
    for p, reverse in ((pf, False), (pb, True)):
        zero = jnp.zeros((1, p["wb"].shape[0], NCHAIN, p["wb"].shape[2]), F32)
        s0 = zero
        if seg > 1:
            _, local_end = _s5_pass(s_in, 1, p["wb"], p["wc"], p["are"], p["aim"], zero, tt, reverse, False)
            s0 = _chain_states(local_end[0], p, seg, ls, reverse)[None]
        y, _ = _s5_pass(s_in, 2, p["wb"], p["wc"], p["are"], p["aim"], jnp.concatenate([s0, zero]), tt,
                        reverse, True)
        ys.append(y)
    return ys


def _glu_kernel(yf_ref, yb_ref, s_ref, d_ref, w_ref, b_ref, o_ref):
    y = yf_ref[...] + yb_ref[...] + d_ref[...] * s_ref[...]
    g = jax.nn.gelu(y)
    o_ref[...] = (g * _sigmoid(_dot(g.astype(BF16), w_ref[...]) + b_ref[...])).astype(o_ref.dtype)


def _glu(yf, yb, s_in, d_skip, glu_w, glu_b, tm):
    rows, cs = yf.shape
    rspec = pl.BlockSpec((tm, cs), lambda i: (i, 0))
    c2 = lambda i: (0, 0)
    return pl.pallas_call(
        _glu_kernel,
        grid=(rows // tm,),
        in_specs=[rspec, rspec, rspec, pl.BlockSpec((1, cs), c2), pl.BlockSpec((cs, cs), c2),
                  pl.BlockSpec((1, cs), c2)],
        out_specs=rspec,
        out_shape=jax.ShapeDtypeStruct((rows, cs), BF16),
        compiler_params=_cparams(("arbitrary",)),
        name="s5_glu",
    )(yf, yb, s_in, d_skip, glu_w, glu_b)


def _pack_halves(v):
    n = v.shape[-1] // 2
    bits = lax.bitcast_convert_type(v.astype(BF16).astype(F32), U32)
    return (bits[..., :n] >> 16) | (bits[..., n:] & jnp.uint32(0xFFFF0000))


def _unpack_halves(p):
    return (lax.bitcast_convert_type(p << 16, F32),
            lax.bitcast_convert_type(p & jnp.uint32(0xFFFF0000), F32))


OUTPROJ_SPLIT = 2


def _outproj_kernel(xp_ref, xs_ref, co_ref, so_ref, g1_ref, sc2_ref, sh2_ref, wo1_ref, wo2_ref, bo_ref,
                    lg_ref, lb_ref, rwh_ref, rwl_ref, x1_ref, hp_ref, lo_ref, *, alpha):
    g = pl.program_id(0)
    tt, d = xp_ref.shape[1], xp_ref.shape[2]
    nj = d // 2 // LANES
    th = tt // OUTPROJ_SPLIT
    for h in range(OUTPROJ_SPLIT):
        ts = slice(h * th, (h + 1) * th)
        rows = slice(h * th * NCHAIN, (h + 1) * th * NCHAIN)
        x = jnp.where(g == 0, xp_ref[:, ts, :], xs_ref[:, ts, :])
        mix = _dot(co_ref[0, rows, :], wo1_ref[...]) + _dot(so_ref[0, rows, :], wo2_ref[...]) + bo_ref[...]
        mix = pltpu.einshape("tcd->ctd", mix.reshape(th, NCHAIN, d))
        x1 = _layer_norm(alpha * x + g1_ref[0] * mix, lg_ref[...], lb_ref[...])
        h2 = x1 * (1.0 + sc2_ref[0]) + sh2_ref[0]
        x1_ref[0, :, ts, :] = x1
        hb = h2.astype(BF16)
        h2f = h2.reshape(NCHAIN * th, d)
        packed = _pack_halves(h2f)
        for j in range(nj):
            hp_ref[0, :, pl.ds(h * th * nj + j, th, stride=nj), :] = (
                packed[:, j * LANES:(j + 1) * LANES].reshape(NCHAIN, th, LANES))
        hi = hb.reshape(NCHAIN * th, d)
        lo = (h2f - hi.astype(F32)).astype(BF16)
        logits = _dot(hi, rwh_ref[...]) + (_dot(hi, rwl_ref[...]) + _dot(lo, rwh_ref[...]))
        lo_ref[0, :, ts, :] = logits.reshape(NCHAIN, th, logits.shape[-1])


def _outproj(xp, xs, conv_out, ssm_out, gate1, scale2, shift2, wo1, wo2, b_out, ln_g, ln_b, rw_hi, rw_lo,
             alpha, tt):
    _, ls, d = xp.shape
    cc, cs = wo1.shape[0], wo2.shape[0]
    ne = rw_hi.shape[1]
    nt = ls // tt
    nj = d // 2 // LANES
    xp_spec, xs_spec = _x_specs(tt, d, nt)
    c2 = lambda g, t: (0, 0)
    nat = lambda w: pl.BlockSpec((1, NCHAIN, tt, w), lambda g, t: (g, 0, t, 0))
    return pl.pallas_call(
        functools.partial(_outproj_kernel, alpha=alpha),
        grid=(2, nt),
        in_specs=[xp_spec, xs_spec,
                  pl.BlockSpec((1, tt * NCHAIN, cc), lambda g, t: (g, t, 0)),
                  pl.BlockSpec((1, tt * NCHAIN, cs), lambda g, t: (g, t, 0)),
                  _mod_spec(d), _mod_spec(d), _mod_spec(d),
                  _resident((cc, d)), _resident((cs, d)), pl.BlockSpec((1, d), c2),
                  pl.BlockSpec((1, d), c2), pl.BlockSpec((1, d), c2),
                  _resident((d, ne)), _resident((d, ne))],
        out_specs=[nat(d),
                   pl.BlockSpec((1, NCHAIN, tt * nj, LANES), lambda g, t: (g, 0, t, 0)),
                   nat(ne)],
        out_shape=[jax.ShapeDtypeStruct((2, NCHAIN, ls, d), F32),
                   jax.ShapeDtypeStruct((2, NCHAIN, ls * nj, LANES), U32),
                   jax.ShapeDtypeStruct((2, NCHAIN, ls, ne), F32)],
        compiler_params=_cparams(("arbitrary", "arbitrary")),
        name="outproj",
    )(xp, xs, conv_out, ssm_out, gate1, scale2, shift2, wo1, wo2, b_out, ln_g, ln_b, rw_hi, rw_lo)


def _route_kernel(lg_ref, bias_ref, eid_ref, e_ref, w_ref, r_ref, cnt_ref, carry_ref):
    i = pl.program_id(0)
    tm, ne = lg_ref.shape
    gs = ne // N_EXPERT_GROUPS
    neg = jnp.float32(-jnp.inf)

    @pl.when(i == 0)
    def _():
        carry_ref[...] = jnp.zeros_like(carry_ref)

    scores = _sigmoid(lg_ref[...]).T
    biased = scores + bias_ref[...]
    eid = eid_ref[...]

    def first_max(v, ids):
        m = jnp.max(v, axis=0, keepdims=True)
        idx = jnp.min(jnp.where(v == m, ids, float(ne)), axis=0, keepdims=True)
        return m, idx

    gscore = []
    for q in range(N_EXPERT_GROUPS):
        vg, ids = biased[q * gs:(q + 1) * gs], eid[q * gs:(q + 1) * gs]
        m1, i1 = first_max(vg, ids)
        m2 = jnp.max(jnp.where(ids == i1, neg, vg), axis=0, keepdims=True)
        gscore.append(m1 + m2)
    parts = []
    for q in range(N_EXPERT_GROUPS):
        beaten = jnp.zeros((1, tm), F32)
        for o in range(N_EXPERT_GROUPS):
            if o != q:
                wins = (gscore[o] >= gscore[q]) if o < q else (gscore[o] > gscore[q])
                beaten = beaten + wins.astype(F32)
        parts.append(jnp.where(beaten < TOPK_GROUPS, biased[q * gs:(q + 1) * gs], neg))
    masked = jnp.concatenate(parts, axis=0)

    chosen = jnp.zeros((ne, tm), F32)
    hits, e_rows, w_rows = [], [], []
    for k in range(TOP_K):
        _, idx = first_max(masked, eid)
        hit = eid == idx
        hits.append(hit)
        w_rows.append(jnp.sum(jnp.where(hit, scores, 0.0), axis=0, keepdims=True))
        e_rows.append(idx)
        masked = jnp.where(hit, neg, masked)
        chosen = jnp.where(hit, 1.0, chosen)
    w_t = jnp.concatenate(w_rows, axis=0)
    w_ref[...] = w_t / jnp.sum(w_t, axis=0, keepdims=True) * ROUTED_SCALE
    e_ref[...] = jnp.concatenate(e_rows, axis=0).astype(I32)

    rr = lax.broadcasted_iota(I32, (tm, tm), 0)
    cc = lax.broadcasted_iota(I32, (tm, tm), 1)
    before = (rr < cc).astype(BF16)
    prefix = _dot(chosen.astype(BF16), before) + carry_ref[...]
    r_rows = [jnp.sum(jnp.where(hits[k], prefix, 0.0), axis=0, keepdims=True) for k in range(TOP_K)]
    r_ref[...] = jnp.concatenate(r_rows, axis=0).astype(I32)
    carry_ref[...] = carry_ref[...] + jnp.sum(chosen, axis=1, keepdims=True)
    cnt_ref[...] = carry_ref[...]


def _route(logits, bias, tm):
    t, ne = logits.shape
    kspec = pl.BlockSpec((TOP_K, tm), lambda i: (0, i))
    col = pl.BlockSpec((ne, 1), lambda i: (0, 0))
    eid = jnp.broadcast_to(jnp.arange(ne, dtype=F32)[:, None], (ne, tm))
    return pl.pallas_call(
        _route_kernel,
        grid=(t // tm,),
        in_specs=[pl.BlockSpec((tm, ne), lambda i: (i, 0)), col, _resident((ne, tm))],
        out_specs=[kspec, kspec, kspec, col],
        out_shape=[jax.ShapeDtypeStruct((TOP_K, t), I32), jax.ShapeDtypeStruct((TOP_K, t), F32),
                   jax.ShapeDtypeStruct((TOP_K, t), I32), jax.ShapeDtypeStruct((ne, 1), F32)],
        scratch_shapes=[pltpu.VMEM((ne, 1), F32)],
        compiler_params=_cparams(("arbitrary",)),
        name="route",
    )(logits, bias.reshape(ne, 1), eid)


def _dest_kernel(e_ref, r_ref, ps_ref, eid_ref, o_ref):
    e = e_ref[...].astype(F32)
    starts = [jnp.sum(jnp.where(eid_ref[...] == e[k:k + 1], ps_ref[...], 0.0), axis=0, keepdims=True)
              for k in range(TOP_K)]
    o_ref[...] = r_ref[...] + jnp.concatenate(starts, axis=0).astype(I32)


def _dest(e_idx, rank, pad_start, tm):
    t = e_idx.shape[1]
    ne = pad_start.shape[0]
    kspec = pl.BlockSpec((TOP_K, tm), lambda i: (0, i))
    eid = jnp.broadcast_to(jnp.arange(ne, dtype=F32)[:, None], (ne, tm))
    return pl.pallas_call(
        _dest_kernel,
        grid=(t // tm,),
        in_specs=[kspec, kspec, pl.BlockSpec((ne, 1), lambda i: (0, 0)), _resident((ne, tm))],
        out_specs=kspec,
        out_shape=jax.ShapeDtypeStruct((TOP_K, t), I32),
        compiler_params=_cparams(("arbitrary",)),
        name="dest",
    )(e_idx, rank, pad_start, eid)


def _dispatch_kernel(last_ref, nu_ref, dest_ref, h_ref, sg_ref, su_ref, sd_ref, xs_ref, sh_ref, zero_ref, sem, zsem,
                     *, nj, tile_rows):
    i = pl.program_id(0)
    tm = h_ref.shape[0] // nj
    ne = last_ref.shape[0]

    @pl.when(i == 0)
    def _():
        zero_ref[...] = jnp.zeros_like(zero_ref)

        def fill(row):
            start = pl.multiple_of(row * nj, 8)
            return pltpu.make_async_copy(zero_ref, xs_ref.at[pl.ds(start, tile_rows * nj)], zsem)

        def issue(e, c):
            @pl.when(last_ref[e] >= 0)
            def _():
                fill(last_ref[e]).start()
            return c

        def drain(e, c):
            @pl.when(last_ref[e] >= 0)
            def _():
                fill(last_ref[e]).wait()
            return c

        lax.fori_loop(0, ne, issue, 0)
        lax.fori_loop(0, ne, drain, 0)
        n_tiles = xs_ref.shape[0] // (tile_rows * nj)
        lax.fori_loop(nu_ref[0], n_tiles, lambda q, c: (fill(q * tile_rows).start(), c)[1], 0)
        lax.fori_loop(nu_ref[0], n_tiles, lambda q, c: (fill(q * tile_rows).wait(), c)[1], 0)

    def issue_rows(r, c):
        src = h_ref.at[pl.ds(pl.multiple_of(r * nj, nj), nj)]
        for k in range(TOP_K):
            dst = xs_ref.at[pl.ds(pl.multiple_of(dest_ref[r * TOP_K + k] * nj, nj), nj)]
            pltpu.make_async_copy(src, dst, sem).start()
        return c

    lax.fori_loop(0, tm, issue_rows, 0)

    x = _unpack_rows(h_ref, tm, nj)
    gate = _dot(x, sg_ref[...])
    up = _dot(x, su_ref[...])
    sh_ref[...] = _dot((gate * _sigmoid(gate) * up).astype(BF16), sd_ref[...]).astype(sh_ref.dtype)

    for _ in range(TOP_K):
        pltpu.make_async_copy(h_ref, xs_ref.at[pl.ds(0, tm * nj)], sem).wait()


def _dispatch(last_tile_row, n_used, dest_flat, h_packed, sh_gate, sh_up, sh_down, n_rows, nj, tile_rows, tm):
    t = h_packed.shape[0] // nj
    d, ff = sh_gate.shape
    grid_spec = pltpu.PrefetchScalarGridSpec(
        num_scalar_prefetch=2,
        grid=(t // tm,),
        in_specs=[pl.BlockSpec((tm * TOP_K,), lambda i, last, nu: (i,), memory_space=pltpu.SMEM),
                  pl.BlockSpec((tm * nj, LANES), lambda i, last, nu: (i, 0)),
                  _resident((d, ff)), _resident((d, ff)), _resident((ff, d))],
        out_specs=[pl.BlockSpec(memory_space=pl.ANY), pl.BlockSpec((tm, d), lambda i, last, nu: (i, 0))],
        scratch_shapes=[pltpu.VMEM((tile_rows * nj, LANES), U32), pltpu.SemaphoreType.DMA,
                        pltpu.SemaphoreType.DMA],
    )
    return pl.pallas_call(
        functools.partial(_dispatch_kernel, nj=nj, tile_rows=tile_rows),
        grid_spec=grid_spec,
        out_shape=[jax.ShapeDtypeStruct((n_rows * nj, LANES), U32), jax.ShapeDtypeStruct((t, d), BF16)],
        compiler_params=_cparams(("arbitrary",)),
        name="dispatch",
    )(last_tile_row, n_used, dest_flat, h_packed, sh_gate, sh_up, sh_down)


def _unpack_rows(p_ref, tm, nj):
    lo, hi = [], []
    for j in range(nj):
        l, h = _unpack_halves(p_ref[pl.ds(j, tm, stride=nj), :])
        lo.append(l.astype(BF16))
        hi.append(h.astype(BF16))
    return jnp.concatenate(lo + hi, axis=-1)


CAST_ROWS = 128


def _experts_kernel(te_ref, nu_ref, nx_ref, x_ref, wg_hbm, wu_hbm, wd_hbm, y_ref,
                    wgf_ref, wuf_ref, wdf_ref, wgb_ref, wub_ref, wdb_ref, sem, *, nj):
    i = pl.program_id(0)
    tm = x_ref.shape[0] // nj
    live = i < nu_ref[0]
    e = te_ref[i]
    first = live & ((i == 0) | (e != te_ref[jnp.maximum(i - 1, 0)]))

    def fetch(expert):
        return (pltpu.make_async_copy(wg_hbm.at[expert], wgf_ref, sem.at[0]),
                pltpu.make_async_copy(wu_hbm.at[expert], wuf_ref, sem.at[1]),
                pltpu.make_async_copy(wd_hbm.at[expert], wdf_ref, sem.at[2]))

    @pl.when(live & (i == 0))
    def _():
        for c in fetch(e):
            c.start()

    @pl.when(first)
    def _():
        for c in fetch(e):
            c.wait()
        for src, dst in ((wgf_ref, wgb_ref), (wuf_ref, wub_ref), (wdf_ref, wdb_ref)):
            for r in range(0, src.shape[0], CAST_ROWS):
                dst[r:r + CAST_ROWS] = src[r:r + CAST_ROWS].astype(BF16)

        @pl.when(nx_ref[i] >= 0)
        def _():
            for c in fetch(nx_ref[i]):
                c.start()

    @pl.when(live)
    def _():
        x = _unpack_rows(x_ref, tm, nj)
        gate = _dot(x, wgb_ref[...])
        up = _dot(x, wub_ref[...])
        act = (gate * _sigmoid(gate) * up).astype(BF16)
        y = _dot(act, wdb_ref[...])
        packed = _pack_halves(y)
        for j in range(nj):
            y_ref[pl.ds(j, tm, stride=nj), :] = packed[:, j * LANES:(j + 1) * LANES]

    @pl.when(i >= nu_ref[0])
    def _():
        y_ref[...] = jnp.zeros_like(y_ref)


def _experts(tile_e, n_used, next_e, x_sorted, w_gate, w_up, w_down, nj, tm):
    n_rows = x_sorted.shape[0] // nj
    ne, d, ff = w_gate.shape
    n_tiles = n_rows // tm
    hbm = pl.BlockSpec(memory_space=pl.ANY)
    grid_spec = pltpu.PrefetchScalarGridSpec(
        num_scalar_prefetch=3,
        grid=(n_tiles,),
        in_specs=[pl.BlockSpec((tm * nj, LANES), lambda i, te, nu, nx: (jnp.minimum(i, nu[0] - 1), 0)),
                  hbm, hbm, hbm],
        out_specs=pl.BlockSpec((tm * nj, LANES), lambda i, te, nu, nx: (i, 0)),
        scratch_shapes=[pltpu.VMEM((d, ff), F32), pltpu.VMEM((d, ff), F32), pltpu.VMEM((ff, d), F32),
                        pltpu.VMEM((d, ff), BF16), pltpu.VMEM((d, ff), BF16), pltpu.VMEM((ff, d), BF16),
                        pltpu.SemaphoreType.DMA((3,))],
    )
    return pl.pallas_call(
        functools.partial(_experts_kernel, nj=nj),
        grid_spec=grid_spec,
        out_shape=jax.ShapeDtypeStruct((n_rows * nj, LANES), U32),
        compiler_params=_cparams(("arbitrary",)),
        name="experts",
    )(tile_e, n_used, next_e, x_sorted, w_gate, w_up, w_down)


COMBINE_ROWS = 16


def _combine_kernel(dest_ref, dnext_ref, w_ref, x1_ref, sh_ref, g2_ref, lg_ref, lb_ref,
                    ys_ref, op_ref, os_ref, buf_ref, acc_ref, sem, *, alpha, half, nj):
    i = pl.program_id(0)
    n = pl.num_programs(0)
    tm, d = sh_ref.shape
    slot = i % 2

    def gather(d_ref, s):
        def issue(r, c):
            for k in range(TOP_K):
                src = ys_ref.at[pl.ds(pl.multiple_of(d_ref[r * TOP_K + k] * nj, nj), nj)]
                dst = buf_ref.at[s, k, pl.ds(pl.multiple_of(r * nj, nj), nj)]
                pltpu.make_async_copy(src, dst, sem.at[s]).start()
            return c

        lax.fori_loop(0, tm, issue, 0)

    @pl.when(i == 0)
    def _():
        gather(dest_ref, 0)

    @pl.when(i + 1 < n)
    def _():
        gather(dnext_ref, 1 - slot)

    for k in range(TOP_K):
        pltpu.make_async_copy(ys_ref.at[pl.ds(0, tm * nj)], buf_ref.at[slot, k], sem.at[slot]).wait()

    def reduce_rows(c, carry):
        r0 = pl.multiple_of(c * COMBINE_ROWS, COMBINE_ROWS)
        w = w_ref[pl.ds(r0, COMBINE_ROWS), :]
        lo = [None] * nj
        hi = [None] * nj
        for k in range(TOP_K):
            wk = jnp.broadcast_to(w[:, k:k + 1], (COMBINE_ROWS, LANES))
            for j in range(nj):
                l, h = _unpack_halves(buf_ref[slot, k, pl.ds(r0 * nj + j, COMBINE_ROWS, stride=nj), :])
                lo[j] = l * wk if k == 0 else lo[j] + l * wk
                hi[j] = h * wk if k == 0 else hi[j] + h * wk
        acc_ref[pl.ds(r0, COMBINE_ROWS), :] = jnp.concatenate(lo + hi, axis=-1)
        return carry

    lax.fori_loop(0, tm // COMBINE_ROWS, reduce_rows, 0)
    ffn = acc_ref[...] + sh_ref[...].astype(F32)
    out = _layer_norm(alpha * x1_ref[...] + g2_ref[0] * ffn, lg_ref[...], lb_ref[...])

    @pl.when(i < half)
    def _():
        op_ref[...] = out

    @pl.when(i >= half)
    def _():
        os_ref[...] = out


def _combine(dest_flat, w, x1, shared, gate2_rows, y_sorted, ln_g, ln_b, alpha, nj, tm, ls):
    t, d = shared.shape
    n = t // tm
    half = n // 2
    per_chain = ls // tm
    assert tm % COMBINE_ROWS == 0
    c2 = lambda i: (0, 0)
    pmap = lambda i: (jnp.minimum(i, half - 1), 0)
    smap = lambda i: (jnp.maximum(i - half, 0), 0)
    return pl.pallas_call(
        functools.partial(_combine_kernel, alpha=alpha, half=half, nj=nj),
        grid=(n,),
        in_specs=[pl.BlockSpec((tm * TOP_K,), lambda i: (i,), memory_space=pltpu.SMEM),
                  pl.BlockSpec((tm * TOP_K,), lambda i: (jnp.minimum(i + 1, n - 1),), memory_space=pltpu.SMEM),
                  pl.BlockSpec((tm, TOP_K), lambda i: (i, 0)),
                  pl.BlockSpec((tm, d), lambda i: (i, 0)),
                  pl.BlockSpec((tm, d), lambda i: (i, 0)),
                  pl.BlockSpec((1, 1, d), lambda i: (i // per_chain, 0, 0)),
                  pl.BlockSpec((1, d), c2), pl.BlockSpec((1, d), c2),
                  pl.BlockSpec(memory_space=pl.ANY)],
        out_specs=[pl.BlockSpec((tm, d), pmap), pl.BlockSpec((tm, d), smap)],
        out_shape=[jax.ShapeDtypeStruct((t // 2, d), F32), jax.ShapeDtypeStruct((t // 2, d), F32)],
        scratch_shapes=[pltpu.VMEM((2, TOP_K, tm * nj, LANES), U32), pltpu.VMEM((tm, d), F32),
                        pltpu.SemaphoreType.DMA((2,))],
        compiler_params=_cparams(("arbitrary",)),
        name="combine",
    )(dest_flat, dest_flat, w, x1, shared, gate2_rows, ln_g, ln_b, y_sorted)


def _tile(pref, n, mult=8):
    t = min(pref, n)
    while n % t or t % mult:
        t -= 1
    return t


def _encoder_layer(xp, xs, c_all, chain_seq, seg, alpha, p):
    _, ls, d = xp.shape
    cc = p["conv_w"].shape[-1]

    ada = _ada(c_all, p["w_ada"], p["b_ada"])
    mods = ada.reshape(ada.shape[0], N_ADA, d)[chain_seq].reshape(2, NCHAIN, N_ADA, 1, d)
    shift1, scale1, gate1, shift2, scale2, gate2 = (mods[:, :, k] for k in range(N_ADA))

    w_in = p["w_in"].astype(BF16)
    b_in = p["b_in"].reshape(1, -1)
    u, s_in = _inproj(xp, xs, scale1, shift1, w_in[:, :cc], w_in[:, cc:2 * cc], w_in[:, 2 * cc:],
                      b_in[:, :cc], b_in[:, cc:2 * cc], b_in[:, 2 * cc:], _tile(TILES["inproj"], ls))

    chain = np.arange(NCHAIN)
    prev_ok = jnp.asarray(np.stack([(chain % seg != 0), np.zeros(NCHAIN, bool)]).astype(np.float32)[..., None])
    next_ok = jnp.asarray(np.stack([(chain % seg != seg - 1), np.zeros(NCHAIN, bool)]).astype(np.float32)[..., None])
    conv_out = _conv(u, prev_ok, next_ok, p["conv_w"], p["conv_b"], p["conv_ln_g"], p["conv_ln_b"],
                     _tile(TILES["conv"], ls, CONV_HALO))

    pf = _s5_params(p["ssm_b_re"], p["ssm_b_im"], p["ssm_a_re_f"], p["ssm_a_im_f"], p["ssm_log_dt_f"],
                    p["ssm_c_re_f"], p["ssm_c_im_f"])
    pb = _s5_params(p["ssm_b_re"], p["ssm_b_im"], p["ssm_a_re_b"], p["ssm_a_im_b"], p["ssm_log_dt_b"],
                    p["ssm_c_re_b"], p["ssm_c_im_b"])
    yf, yb = _s5_both(s_in, seg, pf, pb, _tile(TILES["s5"], ls))
    cs = s_in.shape[-1]
    rows_tm = 2 * ls * NCHAIN
    flat = lambda a: a.reshape(rows_tm, cs)
    ssm_out = _glu(flat(yf), flat(yb), flat(s_in), p["ssm_d"].reshape(1, cs),
                   p["ssm_glu_w"].astype(BF16), p["ssm_glu_b"].reshape(1, cs), _tile(TILES["glu"], rows_tm))
    ssm_out = ssm_out.reshape(2, ls * NCHAIN, cs)

    w_out = p["w_out"].astype(BF16)
    rw = p["router_w"].astype(F32)
    rw_hi = rw.astype(BF16)
    rw_lo = (rw - rw_hi.astype(F32)).astype(BF16)
    row = lambda v: v.reshape(1, -1)
    x1, hp, logits = _outproj(xp, xs, conv_out, ssm_out, gate1, scale2, shift2, w_out[:cc], w_out[cc:],
                              row(p["b_out"]), row(p["ln1_g"]), row(p["ln1_b"]), rw_hi, rw_lo,
                              alpha, _tile(TILES["outproj"], ls, 16))

    t = 2 * NCHAIN * ls
    ne = rw.shape[1]
    nj = d // 2 // LANES
    e_idx, e_w, rank, counts = _route(logits.reshape(t, ne), row(p["router_bias"]).astype(F32),
                                      _tile(TILES["route"], t, LANES))

    tm_e = _tile(TILES["expert"], t)
    n_rows = t * TOP_K + ne * tm_e
    counts = counts.reshape(ne).astype(I32)
    e_w = e_w.T
    padded = (counts + tm_e - 1) // tm_e * tm_e
    pad_end = jnp.cumsum(padded)
    pad_start = pad_end - padded
    n_tiles = n_rows // tm_e
    tile_start = jnp.arange(n_tiles, dtype=I32)[:, None] * tm_e
    tile_e = jnp.minimum(jnp.sum((pad_end[None, :] <= tile_start).astype(I32), axis=1), ne - 1)
    n_used = (pad_end[-1:] // tm_e).astype(I32)
    last_tile_row = jnp.where(counts > 0, pad_end - tm_e, -1).astype(I32)
    following = pad_end[tile_e] // tm_e
    next_e = jnp.where(following < n_used[0], tile_e[jnp.minimum(following, n_tiles - 1)], -1).astype(I32)
    dest = _dest(e_idx, rank, pad_start.astype(F32).reshape(ne, 1), _tile(TILES["dest"], t, LANES))
    dest = dest.T.reshape(t * TOP_K)

    x_sorted, shared = _dispatch(last_tile_row, n_used, dest, hp.reshape(t * nj, LANES),
                                 p["sh_w_gate"].astype(BF16), p["sh_w_up"].astype(BF16),
                                 p["sh_w_down"].astype(BF16), n_rows, nj, tm_e, _tile(TILES["dispatch"], ls))
    y_sorted = _experts(tile_e, n_used, next_e, x_sorted, p["exp_w_gate"], p["exp_w_up"], p["exp_w_down"],
                        nj, tm_e)
    tm_c = _tile(TILES["combine"], ls, COMBINE_ROWS)
    gate2_rows = gate2.reshape(2 * NCHAIN, 1, d)
    yp, ys = _combine(dest, e_w, x1.reshape(t, d), shared, gate2_rows, y_sorted,
                      row(p["ln2_g"]), row(p["ln2_b"]), alpha, nj, tm_c, ls)
    return yp.reshape(NCHAIN, ls, d), ys.reshape(NCHAIN, ls, d)


_PARAM_NAMES = ("w_ada", "b_ada", "w_in", "b_in", "conv_w", "conv_b", "conv_ln_g", "conv_ln_b",
                "ssm_b_re", "ssm_b_im", "ssm_a_re_f", "ssm_a_im_f", "ssm_log_dt_f",
                "ssm_a_re_b", "ssm_a_im_b", "ssm_log_dt_b", "ssm_c_re_f", "ssm_c_im_f",
                "ssm_c_re_b", "ssm_c_im_b", "ssm_d", "ssm_glu_w", "ssm_glu_b", "w_out", "b_out",
                "ln1_g", "ln1_b", "router_w", "router_bias", "exp_w_gate", "exp_w_up", "exp_w_down",
                "sh_w_gate", "sh_w_up", "sh_w_down", "ln2_g", "ln2_b")


def kernel(x_prompt, x_sample, c_prompt, c_sample, w_ada, b_ada, w_in, b_in, conv_w, conv_b, conv_ln_g, conv_ln_b, ssm_b_re, ssm_b_im, ssm_a_re_f, ssm_a_im_f, ssm_log_dt_f, ssm_a_re_b, ssm_a_im_b, ssm_log_dt_b, ssm_c_re_f, ssm_c_im_f, ssm_c_re_b, ssm_c_im_b, ssm_d, ssm_glu_w, ssm_glu_b, w_out, b_out, ln1_g, ln1_b, router_w, router_bias, exp_w_gate, exp_w_up, exp_w_down, sh_w_gate, sh_w_up, sh_w_down, ln2_g, ln2_b):
    stacked = (w_ada, b_ada, w_in, b_in, conv_w, conv_b, conv_ln_g, conv_ln_b, ssm_b_re, ssm_b_im,
               ssm_a_re_f, ssm_a_im_f, ssm_log_dt_f, ssm_a_re_b, ssm_a_im_b, ssm_log_dt_b, ssm_c_re_f,
               ssm_c_im_f, ssm_c_re_b, ssm_c_im_b, ssm_d, ssm_glu_w, ssm_glu_b, w_out, b_out, ln1_g, ln1_b,
               router_w, router_bias, exp_w_gate, exp_w_up, exp_w_down, sh_w_gate, sh_w_up, sh_w_down,
               ln2_g, ln2_b)
    depth = w_ada.shape[0]
    alpha = (2 * depth) ** 0.25
    bp, lp, d = x_prompt.shape
    bs, lsample, _ = x_sample.shape
    assert NCHAIN % bp == 0 and bs == NCHAIN and lp % (NCHAIN // bp) == 0
    seg = NCHAIN // bp
    ls = lp // seg
    assert ls == lsample

    c_all = jnp.concatenate([c_prompt, c_sample], axis=0)
    c_all = jnp.pad(c_all, ((0, -c_all.shape[0] % 8), (0, 0)))
    chain_seq = np.concatenate([np.arange(NCHAIN) // seg, bp + np.arange(NCHAIN)])

    xp = x_prompt.reshape(NCHAIN, ls, d)
    xs = x_sample.reshape(NCHAIN, ls, d)
    for l in range(depth):
        params = {n: v[l] for n, v in zip(_PARAM_NAMES, stacked)}
        xp, xs = _encoder_layer(xp, xs, c_all, chain_seq, seg, alpha, params)
    return xp.reshape(bp, lp, d), xs.reshape(bs, lsample, d)
```

```python
import functools
import math

import numpy as np
import jax
import jax.numpy as jnp
from jax import lax
from jax.experimental import pallas as pl
from jax.experimental.pallas import tpu as pltpu

F32 = jnp.float32
BF16 = jnp.bfloat16
U32 = jnp.uint32
I32 = jnp.int32

LN_EPS = 1e-5
N_ADA = 6
TOP_K = 8
N_EXPERT_GROUPS = 8
TOPK_GROUPS = 4
ROUTED_SCALE = 2.5

NCHAIN = 8
LANES = 128
MXU_DIM = 256
VMEM_LIMIT = 56 * 1024 * 1024

TILES = dict(inproj=64, conv=128, s5=128, glu=1024, outproj=32, route=512, dest=1024, expert=512,
             dispatch=512, combine=256)


def _cparams(sem):
    return pltpu.CompilerParams(dimension_semantics=sem, vmem_limit_bytes=VMEM_LIMIT)


def _resident(shape):
    zeros = (0,) * len(shape)
    return pl.BlockSpec(shape, lambda *_: zeros, pipeline_mode=pl.Buffered(1))


def _dot(a, b):
    return jnp.dot(a, b, preferred_element_type=F32)


def _sigmoid(x):
    return jax.nn.sigmoid(x)


def _layer_norm(v, g, b):
    mu = jnp.mean(v, axis=-1, keepdims=True)
    d = v - mu
    var = jnp.mean(d * d, axis=-1, keepdims=True)
    return d * lax.rsqrt(var + LN_EPS) * g + b


def _ada_kernel(c_ref, w_ref, b_ref, o_ref):
    c = c_ref[...]
    s = (c * _sigmoid(c)).astype(BF16)
    o_ref[...] = _dot(s, w_ref[...].astype(BF16)) + b_ref[...]


def _ada(c_all, w_ada, b_ada):
    rows, d = c_all.shape
    n = w_ada.shape[1]
    tn = _tile(2048, n, LANES)
    return pl.pallas_call(
        _ada_kernel,
        grid=(n // tn,),
        in_specs=[pl.BlockSpec((rows, d), lambda j: (0, 0)),
                  pl.BlockSpec((d, tn), lambda j: (0, j)),
                  pl.BlockSpec((1, tn), lambda j: (0, j))],
        out_specs=pl.BlockSpec((rows, tn), lambda j: (0, j)),
        out_shape=jax.ShapeDtypeStruct((rows, n), F32),
        compiler_params=_cparams(("arbitrary",)),
        name="ada",
    )(c_all, w_ada, b_ada.reshape(1, n))


def _x_specs(tt, d, nt):
    xp = pl.BlockSpec((NCHAIN, tt, d), lambda g, t: (0, jnp.where(g == 0, t, nt - 1), 0))
    xs = pl.BlockSpec((NCHAIN, tt, d), lambda g, t: (0, jnp.where(g == 1, t, 0), 0))
    return xp, xs


def _mod_spec(d):
    return pl.BlockSpec((1, NCHAIN, 1, d), lambda g, t: (g, 0, 0, 0))


def _inproj_kernel(xp_ref, xs_ref, sc_ref, sh_ref, wa_ref, wg_ref, ws_ref, ba_ref, bg_ref, bs_ref,
                   u_ref, s_ref):
    g = pl.program_id(0)
    x = jnp.where(g == 0, xp_ref[...], xs_ref[...])
    h = x * (1.0 + sc_ref[0]) + sh_ref[0]
    tt = h.shape[1]
    ht = pltpu.einshape("ctd->tcd", h).reshape(tt * NCHAIN, h.shape[2]).astype(BF16)
    a = _dot(ht, wa_ref[...]) + ba_ref[...]
    gt = _dot(ht, wg_ref[...]) + bg_ref[...]
    u = a * _sigmoid(gt)
    s = _dot(ht, ws_ref[...]) + bs_ref[...]
    u_ref[0] = u.reshape(tt, NCHAIN, u.shape[-1])
    s_ref[0] = s.reshape(tt, NCHAIN, s.shape[-1])


def _inproj(xp, xs, scale1, shift1, wa, wg, ws, ba, bg, bs, tt):
    _, ls, d = xp.shape
    cc, cs = wa.shape[1], ws.shape[1]
    nt = ls // tt
    xp_spec, xs_spec = _x_specs(tt, d, nt)
    const2 = lambda g, t: (0, 0)
    return pl.pallas_call(
        _inproj_kernel,
        grid=(2, nt),
        in_specs=[xp_spec, xs_spec, _mod_spec(d), _mod_spec(d),
                  _resident((d, cc)), _resident((d, cc)), _resident((d, cs)),
                  pl.BlockSpec((1, cc), const2), pl.BlockSpec((1, cc), const2), pl.BlockSpec((1, cs), const2)],
        out_specs=[pl.BlockSpec((1, tt, NCHAIN, cc), lambda g, t: (g, t, 0, 0)),
                   pl.BlockSpec((1, tt, NCHAIN, cs), lambda g, t: (g, t, 0, 0))],
        out_shape=[jax.ShapeDtypeStruct((2, ls, NCHAIN, cc), F32),
                   jax.ShapeDtypeStruct((2, ls, NCHAIN, cs), F32)],
        compiler_params=_cparams(("arbitrary", "arbitrary")),
        name="inproj",
    )(xp, xs, scale1, shift1, wa, wg, ws, ba, bg, bs)


CONV_HALO = 16
CONV_BLOCK = 8


def _conv_kernel(u_ref, up_ref, un_ref, pok_ref, nok_ref, w_ref, cb_ref, g_ref, b_ref, o_ref,
                 buf_ref, acc_ref, *, width):
    t = pl.program_id(1)
    nt = pl.num_programs(1)
    tt = u_ref.shape[1]
    pad = (width - 1) // 2
    prev = up_ref[0]
    prev_wrapped = pltpu.roll(prev, shift=1, axis=1) * pok_ref[0]
    buf_ref[0:CONV_HALO] = jnp.where(t == 0, prev_wrapped, prev)
    buf_ref[CONV_HALO:CONV_HALO + tt] = u_ref[0]
    nxt = un_ref[0]
    next_wrapped = pltpu.roll(nxt, shift=NCHAIN - 1, axis=1) * nok_ref[0]
    buf_ref[CONV_HALO + tt:2 * CONV_HALO + tt] = jnp.where(t == nt - 1, next_wrapped, nxt)

    base = CONV_HALO - pad

    for c0 in range(0, u_ref.shape[3], LANES):
        lanes = slice(c0, c0 + LANES)
        taps = [w_ref[k, :, lanes] for k in range(width)]
        bias = cb_ref[:, lanes]

        def body(b, carry, lanes=lanes, taps=taps, bias=bias):
            t0 = b * CONV_BLOCK
            acc = [bias] * CONV_BLOCK
            for s in range(CONV_BLOCK + width - 1):
                x = buf_ref[t0 + base + s, :, lanes]
                for i in range(CONV_BLOCK):
                    if 0 <= s - i < width:
                        acc[i] = acc[i] + x * taps[s - i]
            for i in range(CONV_BLOCK):
                acc_ref[t0 + i, :, lanes] = acc[i]
            return carry

        lax.fori_loop(0, tt // CONV_BLOCK, body, 0)
    v = _layer_norm(acc_ref[...], g_ref[...], b_ref[...])
    v = v * _sigmoid(v)
    o_ref[0] = v.reshape(tt * NCHAIN, v.shape[-1]).astype(o_ref.dtype)


def _conv(u, prev_ok, next_ok, conv_w, conv_b, ln_g, ln_b, tt):
    _, ls, _, cc = u.shape
    width = conv_w.shape[0]
    assert (width - 1) // 2 <= CONV_HALO and tt % CONV_HALO == 0 and cc % LANES == 0
    nt = ls // tt
    hb = tt // CONV_HALO
    nh = ls // CONV_HALO
    wb = jnp.broadcast_to(conv_w[:, None, :], (width, NCHAIN, cc))
    row = lambda v: jnp.broadcast_to(v[None, :], (NCHAIN, cc))
    const2 = lambda g, t: (0, 0)
    return pl.pallas_call(
        functools.partial(_conv_kernel, width=width),
        grid=(2, nt),
        in_specs=[pl.BlockSpec((1, tt, NCHAIN, cc), lambda g, t: (g, t, 0, 0)),
                  pl.BlockSpec((1, CONV_HALO, NCHAIN, cc),
                               lambda g, t: (g, jnp.where(t == 0, nh - 1, t * hb - 1), 0, 0)),
                  pl.BlockSpec((1, CONV_HALO, NCHAIN, cc),
                               lambda g, t: (g, jnp.where(t == nt - 1, 0, (t + 1) * hb), 0, 0)),
                  pl.BlockSpec((1, NCHAIN, 1), lambda g, t: (g, 0, 0)),
                  pl.BlockSpec((1, NCHAIN, 1), lambda g, t: (g, 0, 0)),
                  pl.BlockSpec((width, NCHAIN, cc), lambda g, t: (0, 0, 0)),
                  pl.BlockSpec((NCHAIN, cc), const2), pl.BlockSpec((NCHAIN, cc), const2),
                  pl.BlockSpec((NCHAIN, cc), const2)],
        out_specs=pl.BlockSpec((1, tt * NCHAIN, cc), lambda g, t: (g, t, 0)),
        out_shape=jax.ShapeDtypeStruct((2, ls * NCHAIN, cc), BF16),
        scratch_shapes=[pltpu.VMEM((tt + 2 * CONV_HALO, NCHAIN, cc), F32),
                        pltpu.VMEM((tt, NCHAIN, cc), F32)],
        compiler_params=_cparams(("arbitrary", "arbitrary")),
        name="conv",
    )(u, u, u, prev_ok, next_ok, wb, row(conv_b), row(ln_g), row(ln_b))


S5_LANE_BLOCK = 512
S5_ROW_CHUNKS = {False: 2, True: 4}


def _s5_kernel(u_ref, wb_ref, wc_ref, are_ref, aim_ref, s0_ref, *rest, reverse, emit_y):
    if emit_y:
        y_ref, sfin_ref, bu_ref, st_ref, carry_ref = rest
    else:
        sfin_ref, bu_ref, carry_ref = rest
        st_ref = None
    i = pl.program_id(1)
    tt = u_ref.shape[1]
    nk, ks, sw2 = wb_ref.shape
    sw = sw2 // 2

    @pl.when(i == 0)
    def _():
        carry_ref[...] = s0_ref[0]

    u2 = u_ref[0].reshape(tt * NCHAIN, u_ref.shape[3]).astype(BF16)
    chunks = S5_ROW_CHUNKS[reverse]
    tc = tt // chunks
    order = range(chunks - 1, -1, -1) if reverse else range(chunks)
    for kc in range(nk):
        b = kc % 2
        for m in order:
            rows = slice(m * tc * NCHAIN, (m + 1) * tc * NCHAIN)
            bu_ref[b, m * tc:(m + 1) * tc] = _dot(u2[rows, kc * ks:(kc + 1) * ks],
                                                  wb_ref[kc]).reshape(tc, NCHAIN, sw2)
        for lo in range(0, sw, S5_LANE_BLOCK):
            lb = min(S5_LANE_BLOCK, sw - lo)
            re_sl = slice(lo, lo + lb)
            im_sl = slice(sw + lo, sw + lo + lb)
            ar = are_ref[kc, :, re_sl]
            ai = aim_ref[kc, :, re_sl]
            sre, sim = carry_ref[kc, :, re_sl], carry_ref[kc, :, im_sl]
            for jj in range(tt):
                tloc = tt - 1 - jj if reverse else jj
                nre = ar * sre - ai * sim + bu_ref[b, tloc, :, re_sl]
                nim = ar * sim + ai * sre + bu_ref[b, tloc, :, im_sl]
                if emit_y:
                    st_ref[b, tloc, :, re_sl] = nre
                    st_ref[b, tloc, :, im_sl] = nim
                sre, sim = nre, nim
            carry_ref[kc, :, re_sl] = sre
            carry_ref[kc, :, im_sl] = sim
        if emit_y:
            for m in order:
                st = st_ref[b, m * tc:(m + 1) * tc].reshape(tc * NCHAIN, sw2).astype(BF16)
                y_ref[0, m * tc:(m + 1) * tc, :, kc * ks:(kc + 1) * ks] = (
                    _dot(st, wc_ref[kc]).reshape(tc, NCHAIN, ks))
    sfin_ref[0] = carry_ref[...]


def _s5_pass(s_in, ngroups, wb, wc, are, aim, s0, tt, reverse, emit_y):
    _, ls, _, cs = s_in.shape
    nk, ks, sw2 = wb.shape
    nt = ls // tt
    tmap = (lambda i: nt - 1 - i) if reverse else (lambda i: i)
    state_spec = pl.BlockSpec((1, nk, NCHAIN, sw2), lambda g, i: (g, 0, 0, 0))
    out_specs = [state_spec]
    out_shape = [jax.ShapeDtypeStruct((ngroups, nk, NCHAIN, sw2), F32)]
    scratch = [pltpu.VMEM((2, tt, NCHAIN, sw2), F32)]
    if emit_y:
        out_specs.insert(0, pl.BlockSpec((1, tt, NCHAIN, cs), lambda g, i: (g, tmap(i), 0, 0)))
        out_shape.insert(0, jax.ShapeDtypeStruct((ngroups, ls, NCHAIN, cs), F32))
        scratch.append(pltpu.VMEM((2, tt, NCHAIN, sw2), F32))
    scratch.append(pltpu.VMEM((nk, NCHAIN, sw2), F32))
    res = pl.pallas_call(
        functools.partial(_s5_kernel, reverse=reverse, emit_y=emit_y),
        grid=(ngroups, nt),
        in_specs=[pl.BlockSpec((1, tt, NCHAIN, cs), lambda g, i: (g, tmap(i), 0, 0)),
                  _resident((nk, ks, sw2)), _resident((nk, sw2, ks)),
                  _resident((nk, NCHAIN, sw2 // 2)), _resident((nk, NCHAIN, sw2 // 2)),
                  state_spec],
        out_specs=out_specs,
        out_shape=out_shape,
        scratch_shapes=scratch,
        compiler_params=_cparams(("arbitrary", "arbitrary")),
        name="s5_" + ("bwd" if reverse else "fwd") + ("" if emit_y else "_state"),
    )(s_in, wb, wc, are, aim, s0)
    return (res[0], res[1]) if emit_y else (None, res[0])


def _cmul(are, aim, bre, bim):
    return are * bre - aim * bim, are * bim + aim * bre


def _cpow(re, im, n):
    rre, rim = jnp.ones_like(re), jnp.zeros_like(im)
    while n:
        if n & 1:
            rre, rim = _cmul(rre, rim, re, im)
        re, im = _cmul(re, im, re, im)
        n >>= 1
    return rre, rim


def _s5_params(b_re, b_im, a_re, a_im, log_dt, c_re, c_im):
    ng, ns, nh = b_re.shape
    cs = ng * nh
    ks = min(MXU_DIM, cs)
    gps = ks // nh
    nk = cs // ks
    sw = gps * ns
    a_re, a_im = a_re.astype(F32), a_im.astype(F32)
    dt = jnp.exp(log_dt.astype(F32))[:, None]
    mag = jnp.exp(a_re * dt)
    lre, lim = mag * jnp.cos(a_im * dt), mag * jnp.sin(a_im * dt)
    den = a_re * a_re + a_im * a_im
    zre = ((lre - 1.0) * a_re + lim * a_im) / den
    zim = (lim * a_re - (lre - 1.0) * a_im) / den
    cre, cim = _cmul(c_re.astype(F32), c_im.astype(F32), zre[:, None, :], zim[:, None, :])

    eye = jnp.eye(gps, dtype=F32)

    def in_block(b):
        b = b.astype(F32).reshape(nk, gps, ns, nh)
        return jnp.einsum("kgph,gj->kghjp", b, eye).reshape(nk, ks, sw)

    def out_block(c):
        c = c.reshape(nk, gps, nh, ns)
        return jnp.einsum("kghp,gj->kgpjh", c, eye).reshape(nk, sw, ks)

    wb = jnp.concatenate([in_block(b_re), in_block(b_im)], axis=2).astype(BF16)
    wc = jnp.concatenate([out_block(cre), out_block(-cim)], axis=1).astype(BF16)
    bc = lambda v: jnp.broadcast_to(v.reshape(nk, 1, sw), (nk, NCHAIN, sw))
    return dict(wb=wb, wc=wc, are=bc(lre), aim=bc(lim), lre=lre.reshape(nk, sw), lim=lim.reshape(nk, sw))


def _chain_states(local_end, p, seg, ls, reverse):
    nk, _, sw2 = local_end.shape
    sw = sw2 // 2
    pre, pim = _cpow(p["lre"], p["lim"], ls)
    e = local_end.reshape(nk, NCHAIN // seg, seg, sw2)
    ere, eim = e[..., :sw], e[..., sw:]
    zero = jnp.zeros_like(ere[:, :, 0])
    order = range(seg - 1, -1, -1) if reverse else range(seg)
    sre, sim = zero, zero
    out = [None] * seg
    for k in order:
        out[k] = jnp.concatenate([sre, sim], axis=-1)
        mre, mim = _cmul(pre[:, None, :], pim[:, None, :], sre, sim)
        sre, sim = mre + ere[:, :, k], mim + eim[:, :, k]
    return jnp.stack(out, axis=2).reshape(nk, NCHAIN, sw2)


def _s5_both(s_in, seg, pf, pb, tt):
    ls = s_in.shape[1]
    ys = []
    for p, reverse in ((pf, False), (pb, True)):
        zero = jnp.zeros((1, p["wb"].shape[0], NCHAIN, p["wb"].shape[2]), F32)
        s0 = zero
        if seg > 1:
            _, local_end = _s5_pass(s_in, 1, p["wb"], p["wc"], p["are"], p["aim"], zero, tt, reverse, False)
            s0 = _chain_states(local_end[0], p, seg, ls, reverse)[None]
        y, _ = _s5_pass(s_in, 2, p["wb"], p["wc"], p["are"], p["aim"], jnp.concatenate([s0, zero]), tt,
                        reverse, True)
        ys.append(y)
    return ys


def _glu_kernel(yf_ref, yb_ref, s_ref, d_ref, w_ref, b_ref, o_ref):
    y = yf_ref[...] + yb_ref[...] + d_ref[...] * s_ref[...]
    g = jax.nn.gelu(y)
    o_ref[...] = (g * _sigmoid(_dot(g.astype(BF16), w_ref[...]) + b_ref[...])).astype(o_ref.dtype)


def _glu(yf, yb, s_in, d_skip, glu_w, glu_b, tm):
    rows, cs = yf.shape
    rspec = pl.BlockSpec((tm, cs), lambda i: (i, 0))
    c2 = lambda i: (0, 0)
    return pl.pallas_call(
        _glu_kernel,
        grid=(rows // tm,),
        in_specs=[rspec, rspec, rspec, pl.BlockSpec((1, cs), c2), pl.BlockSpec((cs, cs), c2),
                  pl.BlockSpec((1, cs), c2)],
        out_specs=rspec,
        out_shape=jax.ShapeDtypeStruct((rows, cs), BF16),
        compiler_params=_cparams(("arbitrary",)),
        name="s5_glu",
    )(yf, yb, s_in, d_skip, glu_w, glu_b)


def _pack_halves(v):
    n = v.shape[-1] // 2
    bits = lax.bitcast_convert_type(v.astype(BF16).astype(F32), U32)
    return (bits[..., :n] >> 16) | (bits[..., n:] & jnp.uint32(0xFFFF0000))


def _unpack_halves(p):
    return (lax.bitcast_convert_type(p << 16, F32),
            lax.bitcast_convert_type(p & jnp.uint32(0xFFFF0000), F32))


OUTPROJ_SPLIT = 2


def _outproj_kernel(xp_ref, xs_ref, co_ref, so_ref, g1_ref, sc2_ref, sh2_ref, wo1_ref, wo2_ref, bo_ref,
                    lg_ref, lb_ref, rwh_ref, rwl_ref, x1_ref, hp_ref, lo_ref, *, alpha):
    g = pl.program_id(0)
    tt, d = xp_ref.shape[1], xp_ref.shape[2]
    nj = d // 2 // LANES
    th = tt // OUTPROJ_SPLIT
    for h in range(OUTPROJ_SPLIT):
        ts = slice(h * th, (h + 1) * th)
        rows = slice(h * th * NCHAIN, (h + 1) * th * NCHAIN)
        x = jnp.where(g == 0, xp_ref[:, ts, :], xs_ref[:, ts, :])
        mix = _dot(co_ref[0, rows, :], wo1_ref[...]) + _dot(so_ref[0, rows, :], wo2_ref[...]) + bo_ref[...]
        mix = pltpu.einshape("tcd->ctd", mix.reshape(th, NCHAIN, d))
        x1 = _layer_norm(alpha * x + g1_ref[0] * mix, lg_ref[...], lb_ref[...])
        h2 = x1 * (1.0 + sc2_ref[0]) + sh2_ref[0]
        x1_ref[0, :, ts, :] = x1
        hb = h2.astype(BF16)
        h2f = h2.reshape(NCHAIN * th, d)
        packed = _pack_halves(h2f)
        for j in range(nj):
            hp_ref[0, :, pl.ds(h * th * nj + j, th, stride=nj), :] = (
                packed[:, j * LANES:(j + 1) * LANES].reshape(NCHAIN, th, LANES))
        hi = hb.reshape(NCHAIN * th, d)
        lo = (h2f - hi.astype(F32)).astype(BF16)
        logits = _dot(hi, rwh_ref[...]) + (_dot(hi, rwl_ref[...]) + _dot(lo, rwh_ref[...]))
        lo_ref[0, :, ts, :] = logits.reshape(NCHAIN, th, logits.shape[-1])


def _outproj(xp, xs, conv_out, ssm_out, gate1, scale2, shift2, wo1, wo2, b_out, ln_g, ln_b, rw_hi, rw_lo,
             alpha, tt):
    _, ls, d = xp.shape
    cc, cs = wo1.shape[0], wo2.shape[0]
    ne = rw_hi.shape[1]
    nt = ls // tt
    nj = d // 2 // LANES
    xp_spec, xs_spec = _x_specs(tt, d, nt)
    c2 = lambda g, t: (0, 0)
    nat = lambda w: pl.BlockSpec((1, NCHAIN, tt, w), lambda g, t: (g, 0, t, 0))
    return pl.pallas_call(
        functools.partial(_outproj_kernel, alpha=alpha),
        grid=(2, nt),
        in_specs=[xp_spec, xs_spec,
                  pl.BlockSpec((1, tt * NCHAIN, cc), lambda g, t: (g, t, 0)),
                  pl.BlockSpec((1, tt * NCHAIN, cs), lambda g, t: (g, t, 0)),
                  _mod_spec(d), _mod_spec(d), _mod_spec(d),
                  _resident((cc, d)), _resident((cs, d)), pl.BlockSpec((1, d), c2),
                  pl.BlockSpec((1, d), c2), pl.BlockSpec((1, d), c2),
                  _resident((d, ne)), _resident((d, ne))],
        out_specs=[nat(d),
                   pl.BlockSpec((1, NCHAIN, tt * nj, LANES), lambda g, t: (g, 0, t, 0)),
                   nat(ne)],
        out_shape=[jax.ShapeDtypeStruct((2, NCHAIN, ls, d), F32),
                   jax.ShapeDtypeStruct((2, NCHAIN, ls * nj, LANES), U32),
                   jax.ShapeDtypeStruct((2, NCHAIN, ls, ne), F32)],
        compiler_params=_cparams(("arbitrary", "arbitrary")),
        name="outproj",
    )(xp, xs, conv_out, ssm_out, gate1, scale2, shift2, wo1, wo2, b_out, ln_g, ln_b, rw_hi, rw_lo)


def _route_kernel(lg_ref, bias_ref, eid_ref, e_ref, w_ref, r_ref, cnt_ref, sc_ref, carry_ref, *, span):
    i = pl.program_id(0)
    tm, ne = lg_ref.shape
    gs = ne // N_EXPERT_GROUPS
    neg = jnp.float32(-jnp.inf)

    @pl.when(i == 0)
    def _():
        carry_ref[...] = jnp.zeros_like(carry_ref)

    scores = _sigmoid(lg_ref[...]).T
    biased = scores + bias_ref[...]
    eid = eid_ref[...]

    def first_max(v, ids):
        m = jnp.max(v, axis=0, keepdims=True)
        idx = jnp.min(jnp.where(v == m, ids, float(ne)), axis=0, keepdims=True)
        return m, idx

    gscore = []
    for q in range(N_EXPERT_GROUPS):
        vg, ids = biased[q * gs:(q + 1) * gs], eid[q * gs:(q + 1) * gs]
        m1, i1 = first_max(vg, ids)
        m2 = jnp.max(jnp.where(ids == i1, neg, vg), axis=0, keepdims=True)
        gscore.append(m1 + m2)
    parts = []
    for q in range(N_EXPERT_GROUPS):
        beaten = jnp.zeros((1, tm), F32)
        for o in range(N_EXPERT_GROUPS):
            if o != q:
                wins = (gscore[o] >= gscore[q]) if o < q else (gscore[o] > gscore[q])
                beaten = beaten + wins.astype(F32)
        parts.append(jnp.where(beaten < TOPK_GROUPS, biased[q * gs:(q + 1) * gs], neg))
    masked = jnp.concatenate(parts, axis=0)

    chosen = jnp.zeros((ne, tm), F32)
    hits, e_rows, w_rows = [], [], []
    for k in range(TOP_K):
        _, idx = first_max(masked, eid)
        hit = eid == idx
        hits.append(hit)
        w_rows.append(jnp.sum(jnp.where(hit, scores, 0.0), axis=0, keepdims=True))
        e_rows.append(idx)
        masked = jnp.where(hit, neg, masked)
        chosen = jnp.where(hit, 1.0, chosen)
    w_t = jnp.concatenate(w_rows, axis=0)
    w_ref[...] = w_t / jnp.sum(w_t, axis=0, keepdims=True) * ROUTED_SCALE
    e_ref[...] = jnp.concatenate(e_rows, axis=0).astype(I32)

    rr = lax.broadcasted_iota(I32, (tm, tm), 0)
    cc = lax.broadcasted_iota(I32, (tm, tm), 1)
    before = (rr < cc).astype(BF16)
    prefix = _dot(chosen.astype(BF16), before) + carry_ref[...]
    r_rows = [jnp.sum(jnp.where(hits[k], prefix, 0.0), axis=0, keepdims=True) for k in range(TOP_K)]
    r_ref[...] = jnp.concatenate(r_rows, axis=0).astype(I32)
    carry_ref[...] = carry_ref[...] + jnp.sum(chosen, axis=1, keepdims=True)
    cnt_ref[...] = carry_ref[...]
    span_of = lax.broadcasted_iota(I32, (tm, LANES), 0) // span
    col = lax.broadcasted_iota(I32, (tm, LANES), 1)
    sc_ref[0] = _dot(chosen.astype(BF16), (span_of == col).astype(BF16))


def _route(logits, bias, tm, span):
    t, ne = logits.shape
    nb = tm // span
    assert tm % span == 0 and nb <= LANES
    kspec = pl.BlockSpec((TOP_K, tm), lambda i: (0, i))
    col = pl.BlockSpec((ne, 1), lambda i: (0, 0))
    eid = jnp.broadcast_to(jnp.arange(ne, dtype=F32)[:, None], (ne, tm))
    e_idx, e_w, rank, counts, span_cnt = pl.pallas_call(
        functools.partial(_route_kernel, span=span),
        grid=(t // tm,),
        in_specs=[pl.BlockSpec((tm, ne), lambda i: (i, 0)), col, _resident((ne, tm))],
        out_specs=[kspec, kspec, kspec, col, pl.BlockSpec((1, ne, LANES), lambda i: (i, 0, 0))],
        out_shape=[jax.ShapeDtypeStruct((TOP_K, t), I32), jax.ShapeDtypeStruct((TOP_K, t), F32),
                   jax.ShapeDtypeStruct((TOP_K, t), I32), jax.ShapeDtypeStruct((ne, 1), F32),
                   jax.ShapeDtypeStruct((t // tm, ne, LANES), F32)],
        scratch_shapes=[pltpu.VMEM((ne, 1), F32)],
        compiler_params=_cparams(("arbitrary",)),
        name="route",
    )(logits, bias.reshape(ne, 1), eid)
    span_cnt = span_cnt[:, :, :nb].transpose(0, 2, 1).reshape(t // span, ne).astype(I32)
    return e_idx, e_w, rank, counts, span_cnt


def _dest_kernel(e_ref, r_ref, ps_ref, eid_ref, o_ref):
    nb = ps_ref.shape[0]
    span = e_ref.shape[1] // nb
    for b in range(nb):
        lanes = slice(b * span, (b + 1) * span)
        e = e_ref[:, lanes].astype(F32)
        starts = [jnp.sum(jnp.where(eid_ref[:, lanes] == e[k:k + 1], ps_ref[b], 0.0), axis=0, keepdims=True)
                  for k in range(TOP_K)]
        o_ref[:, lanes] = r_ref[:, lanes] + jnp.concatenate(starts, axis=0).astype(I32)


def _dest(e_idx, rank, table, tm, span=None):
    t = e_idx.shape[1]
    ntab, ne, _ = table.shape
    nb = 1 if span is None else tm // span
    assert ntab == (1 if span is None else t // span)
    kspec = pl.BlockSpec((TOP_K, tm), lambda i: (0, i))
    tspec = pl.BlockSpec((nb, ne, 1), (lambda i: (0, 0, 0)) if span is None else (lambda i: (i, 0, 0)))
    eid = jnp.broadcast_to(jnp.arange(ne, dtype=F32)[:, None], (ne, tm))
    return pl.pallas_call(
        _dest_kernel,
        grid=(t // tm,),
        in_specs=[kspec, kspec, tspec, _resident((ne, tm))],
        out_specs=kspec,
        out_shape=jax.ShapeDtypeStruct((TOP_K, t), I32),
        compiler_params=_cparams(("arbitrary",)),
        name="dest" if span is None else "slot",
    )(e_idx, rank, table, eid)


def _dispatch_kernel(last_ref, nu_ref, dest_ref, h_ref, sg_ref, su_ref, sd_ref, xs_ref, sh_ref, zero_ref, sem, zsem,
                     *, nj, tile_rows):
    i = pl.program_id(0)
    tm = h_ref.shape[0] // nj
    ne = last_ref.shape[0]

    @pl.when(i == 0)
    def _():
        zero_ref[...] = jnp.zeros_like(zero_ref)

        def fill(row):
            start = pl.multiple_of(row * nj, 8)
            return pltpu.make_async_copy(zero_ref, xs_ref.at[pl.ds(start, tile_rows * nj)], zsem)

        def issue(e, c):
            @pl.when(last_ref[e] >= 0)
            def _():
                fill(last_ref[e]).start()
            return c

        def drain(e, c):
            @pl.when(last_ref[e] >= 0)
            def _():
                fill(last_ref[e]).wait()
            return c

        lax.fori_loop(0, ne, issue, 0)
        lax.fori_loop(0, ne, drain, 0)
        n_tiles = xs_ref.shape[0] // (tile_rows * nj)
        lax.fori_loop(nu_ref[0], n_tiles, lambda q, c: (fill(q * tile_rows).start(), c)[1], 0)
        lax.fori_loop(nu_ref[0], n_tiles, lambda q, c: (fill(q * tile_rows).wait(), c)[1], 0)

    def issue_rows(r, c):
        src = h_ref.at[pl.ds(pl.multiple_of(r * nj, nj), nj)]
        for k in range(TOP_K):
            dst = xs_ref.at[pl.ds(pl.multiple_of(dest_ref[r * TOP_K + k] * nj, nj), nj)]
            pltpu.make_async_copy(src, dst, sem).start()
        return c

    lax.fori_loop(0, tm, issue_rows, 0)

    x = _unpack_rows(h_ref, tm, nj)
    gate = _dot(x, sg_ref[...])
    up = _dot(x, su_ref[...])
    sh_ref[...] = _dot((gate * _sigmoid(gate) * up).astype(BF16), sd_ref[...]).astype(sh_ref.dtype)

    for _ in range(TOP_K):
        pltpu.make_async_copy(h_ref, xs_ref.at[pl.ds(0, tm * nj)], sem).wait()


def _dispatch(last_tile_row, n_used, dest_flat, h_packed, sh_gate, sh_up, sh_down, n_rows, nj, tile_rows, tm):
    t = h_packed.shape[0] // nj
    d, ff = sh_gate.shape
    grid_spec = pltpu.PrefetchScalarGridSpec(
        num_scalar_prefetch=2,
        grid=(t // tm,),
        in_specs=[pl.BlockSpec((tm * TOP_K,), lambda i, last, nu: (i,), memory_space=pltpu.SMEM),
                  pl.BlockSpec((tm * nj, LANES), lambda i, last, nu: (i, 0)),
                  _resident((d, ff)), _resident((d, ff)), _resident((ff, d))],
        out_specs=[pl.BlockSpec(memory_space=pl.ANY), pl.BlockSpec((tm, d), lambda i, last, nu: (i, 0))],
        scratch_shapes=[pltpu.VMEM((tile_rows * nj, LANES), U32), pltpu.SemaphoreType.DMA,
                        pltpu.SemaphoreType.DMA],
    )
    return pl.pallas_call(
        functools.partial(_dispatch_kernel, nj=nj, tile_rows=tile_rows),
        grid_spec=grid_spec,
        out_shape=[jax.ShapeDtypeStruct((n_rows * nj, LANES), U32), jax.ShapeDtypeStruct((t, d), BF16)],
        compiler_params=_cparams(("arbitrary",)),
        name="dispatch",
    )(last_tile_row, n_used, dest_flat, h_packed, sh_gate, sh_up, sh_down)


def _unpack_rows(p_ref, tm, nj):
    lo, hi = [], []
    for j in range(nj):
        l, h = _unpack_halves(p_ref[pl.ds(j, tm, stride=nj), :])
        lo.append(l.astype(BF16))
        hi.append(h.astype(BF16))
    return jnp.concatenate(lo + hi, axis=-1)


CAST_ROWS = 128


def _experts_kernel(te_ref, nu_ref, nx_ref, x_ref, wg_hbm, wu_hbm, wd_hbm, y_ref,
                    wgf_ref, wuf_ref, wdf_ref, wgb_ref, wub_ref, wdb_ref, sem, *, nj):
    i = pl.program_id(0)
    tm = x_ref.shape[0] // nj
    live = i < nu_ref[0]
    e = te_ref[i]
    first = live & ((i == 0) | (e != te_ref[jnp.maximum(i - 1, 0)]))

    def fetch(expert):
        return (pltpu.make_async_copy(wg_hbm.at[expert], wgf_ref, sem.at[0]),
                pltpu.make_async_copy(wu_hbm.at[expert], wuf_ref, sem.at[1]),
                pltpu.make_async_copy(wd_hbm.at[expert], wdf_ref, sem.at[2]))

    @pl.when(live & (i == 0))
    def _():
        for c in fetch(e):
            c.start()

    @pl.when(first)
    def _():
        for c in fetch(e):
            c.wait()
        for src, dst in ((wgf_ref, wgb_ref), (wuf_ref, wub_ref), (wdf_ref, wdb_ref)):
            for r in range(0, src.shape[0], CAST_ROWS):
                dst[r:r + CAST_ROWS] = src[r:r + CAST_ROWS].astype(BF16)

        @pl.when(nx_ref[i] >= 0)
        def _():
            for c in fetch(nx_ref[i]):
                c.start()

    @pl.when(live)
    def _():
        x = _unpack_rows(x_ref, tm, nj)
        gate = _dot(x, wgb_ref[...])
        up = _dot(x, wub_ref[...])
        act = (gate * _sigmoid(gate) * up).astype(BF16)
        y = _dot(act, wdb_ref[...])
        packed = _pack_halves(y)
        for j in range(nj):
            y_ref[pl.ds(j, tm, stride=nj), :] = packed[:, j * LANES:(j + 1) * LANES]

    @pl.when(i >= nu_ref[0])
    def _():
        y_ref[...] = jnp.zeros_like(y_ref)


def _experts(tile_e, n_used, next_e, x_sorted, w_gate, w_up, w_down, nj, tm):
    n_rows = x_sorted.shape[0] // nj
    ne, d, ff = w_gate.shape
    n_tiles = n_rows // tm
    hbm = pl.BlockSpec(memory_space=pl.ANY)
    grid_spec = pltpu.PrefetchScalarGridSpec(
        num_scalar_prefetch=3,
        grid=(n_tiles,),
        in_specs=[pl.BlockSpec((tm * nj, LANES), lambda i, te, nu, nx: (jnp.minimum(i, nu[0] - 1), 0)),
                  hbm, hbm, hbm],
        out_specs=pl.BlockSpec((tm * nj, LANES), lambda i, te, nu, nx: (i, 0)),
        scratch_shapes=[pltpu.VMEM((d, ff), F32), pltpu.VMEM((d, ff), F32), pltpu.VMEM((ff, d), F32),
                        pltpu.VMEM((d, ff), BF16), pltpu.VMEM((d, ff), BF16), pltpu.VMEM((ff, d), BF16),
                        pltpu.SemaphoreType.DMA((3,))],
    )
    return pl.pallas_call(
        functools.partial(_experts_kernel, nj=nj),
        grid_spec=grid_spec,
        out_shape=jax.ShapeDtypeStruct((n_rows * nj, LANES), U32),
        compiler_params=_cparams(("arbitrary",)),
        name="experts",
    )(tile_e, n_used, next_e, x_sorted, w_gate, w_up, w_down)


COMBINE_CHUNK = 8
COMBINE_UNROLL = 4
COMBINE_ISSUE = 4


def _combine_kernel(nck_ref, src_ref, srcn_ref, pos_ref, w_ref,
                    x1_ref, sh_ref, g2_ref, lg_ref, lb_ref, ys_ref, op_ref, os_ref,
                    buf_ref, alo_ref, ahi_ref, sem, *, alpha, half, nj):
    i = pl.program_id(0)
    n = pl.num_programs(0)
    tm, d = sh_ref.shape
    max_chunks = src_ref.shape[0]
    slot = i % 2
    chunk_rows = COMBINE_CHUNK * nj

    def fetch(s_ref, count, s):
        def start(g, c):
            for u in range(COMBINE_ISSUE):
                q = g * COMBINE_ISSUE + u
                src = ys_ref.at[pl.ds(pl.multiple_of(s_ref[q] * nj, nj), chunk_rows)]
                dst = buf_ref.at[s, pl.ds(pl.multiple_of(q * chunk_rows, chunk_rows), chunk_rows)]
                pltpu.make_async_copy(src, dst, sem.at[s]).start()
            return c
        lax.fori_loop(0, count // COMBINE_ISSUE, start, 0)

    rounded = lambda c: (c + COMBINE_ISSUE - 1) // COMBINE_ISSUE * COMBINE_ISSUE

    @pl.when(i == 0)
    def _():
        fetch(src_ref, rounded(nck_ref[0]), 0)

    @pl.when(i + 1 < n)
    def _():
        fetch(srcn_ref, rounded(nck_ref[i + 1]), 1 - slot)

    count = rounded(nck_ref[i])
    bit = COMBINE_ISSUE
    while bit <= max_chunks:
        @pl.when((count & bit) != 0)
        def _(rows=bit * chunk_rows):
            pltpu.make_async_copy(ys_ref.at[pl.ds(0, rows)], buf_ref.at[slot, pl.ds(0, rows)],
                                  sem.at[slot]).wait()
        bit *= 2

    def reduce_tokens(tt, c):
        for u in range(COMBINE_UNROLL):
            t = tt * COMBINE_UNROLL + u
            lo = hi = None
            for k in range(TOP_K):
                p = pos_ref[t * TOP_K + k]
                l, h = _unpack_halves(buf_ref[slot, pl.ds(pl.multiple_of(p * nj, nj), nj), :])
                wk = w_ref[t * TOP_K + k]
                lo = l * wk if k == 0 else lo + l * wk
                hi = h * wk if k == 0 else hi + h * wk
            alo_ref[pl.ds(pl.multiple_of(t * nj, nj), nj), :] = lo
            ahi_ref[pl.ds(pl.multiple_of(t * nj, nj), nj), :] = hi
        return c

    lax.fori_loop(0, tm // COMBINE_UNROLL, reduce_tokens, 0)
    routed = jnp.concatenate([alo_ref[pl.ds(j, tm, stride=nj), :] for j in range(nj)]
                             + [ahi_ref[pl.ds(j, tm, stride=nj), :] for j in range(nj)], axis=-1)
    ffn = routed + sh_ref[...].astype(F32)
    out = _layer_norm(alpha * x1_ref[...] + g2_ref[0] * ffn, lg_ref[...], lb_ref[...])

    @pl.when(i < half)
    def _():
        op_ref[...] = out

    @pl.when(i >= half)
    def _():
        os_ref[...] = out


def _combine(n_chunks, chunk_src, pos_flat, w_flat, x1, shared, gate2_rows, y_sorted, ln_g, ln_b, alpha, nj, tm, ls):
    t, d = shared.shape
    n = t // tm
    max_chunks = chunk_src.shape[0] // n
    half = n // 2
    per_chain = ls // tm
    assert tm % COMBINE_UNROLL == 0
    c2 = lambda i, nck: (0, 0)
    row = lambda i, nck: (i, 0)
    smem = lambda size, imap: pl.BlockSpec((size,), imap, memory_space=pltpu.SMEM)
    grid_spec = pltpu.PrefetchScalarGridSpec(
        num_scalar_prefetch=1,
        grid=(n,),
        in_specs=[smem(max_chunks, lambda i, nck: (i,)),
                  smem(max_chunks, lambda i, nck: (jnp.minimum(i + 1, n - 1),)),
                  smem(tm * TOP_K, lambda i, nck: (i,)), smem(tm * TOP_K, lambda i, nck: (i,)),
                  pl.BlockSpec((tm, d), row), pl.BlockSpec((tm, d), row),
                  pl.BlockSpec((1, 1, d), lambda i, nck: (i // per_chain, 0, 0)),
                  pl.BlockSpec((1, d), c2), pl.BlockSpec((1, d), c2),
                  pl.BlockSpec(memory_space=pl.ANY)],
        out_specs=[pl.BlockSpec((tm, d), lambda i, nck: (jnp.minimum(i, half - 1), 0)),
                   pl.BlockSpec((tm, d), lambda i, nck: (jnp.maximum(i - half, 0), 0))],
        scratch_shapes=[pltpu.VMEM((2, max_chunks * COMBINE_CHUNK * nj, LANES), U32),
                        pltpu.VMEM((tm * nj, LANES), F32), pltpu.VMEM((tm * nj, LANES), F32),
                        pltpu.SemaphoreType.DMA((2,))],
    )
    return pl.pallas_call(
        functools.partial(_combine_kernel, alpha=alpha, half=half, nj=nj),
        grid_spec=grid_spec,
        out_shape=[jax.ShapeDtypeStruct((t // 2, d), F32), jax.ShapeDtypeStruct((t // 2, d), F32)],
        compiler_params=_cparams(("arbitrary",)),
        name="combine",
    )(n_chunks, chunk_src, chunk_src, pos_flat, w_flat, x1, shared, gate2_rows, ln_g, ln_b, y_sorted)


def _tile(pref, n, mult=8):
    t = min(pref, n)
    while n % t or t % mult:
        t -= 1
    return t


def _encoder_layer(xp, xs, c_all, chain_seq, seg, alpha, p):
    _, ls, d = xp.shape
    cc = p["conv_w"].shape[-1]

    ada = _ada(c_all, p["w_ada"], p["b_ada"])
    mods = ada.reshape(ada.shape[0], N_ADA, d)[chain_seq].reshape(2, NCHAIN, N_ADA, 1, d)
    shift1, scale1, gate1, shift2, scale2, gate2 = (mods[:, :, k] for k in range(N_ADA))

    w_in = p["w_in"].astype(BF16)
    b_in = p["b_in"].reshape(1, -1)
    u, s_in = _inproj(xp, xs, scale1, shift1, w_in[:, :cc], w_in[:, cc:2 * cc], w_in[:, 2 * cc:],
                      b_in[:, :cc], b_in[:, cc:2 * cc], b_in[:, 2 * cc:], _tile(TILES["inproj"], ls))

    chain = np.arange(NCHAIN)
    prev_ok = jnp.asarray(np.stack([(chain % seg != 0), np.zeros(NCHAIN, bool)]).astype(np.float32)[..., None])
    next_ok = jnp.asarray(np.stack([(chain % seg != seg - 1), np.zeros(NCHAIN, bool)]).astype(np.float32)[..., None])
    conv_out = _conv(u, prev_ok, next_ok, p["conv_w"], p["conv_b"], p["conv_ln_g"], p["conv_ln_b"],
                     _tile(TILES["conv"], ls, CONV_HALO))

    pf = _s5_params(p["ssm_b_re"], p["ssm_b_im"], p["ssm_a_re_f"], p["ssm_a_im_f"], p["ssm_log_dt_f"],
                    p["ssm_c_re_f"], p["ssm_c_im_f"])
    pb = _s5_params(p["ssm_b_re"], p["ssm_b_im"], p["ssm_a_re_b"], p["ssm_a_im_b"], p["ssm_log_dt_b"],
                    p["ssm_c_re_b"], p["ssm_c_im_b"])
    yf, yb = _s5_both(s_in, seg, pf, pb, _tile(TILES["s5"], ls))
    cs = s_in.shape[-1]
    rows_tm = 2 * ls * NCHAIN
    flat = lambda a: a.reshape(rows_tm, cs)
    ssm_out = _glu(flat(yf), flat(yb), flat(s_in), p["ssm_d"].reshape(1, cs),
                   p["ssm_glu_w"].astype(BF16), p["ssm_glu_b"].reshape(1, cs), _tile(TILES["glu"], rows_tm))
    ssm_out = ssm_out.reshape(2, ls * NCHAIN, cs)

    w_out = p["w_out"].astype(BF16)
    rw = p["router_w"].astype(F32)
    rw_hi = rw.astype(BF16)
    rw_lo = (rw - rw_hi.astype(F32)).astype(BF16)
    row = lambda v: v.reshape(1, -1)
    x1, hp, logits = _outproj(xp, xs, conv_out, ssm_out, gate1, scale2, shift2, w_out[:cc], w_out[cc:],
                              row(p["b_out"]), row(p["ln1_g"]), row(p["ln1_b"]), rw_hi, rw_lo,
                              alpha, _tile(TILES["outproj"], ls, 16))

    t = 2 * NCHAIN * ls
    ne = rw.shape[1]
    nj = d // 2 // LANES
    tm_c = _tile(TILES["combine"], ls, COMBINE_UNROLL)
    tm_r = _tile(TILES["route"], t, LANES)
    e_idx, e_w, rank, counts, tile_cnt = _route(logits.reshape(t, ne), row(p["router_bias"]).astype(F32),
                                                tm_r, tm_c)

    tm_e = _tile(TILES["expert"], t)
    n_rows = t * TOP_K + ne * tm_e
    counts = counts.reshape(ne).astype(I32)
    padded = (counts + tm_e - 1) // tm_e * tm_e
    pad_end = jnp.cumsum(padded)
    pad_start = pad_end - padded
    n_tiles = n_rows // tm_e
    tile_start = jnp.arange(n_tiles, dtype=I32)[:, None] * tm_e
    tile_e = jnp.minimum(jnp.sum((pad_end[None, :] <= tile_start).astype(I32), axis=1), ne - 1)
    n_used = (pad_end[-1:] // tm_e).astype(I32)
    last_tile_row = jnp.where(counts > 0, pad_end - tm_e, -1).astype(I32)
    following = pad_end[tile_e] // tm_e
    next_e = jnp.where(following < n_used[0], tile_e[jnp.minimum(following, n_tiles - 1)], -1).astype(I32)
    dest = _dest(e_idx, rank, pad_start.astype(F32).reshape(1, ne, 1), _tile(TILES["dest"], t, LANES))
    dest = dest.T.reshape(t * TOP_K)

    before = jnp.cumsum(tile_cnt, axis=0) - tile_cnt
    run_src = pad_start[None, :] + before
    run_chunks = (tile_cnt + COMBINE_CHUNK - 1) // COMBINE_CHUNK
    chunks_through = jnp.cumsum(run_chunks, axis=1)
    chunks_before = chunks_through - run_chunks
    slot = _dest(e_idx, rank, (chunks_before * COMBINE_CHUNK - before).astype(F32)[:, :, None], tm_r, tm_c)
    slot = slot.T.reshape(t * TOP_K)
    max_chunks = ne + tm_c * TOP_K // COMBINE_CHUNK
    q = jnp.arange(max_chunks, dtype=I32)
    owner = jnp.sum((chunks_through[:, None, :] <= q[None, :, None]).astype(I32), axis=2)
    run_base = run_src - chunks_before * COMBINE_CHUNK
    chunk_src = jnp.sum(jnp.where(jnp.arange(ne, dtype=I32)[None, None, :] == owner[:, :, None],
                                  run_base[:, None, :], 0), axis=2) + q[None, :] * COMBINE_CHUNK

    x_sorted, shared = _dispatch(last_tile_row, n_used, dest, hp.reshape(t * nj, LANES),
                                 p["sh_w_gate"].astype(BF16), p["sh_w_up"].astype(BF16),
                                 p["sh_w_down"].astype(BF16), n_rows, nj, tm_e, _tile(TILES["dispatch"], ls))
    y_sorted = _experts(tile_e, n_used, next_e, x_sorted, p["exp_w_gate"], p["exp_w_up"], p["exp_w_down"],
                        nj, tm_e)
    gate2_rows = gate2.reshape(2 * NCHAIN, 1, d)
    yp, ys = _combine(chunks_through[:, -1].astype(I32), chunk_src.astype(I32).reshape(-1), slot,
                      e_w.T.reshape(t * TOP_K), x1.reshape(t, d), shared, gate2_rows, y_sorted,
                      row(p["ln2_g"]), row(p["ln2_b"]), alpha, nj, tm_c, ls)
    return yp.reshape(NCHAIN, ls, d), ys.reshape(NCHAIN, ls, d)


_PARAM_NAMES = ("w_ada", "b_ada", "w_in", "b_in", "conv_w", "conv_b", "conv_ln_g", "conv_ln_b",
                "ssm_b_re", "ssm_b_im", "ssm_a_re_f", "ssm_a_im_f", "ssm_log_dt_f",
                "ssm_a_re_b", "ssm_a_im_b", "ssm_log_dt_b", "ssm_c_re_f", "ssm_c_im_f",
                "ssm_c_re_b", "ssm_c_im_b", "ssm_d", "ssm_glu_w", "ssm_glu_b", "w_out", "b_out",
                "ln1_g", "ln1_b", "router_w", "router_bias", "exp_w_gate", "exp_w_up", "exp_w_down",
                "sh_w_gate", "sh_w_up", "sh_w_down", "ln2_g", "ln2_b")


def kernel(x_prompt, x_sample, c_prompt, c_sample, w_ada, b_ada, w_in, b_in, conv_w, conv_b, conv_ln_g, conv_ln_b, ssm_b_re, ssm_b_im, ssm_a_re_f, ssm_a_im_f, ssm_log_dt_f, ssm_a_re_b, ssm_a_im_b, ssm_log_dt_b, ssm_c_re_f, ssm_c_im_f, ssm_c_re_b, ssm_c_im_b, ssm_d, ssm_glu_w, ssm_glu_b, w_out, b_out, ln1_g, ln1_b, router_w, router_bias, exp_w_gate, exp_w_up, exp_w_down, sh_w_gate, sh_w_up, sh_w_down, ln2_g, ln2_b):
    stacked = (w_ada, b_ada, w_in, b_in, conv_w, conv_b, conv_ln_g, conv_ln_b, ssm_b_re, ssm_b_im,
               ssm_a_re_f, ssm_a_im_f, ssm_log_dt_f, ssm_a_re_b, ssm_a_im_b, ssm_log_dt_b, ssm_c_re_f,
               ssm_c_im_f, ssm_c_re_b, ssm_c_im_b, ssm_d, ssm_glu_w, ssm_glu_b, w_out, b_out, ln1_g, ln1_b,
               router_w, router_bias, exp_w_gate, exp_w_up, exp_w_down, sh_w_gate, sh_w_up, sh_w_down,
               ln2_g, ln2_b)
    depth = w_ada.shape[0]
    alpha = (2 * depth) ** 0.25
    bp, lp, d = x_prompt.shape
    bs, lsample, _ = x_sample.shape
    assert NCHAIN % bp == 0 and bs == NCHAIN and lp % (NCHAIN // bp) == 0
    seg = NCHAIN // bp
    ls = lp // seg
    assert ls == lsample

    c_all = jnp.concatenate([c_prompt, c_sample], axis=0)
    c_all = jnp.pad(c_all, ((0, -c_all.shape[0] % 8), (0, 0)))
    chain_seq = np.concatenate([np.arange(NCHAIN) // seg, bp + np.arange(NCHAIN)])

    xp = x_prompt.reshape(NCHAIN, ls, d)
    xs = x_sample.reshape(NCHAIN, ls, d)
    for l in range(depth):
        params = {n: v[l] for n, v in zip(_PARAM_NAMES, stacked)}
        xp, xs = _encoder_layer(xp, xs, c_all, chain_seq, seg, alpha, params)
    return xp.reshape(bp, lp, d), xs.reshape(bs, lsample, d)
```

```python
import functools
import math

import numpy as np
import jax
import jax.numpy as jnp
from jax import lax
from jax.experimental import pallas as pl
from jax.experimental.pallas import tpu as pltpu

F32 = jnp.float32
BF16 = jnp.bfloat16
U32 = jnp.uint32
I32 = jnp.int32

LN_EPS = 1e-5
N_ADA = 6
TOP_K = 8
N_EXPERT_GROUPS = 8
TOPK_GROUPS = 4
ROUTED_SCALE = 2.5

NCHAIN = 8
LANES = 128
MXU_DIM = 256
VMEM_LIMIT = 56 * 1024 * 1024

TILES = dict(inproj=64, conv=128, s5=128, glu=1024, outproj=32, route=512, dest=1024, expert=512,
             dispatch=512, combine=256)


def _cparams(sem):
    return pltpu.CompilerParams(dimension_semantics=sem, vmem_limit_bytes=VMEM_LIMIT)


def _resident(shape):
    zeros = (0,) * len(shape)
    return pl.BlockSpec(shape, lambda *_: zeros, pipeline_mode=pl.Buffered(1))


def _dot(a, b):
    return jnp.dot(a, b, preferred_element_type=F32)


def _sigmoid(x):
    return jax.nn.sigmoid(x)


def _layer_norm(v, g, b):
    mu = jnp.mean(v, axis=-1, keepdims=True)
    d = v - mu
    var = jnp.mean(d * d, axis=-1, keepdims=True)
    return d * lax.rsqrt(var + LN_EPS) * g + b


def _ada_kernel(c_ref, w_ref, b_ref, o_ref):
    c = c_ref[...]
    s = (c * _sigmoid(c)).astype(BF16)
    o_ref[...] = _dot(s, w_ref[...].astype(BF16)) + b_ref[...]


def _ada(c_all, w_ada, b_ada):
    rows, d = c_all.shape
    n = w_ada.shape[1]
    tn = _tile(2048, n, LANES)
    return pl.pallas_call(
        _ada_kernel,
        grid=(n // tn,),
        in_specs=[pl.BlockSpec((rows, d), lambda j: (0, 0)),
                  pl.BlockSpec((d, tn), lambda j: (0, j)),
                  pl.BlockSpec((1, tn), lambda j: (0, j))],
        out_specs=pl.BlockSpec((rows, tn), lambda j: (0, j)),
        out_shape=jax.ShapeDtypeStruct((rows, n), F32),
        compiler_params=_cparams(("arbitrary",)),
        name="ada",
    )(c_all, w_ada, b_ada.reshape(1, n))


def _x_specs(tt, d, nt):
    xp = pl.BlockSpec((NCHAIN, tt, d), lambda g, t: (0, jnp.where(g == 0, t, nt - 1), 0))
    xs = pl.BlockSpec((NCHAIN, tt, d), lambda g, t: (0, jnp.where(g == 1, t, 0), 0))
    return xp, xs


def _mod_spec(d):
    return pl.BlockSpec((1, NCHAIN, 1, d), lambda g, t: (g, 0, 0, 0))


def _inproj_kernel(xp_ref, xs_ref, sc_ref, sh_ref, wa_ref, wg_ref, ws_ref, ba_ref, bg_ref, bs_ref,
                   u_ref, s_ref):
    g = pl.program_id(0)
    x = jnp.where(g == 0, xp_ref[...], xs_ref[...])
    h = x * (1.0 + sc_ref[0]) + sh_ref[0]
    tt = h.shape[1]
    ht = pltpu.einshape("ctd->tcd", h).reshape(tt * NCHAIN, h.shape[2]).astype(BF16)
    a = _dot(ht, wa_ref[...]) + ba_ref[...]
    gt = _dot(ht, wg_ref[...]) + bg_ref[...]
    u = a * _sigmoid(gt)
    s = _dot(ht, ws_ref[...]) + bs_ref[...]
    u_ref[0] = u.reshape(tt, NCHAIN, u.shape[-1])
    s_ref[0] = s.reshape(tt, NCHAIN, s.shape[-1])


def _inproj(xp, xs, scale1, shift1, wa, wg, ws, ba, bg, bs, tt):
    _, ls, d = xp.shape
    cc, cs = wa.shape[1], ws.shape[1]
    nt = ls // tt
    xp_spec, xs_spec = _x_specs(tt, d, nt)
    const2 = lambda g, t: (0, 0)
    return pl.pallas_call(
        _inproj_kernel,
        grid=(2, nt),
        in_specs=[xp_spec, xs_spec, _mod_spec(d), _mod_spec(d),
                  _resident((d, cc)), _resident((d, cc)), _resident((d, cs)),
                  pl.BlockSpec((1, cc), const2), pl.BlockSpec((1, cc), const2), pl.BlockSpec((1, cs), const2)],
        out_specs=[pl.BlockSpec((1, tt, NCHAIN, cc), lambda g, t: (g, t, 0, 0)),
                   pl.BlockSpec((1, tt, NCHAIN, cs), lambda g, t: (g, t, 0, 0))],
        out_shape=[jax.ShapeDtypeStruct((2, ls, NCHAIN, cc), F32),
                   jax.ShapeDtypeStruct((2, ls, NCHAIN, cs), F32)],
        compiler_params=_cparams(("arbitrary", "arbitrary")),
        name="inproj",
    )(xp, xs, scale1, shift1, wa, wg, ws, ba, bg, bs)


CONV_HALO = 16
CONV_BLOCK = 8


def _conv_kernel(u_ref, up_ref, un_ref, pok_ref, nok_ref, w_ref, cb_ref, g_ref, b_ref, o_ref,
                 buf_ref, acc_ref, *, width):
    t = pl.program_id(1)
    nt = pl.num_programs(1)
    tt = u_ref.shape[1]
    pad = (width - 1) // 2
    prev = up_ref[0]
    prev_wrapped = pltpu.roll(prev, shift=1, axis=1) * pok_ref[0]
    buf_ref[0:CONV_HALO] = jnp.where(t == 0, prev_wrapped, prev)
    buf_ref[CONV_HALO:CONV_HALO + tt] = u_ref[0]
    nxt = un_ref[0]
    next_wrapped = pltpu.roll(nxt, shift=NCHAIN - 1, axis=1) * nok_ref[0]
    buf_ref[CONV_HALO + tt:2 * CONV_HALO + tt] = jnp.where(t == nt - 1, next_wrapped, nxt)

    base = CONV_HALO - pad

    for c0 in range(0, u_ref.shape[3], LANES):
        lanes = slice(c0, c0 + LANES)
        taps = [w_ref[k, :, lanes] for k in range(width)]
        bias = cb_ref[:, lanes]

        def body(b, carry, lanes=lanes, taps=taps, bias=bias):
            t0 = b * CONV_BLOCK
            acc = [bias] * CONV_BLOCK
            for s in range(CONV_BLOCK + width - 1):
                x = buf_ref[t0 + base + s, :, lanes]
                for i in range(CONV_BLOCK):
                    if 0 <= s - i < width:
                        acc[i] = acc[i] + x * taps[s - i]
            for i in range(CONV_BLOCK):
                acc_ref[t0 + i, :, lanes] = acc[i]
            return carry

        lax.fori_loop(0, tt // CONV_BLOCK, body, 0)
    v = _layer_norm(acc_ref[...], g_ref[...], b_ref[...])
    v = v * _sigmoid(v)
    o_ref[0] = v.reshape(tt * NCHAIN, v.shape[-1]).astype(o_ref.dtype)


def _conv(u, prev_ok, next_ok, conv_w, conv_b, ln_g, ln_b, tt):
    _, ls, _, cc = u.shape
    width = conv_w.shape[0]
    assert (width - 1) // 2 <= CONV_HALO and tt % CONV_HALO == 0 and cc % LANES == 0
    nt = ls // tt
    hb = tt // CONV_HALO
    nh = ls // CONV_HALO
    wb = jnp.broadcast_to(conv_w[:, None, :], (width, NCHAIN, cc))
    row = lambda v: jnp.broadcast_to(v[None, :], (NCHAIN, cc))
    const2 = lambda g, t: (0, 0)
    return pl.pallas_call(
        functools.partial(_conv_kernel, width=width),
        grid=(2, nt),
        in_specs=[pl.BlockSpec((1, tt, NCHAIN, cc), lambda g, t: (g, t, 0, 0)),
                  pl.BlockSpec((1, CONV_HALO, NCHAIN, cc),
                               lambda g, t: (g, jnp.where(t == 0, nh - 1, t * hb - 1), 0, 0)),
                  pl.BlockSpec((1, CONV_HALO, NCHAIN, cc),
                               lambda g, t: (g, jnp.where(t == nt - 1, 0, (t + 1) * hb), 0, 0)),
                  pl.BlockSpec((1, NCHAIN, 1), lambda g, t: (g, 0, 0)),
                  pl.BlockSpec((1, NCHAIN, 1), lambda g, t: (g, 0, 0)),
                  pl.BlockSpec((width, NCHAIN, cc), lambda g, t: (0, 0, 0)),
                  pl.BlockSpec((NCHAIN, cc), const2), pl.BlockSpec((NCHAIN, cc), const2),
                  pl.BlockSpec((NCHAIN, cc), const2)],
        out_specs=pl.BlockSpec((1, tt * NCHAIN, cc), lambda g, t: (g, t, 0)),
        out_shape=jax.ShapeDtypeStruct((2, ls * NCHAIN, cc), BF16),
        scratch_shapes=[pltpu.VMEM((tt + 2 * CONV_HALO, NCHAIN, cc), F32),
                        pltpu.VMEM((tt, NCHAIN, cc), F32)],
        compiler_params=_cparams(("arbitrary", "arbitrary")),
        name="conv",
    )(u, u, u, prev_ok, next_ok, wb, row(conv_b), row(ln_g), row(ln_b))


S5_LANE_BLOCK = 512
S5_ROW_CHUNKS = {False: 2, True: 4}


def _s5_kernel(u_ref, wb_ref, wc_ref, are_ref, aim_ref, s0_ref, *rest, reverse, emit_y):
    if emit_y:
        y_ref, sfin_ref, bu_ref, st_ref, carry_ref = rest
    else:
        sfin_ref, bu_ref, carry_ref = rest
        st_ref = None
    i = pl.program_id(1)
    tt = u_ref.shape[1]
    nk, ks, sw2 = wb_ref.shape
    sw = sw2 // 2

    @pl.when(i == 0)
    def _():
        carry_ref[...] = s0_ref[0]

    u2 = u_ref[0].reshape(tt * NCHAIN, u_ref.shape[3]).astype(BF16)
    chunks = S5_ROW_CHUNKS[reverse]
    tc = tt // chunks
    order = range(chunks - 1, -1, -1) if reverse else range(chunks)
    for kc in range(nk):
        b = kc % 2
        for m in order:
            rows = slice(m * tc * NCHAIN, (m + 1) * tc * NCHAIN)
            bu_ref[b, m * tc:(m + 1) * tc] = _dot(u2[rows, kc * ks:(kc + 1) * ks],
                                                  wb_ref[kc]).reshape(tc, NCHAIN, sw2)
        for lo in range(0, sw, S5_LANE_BLOCK):
            lb = min(S5_LANE_BLOCK, sw - lo)
            re_sl = slice(lo, lo + lb)
            im_sl = slice(sw + lo, sw + lo + lb)
            ar = are_ref[kc, :, re_sl]
            ai = aim_ref[kc, :, re_sl]
            sre, sim = carry_ref[kc, :, re_sl], carry_ref[kc, :, im_sl]
            for jj in range(tt):
                tloc = tt - 1 - jj if reverse else jj
                nre = ar * sre - ai * sim + bu_ref[b, tloc, :, re_sl]
                nim = ar * sim + ai * sre + bu_ref[b, tloc, :, im_sl]
                if emit_y:
                    st_ref[b, tloc, :, re_sl] = nre
                    st_ref[b, tloc, :, im_sl] = nim
                sre, sim = nre, nim
            carry_ref[kc, :, re_sl] = sre
            carry_ref[kc, :, im_sl] = sim
        if emit_y:
            for m in order:
                st = st_ref[b, m * tc:(m + 1) * tc].reshape(tc * NCHAIN, sw2).astype(BF16)
                y_ref[0, m * tc:(m + 1) * tc, :, kc * ks:(kc + 1) * ks] = (
                    _dot(st, wc_ref[kc]).reshape(tc, NCHAIN, ks))
    sfin_ref[0] = carry_ref[...]


def _s5_pass(s_in, ngroups, wb, wc, are, aim, s0, tt, reverse, emit_y):
    _, ls, _, cs = s_in.shape
    nk, ks, sw2 = wb.shape
    nt = ls // tt
    tmap = (lambda i: nt - 1 - i) if reverse else (lambda i: i)
    state_spec = pl.BlockSpec((1, nk, NCHAIN, sw2), lambda g, i: (g, 0, 0, 0))
    out_specs = [state_spec]
    out_shape = [jax.ShapeDtypeStruct((ngroups, nk, NCHAIN, sw2), F32)]
    scratch = [pltpu.VMEM((2, tt, NCHAIN, sw2), F32)]
    if emit_y:
        out_specs.insert(0, pl.BlockSpec((1, tt, NCHAIN, cs), lambda g, i: (g, tmap(i), 0, 0)))
        out_shape.insert(0, jax.ShapeDtypeStruct((ngroups, ls, NCHAIN, cs), F32))
        scratch.append(pltpu.VMEM((2, tt, NCHAIN, sw2), F32))
    scratch.append(pltpu.VMEM((nk, NCHAIN, sw2), F32))
    res = pl.pallas_call(
        functools.partial(_s5_kernel, reverse=reverse, emit_y=emit_y),
        grid=(ngroups, nt),
        in_specs=[pl.BlockSpec((1, tt, NCHAIN, cs), lambda g, i: (g, tmap(i), 0, 0)),
                  _resident((nk, ks, sw2)), _resident((nk, sw2, ks)),
                  _resident((nk, NCHAIN, sw2 // 2)), _resident((nk, NCHAIN, sw2 // 2)),
                  state_spec],
        out_specs=out_specs,
        out_shape=out_shape,
        scratch_shapes=scratch,
        compiler_params=_cparams(("arbitrary", "arbitrary")),
        name="s5_" + ("bwd" if reverse else "fwd") + ("" if emit_y else "_state"),
    )(s_in, wb, wc, are, aim, s0)
    return (res[0], res[1]) if emit_y else (None, res[0])


def _cmul(are, aim, bre, bim):
    return are * bre - aim * bim, are * bim + aim * bre


def _cpow(re, im, n):
    rre, rim = jnp.ones_like(re), jnp.zeros_like(im)
    while n:
        if n & 1:
            rre, rim = _cmul(rre, rim, re, im)
        re, im = _cmul(re, im, re, im)
        n >>= 1
    return rre, rim


def _s5_params(b_re, b_im, a_re, a_im, log_dt, c_re, c_im):
    ng, ns, nh = b_re.shape
    cs = ng * nh
    ks = min(MXU_DIM, cs)
    gps = ks // nh
    nk = cs // ks
    sw = gps * ns
    a_re, a_im = a_re.astype(F32), a_im.astype(F32)
    dt = jnp.exp(log_dt.astype(F32))[:, None]
    mag = jnp.exp(a_re * dt)
    lre, lim = mag * jnp.cos(a_im * dt), mag * jnp.sin(a_im * dt)
    den = a_re * a_re + a_im * a_im
    zre = ((lre - 1.0) * a_re + lim * a_im) / den
    zim = (lim * a_re - (lre - 1.0) * a_im) / den
    cre, cim = _cmul(c_re.astype(F32), c_im.astype(F32), zre[:, None, :], zim[:, None, :])

    eye = jnp.eye(gps, dtype=F32)

    def in_block(b):
        b = b.astype(F32).reshape(nk, gps, ns, nh)
        return jnp.einsum("kgph,gj->kghjp", b, eye).reshape(nk, ks, sw)

    def out_block(c):
        c = c.reshape(nk, gps, nh, ns)
        return jnp.einsum("kghp,gj->kgpjh", c, eye).reshape(nk, sw, ks)

    wb = jnp.concatenate([in_block(b_re), in_block(b_im)], axis=2).astype(BF16)
    wc = jnp.concatenate([out_block(cre), out_block(-cim)], axis=1).astype(BF16)
    bc = lambda v: jnp.broadcast_to(v.reshape(nk, 1, sw), (nk, NCHAIN, sw))
    return dict(wb=wb, wc=wc, are=bc(lre), aim=bc(lim), lre=lre.reshape(nk, sw), lim=lim.reshape(nk, sw))


def _chain_states(local_end, p, seg, ls, reverse):
    nk, _, sw2 = local_end.shape
    sw = sw2 // 2
    pre, pim = _cpow(p["lre"], p["lim"], ls)
    e = local_end.reshape(nk, NCHAIN // seg, seg, sw2)
    ere, eim = e[..., :sw], e[..., sw:]
    zero = jnp.zeros_like(ere[:, :, 0])
    order = range(seg - 1, -1, -1) if reverse else range(seg)
    sre, sim = zero, zero
    out = [None] * seg
    for k in order:
        out[k] = jnp.concatenate([sre, sim], axis=-1)
        mre, mim = _cmul(pre[:, None, :], pim[:, None, :], sre, sim)
        sre, sim = mre + ere[:, :, k], mim + eim[:, :, k]
    return jnp.stack(out, axis=2).reshape(nk, NCHAIN, sw2)


def _s5_both(s_in, seg, pf, pb, tt):
    ls = s_in.shape[1]
    ys = []
    for p, reverse in ((pf, False), (pb, True)):
        zero = jnp.zeros((1, p["wb"].shape[0], NCHAIN, p["wb"].shape[2]), F32)
        s0 = zero
        if seg > 1:
            _, local_end = _s5_pass(s_in, 1, p["wb"], p["wc"], p["are"], p["aim"], zero, tt, reverse, False)
            s0 = _chain_states(local_end[0], p, seg, ls, reverse)[None]
        y, _ = _s5_pass(s_in, 2, p["wb"], p["wc"], p["are"], p["aim"], jnp.concatenate([s0, zero]), tt,
                        reverse, True)
        ys.append(y)
    return ys


def _glu_kernel(yf_ref, yb_ref, s_ref, d_ref, w_ref, b_ref, o_ref):
    y = yf_ref[...] + yb_ref[...] + d_ref[...] * s_ref[...]
    g = jax.nn.gelu(y)
    o_ref[...] = (g * _sigmoid(_dot(g.astype(BF16), w_ref[...]) + b_ref[...])).astype(o_ref.dtype)


def _glu(yf, yb, s_in, d_skip, glu_w, glu_b, tm):
    rows, cs = yf.shape
    rspec = pl.BlockSpec((tm, cs), lambda i: (i, 0))
    c2 = lambda i: (0, 0)
    return pl.pallas_call(
        _glu_kernel,
        grid=(rows // tm,),
        in_specs=[rspec, rspec, rspec, pl.BlockSpec((1, cs), c2), pl.BlockSpec((cs, cs), c2),
                  pl.BlockSpec((1, cs), c2)],
        out_specs=rspec,
        out_shape=jax.ShapeDtypeStruct((rows, cs), BF16),
        compiler_params=_cparams(("arbitrary",)),
        name="s5_glu",
    )(yf, yb, s_in, d_skip, glu_w, glu_b)


def _pack_halves(v):
    n = v.shape[-1] // 2
    bits = lax.bitcast_convert_type(v.astype(BF16).astype(F32), U32)
    return (bits[..., :n] >> 16) | (bits[..., n:] & jnp.uint32(0xFFFF0000))


def _unpack_halves(p):
    return (lax.bitcast_convert_type(p << 16, F32),
            lax.bitcast_convert_type(p & jnp.uint32(0xFFFF0000), F32))


OUTPROJ_SPLIT = 2


def _outproj_kernel(xp_ref, xs_ref, co_ref, so_ref, g1_ref, sc2_ref, sh2_ref, wo1_ref, wo2_ref, bo_ref,
                    lg_ref, lb_ref, rwh_ref, rwl_ref, x1_ref, hp_ref, lo_ref, *, alpha):
    g = pl.program_id(0)
    tt, d = xp_ref.shape[1], xp_ref.shape[2]
    nj = d // 2 // LANES
    th = tt // OUTPROJ_SPLIT
    for h in range(OUTPROJ_SPLIT):
        ts = slice(h * th, (h + 1) * th)
        rows = slice(h * th * NCHAIN, (h + 1) * th * NCHAIN)
        x = jnp.where(g == 0, xp_ref[:, ts, :], xs_ref[:, ts, :])
        mix = _dot(co_ref[0, rows, :], wo1_ref[...]) + _dot(so_ref[0, rows, :], wo2_ref[...]) + bo_ref[...]
        mix = pltpu.einshape("tcd->ctd", mix.reshape(th, NCHAIN, d))
        x1 = _layer_norm(alpha * x + g1_ref[0] * mix, lg_ref[...], lb_ref[...])
        h2 = x1 * (1.0 + sc2_ref[0]) + sh2_ref[0]
        x1_ref[0, :, ts, :] = x1
        hb = h2.astype(BF16)
        h2f = h2.reshape(NCHAIN * th, d)
        packed = _pack_halves(h2f)
        for j in range(nj):
            hp_ref[0, :, pl.ds(h * th * nj + j, th, stride=nj), :] = (
                packed[:, j * LANES:(j + 1) * LANES].reshape(NCHAIN, th, LANES))
        hi = hb.reshape(NCHAIN * th, d)
        lo = (h2f - hi.astype(F32)).astype(BF16)
        logits = _dot(hi, rwh_ref[...]) + (_dot(hi, rwl_ref[...]) + _dot(lo, rwh_ref[...]))
        lo_ref[0, :, ts, :] = logits.reshape(NCHAIN, th, logits.shape[-1])


def _outproj(xp, xs, conv_out, ssm_out, gate1, scale2, shift2, wo1, wo2, b_out, ln_g, ln_b, rw_hi, rw_lo,
             alpha, tt):
    _, ls, d = xp.shape
    cc, cs = wo1.shape[0], wo2.shape[0]
    ne = rw_hi.shape[1]
    nt = ls // tt
    nj = d // 2 // LANES
    xp_spec, xs_spec = _x_specs(tt, d, nt)
    c2 = lambda g, t: (0, 0)
    nat = lambda w: pl.BlockSpec((1, NCHAIN, tt, w), lambda g, t: (g, 0, t, 0))
    return pl.pallas_call(
        functools.partial(_outproj_kernel, alpha=alpha),
        grid=(2, nt),
        in_specs=[xp_spec, xs_spec,
                  pl.BlockSpec((1, tt * NCHAIN, cc), lambda g, t: (g, t, 0)),
                  pl.BlockSpec((1, tt * NCHAIN, cs), lambda g, t: (g, t, 0)),
                  _mod_spec(d), _mod_spec(d), _mod_spec(d),
                  _resident((cc, d)), _resident((cs, d)), pl.BlockSpec((1, d), c2),
                  pl.BlockSpec((1, d), c2), pl.BlockSpec((1, d), c2),
                  _resident((d, ne)), _resident((d, ne))],
        out_specs=[nat(d),
                   pl.BlockSpec((1, NCHAIN, tt * nj, LANES), lambda g, t: (g, 0, t, 0)),
                   nat(ne)],
        out_shape=[jax.ShapeDtypeStruct((2, NCHAIN, ls, d), F32),
                   jax.ShapeDtypeStruct((2, NCHAIN, ls * nj, LANES), U32),
                   jax.ShapeDtypeStruct((2, NCHAIN, ls, ne), F32)],
        compiler_params=_cparams(("arbitrary", "arbitrary")),
        name="outproj",
    )(xp, xs, conv_out, ssm_out, gate1, scale2, shift2, wo1, wo2, b_out, ln_g, ln_b, rw_hi, rw_lo)


def _route_kernel(lg_ref, bias_ref, eid_ref, e_ref, w_ref, r_ref, cnt_ref, sc_ref, carry_ref, *, span):
    i = pl.program_id(0)
    tm, ne = lg_ref.shape
    gs = ne // N_EXPERT_GROUPS
    neg = jnp.float32(-jnp.inf)

    @pl.when(i == 0)
    def _():
        carry_ref[...] = jnp.zeros_like(carry_ref)

    scores = _sigmoid(lg_ref[...]).T
    biased = scores + bias_ref[...]
    eid = eid_ref[...]

    def first_max(v, ids):
        m = jnp.max(v, axis=0, keepdims=True)
        idx = jnp.min(jnp.where(v == m, ids, float(ne)), axis=0, keepdims=True)
        return m, idx

    gscore = []
    for q in range(N_EXPERT_GROUPS):
        vg, ids = biased[q * gs:(q + 1) * gs], eid[q * gs:(q + 1) * gs]
        m1, i1 = first_max(vg, ids)
        m2 = jnp.max(jnp.where(ids == i1, neg, vg), axis=0, keepdims=True)
        gscore.append(m1 + m2)
    parts = []
    for q in range(N_EXPERT_GROUPS):
        beaten = jnp.zeros((1, tm), F32)
        for o in range(N_EXPERT_GROUPS):
            if o != q:
                wins = (gscore[o] >= gscore[q]) if o < q else (gscore[o] > gscore[q])
                beaten = beaten + wins.astype(F32)
        parts.append(jnp.where(beaten < TOPK_GROUPS, biased[q * gs:(q + 1) * gs], neg))
    masked = jnp.concatenate(parts, axis=0)

    chosen = jnp.zeros((ne, tm), F32)
    hits, e_rows, w_rows = [], [], []
    for k in range(TOP_K):
        _, idx = first_max(masked, eid)
        hit = eid == idx
        hits.append(hit)
        w_rows.append(jnp.sum(jnp.where(hit, scores, 0.0), axis=0, keepdims=True))
        e_rows.append(idx)
        masked = jnp.where(hit, neg, masked)
        chosen = jnp.where(hit, 1.0, chosen)
    w_t = jnp.concatenate(w_rows, axis=0)
    w_ref[...] = w_t / jnp.sum(w_t, axis=0, keepdims=True) * ROUTED_SCALE
    e_ref[...] = jnp.concatenate(e_rows, axis=0).astype(I32)

    rr = lax.broadcasted_iota(I32, (tm, tm), 0)
    cc = lax.broadcasted_iota(I32, (tm, tm), 1)
    before = (rr < cc).astype(BF16)
    prefix = _dot(chosen.astype(BF16), before) + carry_ref[...]
    r_rows = [jnp.sum(jnp.where(hits[k], prefix, 0.0), axis=0, keepdims=True) for k in range(TOP_K)]
    r_ref[...] = jnp.concatenate(r_rows, axis=0).astype(I32)
    carry_ref[...] = carry_ref[...] + jnp.sum(chosen, axis=1, keepdims=True)
    cnt_ref[...] = carry_ref[...]
    span_of = lax.broadcasted_iota(I32, (tm, LANES), 0) // span
    col = lax.broadcasted_iota(I32, (tm, LANES), 1)
    sc_ref[0] = _dot(chosen.astype(BF16), (span_of == col).astype(BF16))


def _route(logits, bias, tm, span):
    t, ne = logits.shape
    nb = tm // span
    assert tm % span == 0 and nb <= LANES
    kspec = pl.BlockSpec((TOP_K, tm), lambda i: (0, i))
    col = pl.BlockSpec((ne, 1), lambda i: (0, 0))
    eid = jnp.broadcast_to(jnp.arange(ne, dtype=F32)[:, None], (ne, tm))
    e_idx, e_w, rank, counts, span_cnt = pl.pallas_call(
        functools.partial(_route_kernel, span=span),
        grid=(t // tm,),
        in_specs=[pl.BlockSpec((tm, ne), lambda i: (i, 0)), col, _resident((ne, tm))],
        out_specs=[kspec, kspec, kspec, col, pl.BlockSpec((1, ne, LANES), lambda i: (i, 0, 0))],
        out_shape=[jax.ShapeDtypeStruct((TOP_K, t), I32), jax.ShapeDtypeStruct((TOP_K, t), F32),
                   jax.ShapeDtypeStruct((TOP_K, t), I32), jax.ShapeDtypeStruct((ne, 1), F32),
                   jax.ShapeDtypeStruct((t // tm, ne, LANES), F32)],
        scratch_shapes=[pltpu.VMEM((ne, 1), F32)],
        compiler_params=_cparams(("arbitrary",)),
        name="route",
    )(logits, bias.reshape(ne, 1), eid)
    span_cnt = span_cnt[:, :, :nb].transpose(0, 2, 1).reshape(t // span, ne).astype(I32)
    return e_idx, e_w, rank, counts, span_cnt


def _dest_kernel(e_ref, r_ref, ps_ref, eid_ref, o_ref):
    nb = ps_ref.shape[0]
    span = e_ref.shape[1] // nb
    for b in range(nb):
        lanes = slice(b * span, (b + 1) * span)
        e = e_ref[:, lanes].astype(F32)
        starts = [jnp.sum(jnp.where(eid_ref[:, lanes] == e[k:k + 1], ps_ref[b], 0.0), axis=0, keepdims=True)
                  for k in range(TOP_K)]
        o_ref[:, lanes] = r_ref[:, lanes] + jnp.concatenate(starts, axis=0).astype(I32)


def _dest(e_idx, rank, table, tm, span=None):
    t = e_idx.shape[1]
    ntab, ne, _ = table.shape
    nb = 1 if span is None else tm // span
    assert ntab == (1 if span is None else t // span)
    kspec = pl.BlockSpec((TOP_K, tm), lambda i: (0, i))
    tspec = pl.BlockSpec((nb, ne, 1), (lambda i: (0, 0, 0)) if span is None else (lambda i: (i, 0, 0)))
    eid = jnp.broadcast_to(jnp.arange(ne, dtype=F32)[:, None], (ne, tm))
    return pl.pallas_call(
        _dest_kernel,
        grid=(t // tm,),
        in_specs=[kspec, kspec, tspec, _resident((ne, tm))],
        out_specs=kspec,
        out_shape=jax.ShapeDtypeStruct((TOP_K, t), I32),
        compiler_params=_cparams(("arbitrary",)),
        name="dest" if span is None else "slot",
    )(e_idx, rank, table, eid)


def _dispatch_kernel(last_ref, nu_ref, nck_ref, src_ref, pos_ref, h_ref, sg_ref, su_ref, sd_ref, xs_ref, sh_ref,
                     zero_ref, stage_ref, sem, zsem, *, nj, tile_rows):
    i = pl.program_id(0)
    n = pl.num_programs(0)
    tm = h_ref.shape[0] // nj
    ne = last_ref.shape[0]
    max_chunks = src_ref.shape[0]
    slot = i % 2
    chunk_rows = COMBINE_CHUNK * nj

    @pl.when(i == 0)
    def _():
        zero_ref[...] = jnp.zeros_like(zero_ref)
        stage_ref[...] = jnp.zeros_like(stage_ref)

        def fill(row):
            start = pl.multiple_of(row * nj, 8)
            return pltpu.make_async_copy(zero_ref, xs_ref.at[pl.ds(start, tile_rows * nj)], zsem)

        def issue(e, c):
            @pl.when(last_ref[e] >= 0)
            def _():
                fill(last_ref[e]).start()
            return c

        def drain(e, c):
            @pl.when(last_ref[e] >= 0)
            def _():
                fill(last_ref[e]).wait()
            return c

        lax.fori_loop(0, ne, issue, 0)
        lax.fori_loop(0, ne, drain, 0)
        n_tiles = xs_ref.shape[0] // (tile_rows * nj)
        lax.fori_loop(nu_ref[0], n_tiles, lambda q, c: (fill(q * tile_rows).start(), c)[1], 0)
        lax.fori_loop(nu_ref[0], n_tiles, lambda q, c: (fill(q * tile_rows).wait(), c)[1], 0)

    def place(tt, c):
        for u in range(COMBINE_UNROLL):
            t = tt * COMBINE_UNROLL + u
            v = h_ref[pl.ds(pl.multiple_of(t * nj, nj), nj), :]
            for k in range(TOP_K):
                p = pos_ref[t * TOP_K + k]
                stage_ref[slot, pl.ds(pl.multiple_of(p * nj, nj), nj), :] = v
        return c

    lax.fori_loop(0, tm // COMBINE_UNROLL, place, 0)

    def chunk_copy(s, q):
        src = stage_ref.at[s, pl.ds(pl.multiple_of(q * chunk_rows, chunk_rows), chunk_rows)]
        dst = xs_ref.at[pl.ds(pl.multiple_of(src_ref[q] * nj, nj), chunk_rows)]
        return pltpu.make_async_copy(src, dst, sem.at[s])

    def wait_chunks(s, count):
        bit = 1
        while bit <= max_chunks:
            @pl.when((count & bit) != 0)
            def _(rows=bit * chunk_rows):
                pltpu.make_async_copy(stage_ref.at[s, pl.ds(0, rows)], xs_ref.at[pl.ds(0, rows)], sem.at[s]).wait()
            bit *= 2

    @pl.when(i > 0)
    def _():
        wait_chunks(1 - slot, nck_ref[jnp.maximum(i - 1, 0)])

    count = nck_ref[i]

    def issue(g, c):
        for u in range(COMBINE_ISSUE):
            chunk_copy(slot, g * COMBINE_ISSUE + u).start()
        return c

    lax.fori_loop(0, count // COMBINE_ISSUE, issue, 0)
    lax.fori_loop(count // COMBINE_ISSUE * COMBINE_ISSUE, count, lambda q, c: (chunk_copy(slot, q).start(), c)[1], 0)

    x = _unpack_rows(h_ref, tm, nj)
    gate = _dot(x, sg_ref[...])
    up = _dot(x, su_ref[...])
    sh_ref[...] = _dot((gate * _sigmoid(gate) * up).astype(BF16), sd_ref[...]).astype(sh_ref.dtype)

    @pl.when(i == n - 1)
    def _():
        wait_chunks(slot, count)


def _dispatch(last_tile_row, n_used, n_chunks, chunk_src, pos_flat, h_packed, sh_gate, sh_up, sh_down,
              n_rows, nj, tile_rows, tm):
    t = h_packed.shape[0] // nj
    n = t // tm
    d, ff = sh_gate.shape
    max_chunks = chunk_src.shape[0] // n
    smem = lambda size: pl.BlockSpec((size,), lambda i, *_: (i,), memory_space=pltpu.SMEM)
    grid_spec = pltpu.PrefetchScalarGridSpec(
        num_scalar_prefetch=3,
        grid=(n,),
        in_specs=[smem(max_chunks), smem(tm * TOP_K),
                  pl.BlockSpec((tm * nj, LANES), lambda i, *_: (i, 0)),
                  _resident((d, ff)), _resident((d, ff)), _resident((ff, d))],
        out_specs=[pl.BlockSpec(memory_space=pl.ANY), pl.BlockSpec((tm, d), lambda i, *_: (i, 0))],
        scratch_shapes=[pltpu.VMEM((tile_rows * nj, LANES), U32),
                        pltpu.VMEM((2, max_chunks * COMBINE_CHUNK * nj, LANES), U32),
                        pltpu.SemaphoreType.DMA((2,)), pltpu.SemaphoreType.DMA],
    )
    return pl.pallas_call(
        functools.partial(_dispatch_kernel, nj=nj, tile_rows=tile_rows),
        grid_spec=grid_spec,
        out_shape=[jax.ShapeDtypeStruct((n_rows * nj, LANES), U32), jax.ShapeDtypeStruct((t, d), BF16)],
        compiler_params=_cparams(("arbitrary",)),
        name="dispatch",
    )(last_tile_row, n_used, n_chunks, chunk_src, pos_flat, h_packed, sh_gate, sh_up, sh_down)


def _unpack_rows(p_ref, tm, nj):
    lo, hi = [], []
    for j in range(nj):
        l, h = _unpack_halves(p_ref[pl.ds(j, tm, stride=nj), :])
        lo.append(l.astype(BF16))
        hi.append(h.astype(BF16))
    return jnp.concatenate(lo + hi, axis=-1)


CAST_ROWS = 128


def _experts_kernel(te_ref, nu_ref, nx_ref, x_ref, wg_hbm, wu_hbm, wd_hbm, y_ref,
                    wgf_ref, wuf_ref, wdf_ref, wgb_ref, wub_ref, wdb_ref, sem, *, nj):
    i = pl.program_id(0)
    tm = x_ref.shape[0] // nj
    live = i < nu_ref[0]
    e = te_ref[i]
    first = live & ((i == 0) | (e != te_ref[jnp.maximum(i - 1, 0)]))

    def fetch(expert):
        return (pltpu.make_async_copy(wg_hbm.at[expert], wgf_ref, sem.at[0]),
                pltpu.make_async_copy(wu_hbm.at[expert], wuf_ref, sem.at[1]),
                pltpu.make_async_copy(wd_hbm.at[expert], wdf_ref, sem.at[2]))

    @pl.when(live & (i == 0))
    def _():
        for c in fetch(e):
            c.start()

    @pl.when(first)
    def _():
        for c in fetch(e):
            c.wait()
        for src, dst in ((wgf_ref, wgb_ref), (wuf_ref, wub_ref), (wdf_ref, wdb_ref)):
            for r in range(0, src.shape[0], CAST_ROWS):
                dst[r:r + CAST_ROWS] = src[r:r + CAST_ROWS].astype(BF16)

        @pl.when(nx_ref[i] >= 0)
        def _():
            for c in fetch(nx_ref[i]):
                c.start()

    @pl.when(live)
    def _():
        x = _unpack_rows(x_ref, tm, nj)
        gate = _dot(x, wgb_ref[...])
        up = _dot(x, wub_ref[...])
        act = (gate * _sigmoid(gate) * up).astype(BF16)
        y = _dot(act, wdb_ref[...])
        packed = _pack_halves(y)
        for j in range(nj):
            y_ref[pl.ds(j, tm, stride=nj), :] = packed[:, j * LANES:(j + 1) * LANES]

    @pl.when(i >= nu_ref[0])
    def _():
        y_ref[...] = jnp.zeros_like(y_ref)


def _experts(tile_e, n_used, next_e, x_sorted, w_gate, w_up, w_down, nj, tm):
    n_rows = x_sorted.shape[0] // nj
    ne, d, ff = w_gate.shape
    n_tiles = n_rows // tm
    hbm = pl.BlockSpec(memory_space=pl.ANY)
    grid_spec = pltpu.PrefetchScalarGridSpec(
        num_scalar_prefetch=3,
        grid=(n_tiles,),
        in_specs=[pl.BlockSpec((tm * nj, LANES), lambda i, te, nu, nx: (jnp.minimum(i, nu[0] - 1), 0)),
                  hbm, hbm, hbm],
        out_specs=pl.BlockSpec((tm * nj, LANES), lambda i, te, nu, nx: (i, 0)),
        scratch_shapes=[pltpu.VMEM((d, ff), F32), pltpu.VMEM((d, ff), F32), pltpu.VMEM((ff, d), F32),
                        pltpu.VMEM((d, ff), BF16), pltpu.VMEM((d, ff), BF16), pltpu.VMEM((ff, d), BF16),
                        pltpu.SemaphoreType.DMA((3,))],
    )
    return pl.pallas_call(
        functools.partial(_experts_kernel, nj=nj),
        grid_spec=grid_spec,
        out_shape=jax.ShapeDtypeStruct((n_rows * nj, LANES), U32),
        compiler_params=_cparams(("arbitrary",)),
        name="experts",
    )(tile_e, n_used, next_e, x_sorted, w_gate, w_up, w_down)


COMBINE_CHUNK = 8
COMBINE_UNROLL = 4
COMBINE_ISSUE = 4


def _combine_kernel(nck_ref, src_ref, srcn_ref, pos_ref, w_ref,
                    x1_ref, sh_ref, g2_ref, lg_ref, lb_ref, ys_ref, op_ref, os_ref,
                    buf_ref, alo_ref, ahi_ref, sem, *, alpha, half, nj):
    i = pl.program_id(0)
    n = pl.num_programs(0)
    tm, d = sh_ref.shape
    max_chunks = src_ref.shape[0]
    slot = i % 2
    chunk_rows = COMBINE_CHUNK * nj

    def fetch(s_ref, count, s):
        def start(g, c):
            for u in range(COMBINE_ISSUE):
                q = g * COMBINE_ISSUE + u
                src = ys_ref.at[pl.ds(pl.multiple_of(s_ref[q] * nj, nj), chunk_rows)]
                dst = buf_ref.at[s, pl.ds(pl.multiple_of(q * chunk_rows, chunk_rows), chunk_rows)]
                pltpu.make_async_copy(src, dst, sem.at[s]).start()
            return c
        lax.fori_loop(0, count // COMBINE_ISSUE, start, 0)

    rounded = lambda c: (c + COMBINE_ISSUE - 1) // COMBINE_ISSUE * COMBINE_ISSUE

    @pl.when(i == 0)
    def _():
        fetch(src_ref, rounded(nck_ref[0]), 0)

    @pl.when(i + 1 < n)
    def _():
        fetch(srcn_ref, rounded(nck_ref[i + 1]), 1 - slot)

    count = rounded(nck_ref[i])
    bit = COMBINE_ISSUE
    while bit <= max_chunks:
        @pl.when((count & bit) != 0)
        def _(rows=bit * chunk_rows):
            pltpu.make_async_copy(ys_ref.at[pl.ds(0, rows)], buf_ref.at[slot, pl.ds(0, rows)],
                                  sem.at[slot]).wait()
        bit *= 2

    def reduce_tokens(tt, c):
        for u in range(COMBINE_UNROLL):
            t = tt * COMBINE_UNROLL + u
            lo = hi = None
            for k in range(TOP_K):
                p = pos_ref[t * TOP_K + k]
                l, h = _unpack_halves(buf_ref[slot, pl.ds(pl.multiple_of(p * nj, nj), nj), :])
                wk = w_ref[t * TOP_K + k]
                lo = l * wk if k == 0 else lo + l * wk
                hi = h * wk if k == 0 else hi + h * wk
            alo_ref[pl.ds(pl.multiple_of(t * nj, nj), nj), :] = lo
            ahi_ref[pl.ds(pl.multiple_of(t * nj, nj), nj), :] = hi
        return c

    lax.fori_loop(0, tm // COMBINE_UNROLL, reduce_tokens, 0)
    routed = jnp.concatenate([alo_ref[pl.ds(j, tm, stride=nj), :] for j in range(nj)]
                             + [ahi_ref[pl.ds(j, tm, stride=nj), :] for j in range(nj)], axis=-1)
    ffn = routed + sh_ref[...].astype(F32)
    out = _layer_norm(alpha * x1_ref[...] + g2_ref[0] * ffn, lg_ref[...], lb_ref[...])

    @pl.when(i < half)
    def _():
        op_ref[...] = out

    @pl.when(i >= half)
    def _():
        os_ref[...] = out


def _combine(n_chunks, chunk_src, pos_flat, w_flat, x1, shared, gate2_rows, y_sorted, ln_g, ln_b, alpha, nj, tm, ls):
    t, d = shared.shape
    n = t // tm
    max_chunks = chunk_src.shape[0] // n
    half = n // 2
    per_chain = ls // tm
    assert tm % COMBINE_UNROLL == 0
    c2 = lambda i, nck: (0, 0)
    row = lambda i, nck: (i, 0)
    smem = lambda size, imap: pl.BlockSpec((size,), imap, memory_space=pltpu.SMEM)
    grid_spec = pltpu.PrefetchScalarGridSpec(
        num_scalar_prefetch=1,
        grid=(n,),
        in_specs=[smem(max_chunks, lambda i, nck: (i,)),
                  smem(max_chunks, lambda i, nck: (jnp.minimum(i + 1, n - 1),)),
                  smem(tm * TOP_K, lambda i, nck: (i,)), smem(tm * TOP_K, lambda i, nck: (i,)),
                  pl.BlockSpec((tm, d), row), pl.BlockSpec((tm, d), row),
                  pl.BlockSpec((1, 1, d), lambda i, nck: (i // per_chain, 0, 0)),
                  pl.BlockSpec((1, d), c2), pl.BlockSpec((1, d), c2),
                  pl.BlockSpec(memory_space=pl.ANY)],
        out_specs=[pl.BlockSpec((tm, d), lambda i, nck: (jnp.minimum(i, half - 1), 0)),
                   pl.BlockSpec((tm, d), lambda i, nck: (jnp.maximum(i - half, 0), 0))],
        scratch_shapes=[pltpu.VMEM((2, max_chunks * COMBINE_CHUNK * nj, LANES), U32),
                        pltpu.VMEM((tm * nj, LANES), F32), pltpu.VMEM((tm * nj, LANES), F32),
                        pltpu.SemaphoreType.DMA((2,))],
    )
    return pl.pallas_call(
        functools.partial(_combine_kernel, alpha=alpha, half=half, nj=nj),
        grid_spec=grid_spec,
        out_shape=[jax.ShapeDtypeStruct((t // 2, d), F32), jax.ShapeDtypeStruct((t // 2, d), F32)],
        compiler_params=_cparams(("arbitrary",)),
        name="combine",
    )(n_chunks, chunk_src, chunk_src, pos_flat, w_flat, x1, shared, gate2_rows, ln_g, ln_b, y_sorted)


def _tile(pref, n, mult=8):
    t = min(pref, n)
    while n % t or t % mult:
        t -= 1
    return t


def _encoder_layer(xp, xs, c_all, chain_seq, seg, alpha, p):
    _, ls, d = xp.shape
    cc = p["conv_w"].shape[-1]

    ada = _ada(c_all, p["w_ada"], p["b_ada"])
    mods = ada.reshape(ada.shape[0], N_ADA, d)[chain_seq].reshape(2, NCHAIN, N_ADA, 1, d)
    shift1, scale1, gate1, shift2, scale2, gate2 = (mods[:, :, k] for k in range(N_ADA))

    w_in = p["w_in"].astype(BF16)
    b_in = p["b_in"].reshape(1, -1)
    u, s_in = _inproj(xp, xs, scale1, shift1, w_in[:, :cc], w_in[:, cc:2 * cc], w_in[:, 2 * cc:],
                      b_in[:, :cc], b_in[:, cc:2 * cc], b_in[:, 2 * cc:], _tile(TILES["inproj"], ls))

    chain = np.arange(NCHAIN)
    prev_ok = jnp.asarray(np.stack([(chain % seg != 0), np.zeros(NCHAIN, bool)]).astype(np.float32)[..., None])
    next_ok = jnp.asarray(np.stack([(chain % seg != seg - 1), np.zeros(NCHAIN, bool)]).astype(np.float32)[..., None])
    conv_out = _conv(u, prev_ok, next_ok, p["conv_w"], p["conv_b"], p["conv_ln_g"], p["conv_ln_b"],
                     _tile(TILES["conv"], ls, CONV_HALO))

    pf = _s5_params(p["ssm_b_re"], p["ssm_b_im"], p["ssm_a_re_f"], p["ssm_a_im_f"], p["ssm_log_dt_f"],
                    p["ssm_c_re_f"], p["ssm_c_im_f"])
    pb = _s5_params(p["ssm_b_re"], p["ssm_b_im"], p["ssm_a_re_b"], p["ssm_a_im_b"], p["ssm_log_dt_b"],
                    p["ssm_c_re_b"], p["ssm_c_im_b"])
    yf, yb = _s5_both(s_in, seg, pf, pb, _tile(TILES["s5"], ls))
    cs = s_in.shape[-1]
    rows_tm = 2 * ls * NCHAIN
    flat = lambda a: a.reshape(rows_tm, cs)
    ssm_out = _glu(flat(yf), flat(yb), flat(s_in), p["ssm_d"].reshape(1, cs),
                   p["ssm_glu_w"].astype(BF16), p["ssm_glu_b"].reshape(1, cs), _tile(TILES["glu"], rows_tm))
    ssm_out = ssm_out.reshape(2, ls * NCHAIN, cs)

    w_out = p["w_out"].astype(BF16)
    rw = p["router_w"].astype(F32)
    rw_hi = rw.astype(BF16)
    rw_lo = (rw - rw_hi.astype(F32)).astype(BF16)
    row = lambda v: v.reshape(1, -1)
    x1, hp, logits = _outproj(xp, xs, conv_out, ssm_out, gate1, scale2, shift2, w_out[:cc], w_out[cc:],
                              row(p["b_out"]), row(p["ln1_g"]), row(p["ln1_b"]), rw_hi, rw_lo,
                              alpha, _tile(TILES["outproj"], ls, 16))

    t = 2 * NCHAIN * ls
    ne = rw.shape[1]
    nj = d // 2 // LANES
    tm_c = _tile(TILES["combine"], ls, COMBINE_UNROLL)
    tm_r = _tile(TILES["route"], t, LANES)
    e_idx, e_w, rank, counts, tile_cnt = _route(logits.reshape(t, ne), row(p["router_bias"]).astype(F32),
                                                tm_r, tm_c)

    tm_e = _tile(TILES["expert"], t)
    slack = COMBINE_CHUNK - 1
    n_rows = (t * TOP_K + ne * (tm_e + slack) + tm_e - 1) // tm_e * tm_e
    counts = counts.reshape(ne).astype(I32)
    padded = jnp.where(counts > 0, (counts + slack + tm_e - 1) // tm_e * tm_e, 0)
    pad_end = jnp.cumsum(padded)
    pad_start = pad_end - padded
    n_tiles = n_rows // tm_e
    tile_start = jnp.arange(n_tiles, dtype=I32)[:, None] * tm_e
    tile_e = jnp.minimum(jnp.sum((pad_end[None, :] <= tile_start).astype(I32), axis=1), ne - 1)
    n_used = (pad_end[-1:] // tm_e).astype(I32)
    pad_tile = (pad_start + counts) // tm_e * tm_e
    fill_rows = jnp.stack([jnp.where(counts > 0, pad_tile, -1),
                           jnp.where((counts > 0) & (pad_tile + tm_e < pad_end), pad_tile + tm_e, -1)],
                          axis=1).reshape(2 * ne).astype(I32)
    following = pad_end[tile_e] // tm_e
    next_e = jnp.where(following < n_used[0], tile_e[jnp.minimum(following, n_tiles - 1)], -1).astype(I32)

    before = jnp.cumsum(tile_cnt, axis=0) - tile_cnt
    run_src = pad_start[None, :] + before
    run_chunks = (tile_cnt + COMBINE_CHUNK - 1) // COMBINE_CHUNK
    chunks_through = jnp.cumsum(run_chunks, axis=1)
    chunks_before = chunks_through - run_chunks
    slot = _dest(e_idx, rank, (chunks_before * COMBINE_CHUNK - before).astype(F32)[:, :, None], tm_r, tm_c)
    slot = slot.T.reshape(t * TOP_K)
    max_chunks = ne + tm_c * TOP_K // COMBINE_CHUNK
    q = jnp.arange(max_chunks, dtype=I32)
    owner = jnp.sum((chunks_through[:, None, :] <= q[None, :, None]).astype(I32), axis=2)
    run_base = run_src - chunks_before * COMBINE_CHUNK
    chunk_src = jnp.sum(jnp.where(jnp.arange(ne, dtype=I32)[None, None, :] == owner[:, :, None],
                                  run_base[:, None, :], 0), axis=2) + q[None, :] * COMBINE_CHUNK

    n_chunks = chunks_through[:, -1].astype(I32)
    chunk_src = chunk_src.astype(I32).reshape(-1)

    x_sorted, shared = _dispatch(fill_rows, n_used, n_chunks, chunk_src, slot, hp.reshape(t * nj, LANES),
                                 p["sh_w_gate"].astype(BF16), p["sh_w_up"].astype(BF16),
                                 p["sh_w_down"].astype(BF16), n_rows, nj, tm_e, tm_c)
    y_sorted = _experts(tile_e, n_used, next_e, x_sorted, p["exp_w_gate"], p["exp_w_up"], p["exp_w_down"],
                        nj, tm_e)
    gate2_rows = gate2.reshape(2 * NCHAIN, 1, d)
    yp, ys = _combine(n_chunks, chunk_src, slot,
                      e_w.T.reshape(t * TOP_K), x1.reshape(t, d), shared, gate2_rows, y_sorted,
                      row(p["ln2_g"]), row(p["ln2_b"]), alpha, nj, tm_c, ls)
    return yp.reshape(NCHAIN, ls, d), ys.reshape(NCHAIN, ls, d)


_PARAM_NAMES = ("w_ada", "b_ada", "w_in", "b_in", "conv_w", "conv_b", "conv_ln_g", "conv_ln_b",
                "ssm_b_re", "ssm_b_im", "ssm_a_re_f", "ssm_a_im_f", "ssm_log_dt_f",
                "ssm_a_re_b", "ssm_a_im_b", "ssm_log_dt_b", "ssm_c_re_f", "ssm_c_im_f",
                "ssm_c_re_b", "ssm_c_im_b", "ssm_d", "ssm_glu_w", "ssm_glu_b", "w_out", "b_out",
                "ln1_g", "ln1_b", "router_w", "router_bias", "exp_w_gate", "exp_w_up", "exp_w_down",
                "sh_w_gate", "sh_w_up", "sh_w_down", "ln2_g", "ln2_b")


def kernel(x_prompt, x_sample, c_prompt, c_sample, w_ada, b_ada, w_in, b_in, conv_w, conv_b, conv_ln_g, conv_ln_b, ssm_b_re, ssm_b_im, ssm_a_re_f, ssm_a_im_f, ssm_log_dt_f, ssm_a_re_b, ssm_a_im_b, ssm_log_dt_b, ssm_c_re_f, ssm_c_im_f, ssm_c_re_b, ssm_c_im_b, ssm_d, ssm_glu_w, ssm_glu_b, w_out, b_out, ln1_g, ln1_b, router_w, router_bias, exp_w_gate, exp_w_up, exp_w_down, sh_w_gate, sh_w_up, sh_w_down, ln2_g, ln2_b):
    stacked = (w_ada, b_ada, w_in, b_in, conv_w, conv_b, conv_ln_g, conv_ln_b, ssm_b_re, ssm_b_im,
               ssm_a_re_f, ssm_a_im_f, ssm_log_dt_f, ssm_a_re_b, ssm_a_im_b, ssm_log_dt_b, ssm_c_re_f,
               ssm_c_im_f, ssm_c_re_b, ssm_c_im_b, ssm_d, ssm_glu_w, ssm_glu_b, w_out, b_out, ln1_g, ln1_b,
               router_w, router_bias, exp_w_gate, exp_w_up, exp_w_down, sh_w_gate, sh_w_up, sh_w_down,
               ln2_g, ln2_b)
    depth = w_ada.shape[0]
    alpha = (2 * depth) ** 0.25
    bp, lp, d = x_prompt.shape
    bs, lsample, _ = x_sample.shape
    assert NCHAIN % bp == 0 and bs == NCHAIN and lp % (NCHAIN // bp) == 0
    seg = NCHAIN // bp
    ls = lp // seg
    assert ls == lsample

    c_all = jnp.concatenate([c_prompt, c_sample], axis=0)
    c_all = jnp.pad(c_all, ((0, -c_all.shape[0] % 8), (0, 0)))
    chain_seq = np.concatenate([np.arange(NCHAIN) // seg, bp + np.arange(NCHAIN)])

    xp = x_prompt.reshape(NCHAIN, ls, d)
    xs = x_sample.reshape(NCHAIN, ls, d)
    for l in range(depth):
        params = {n: v[l] for n, v in zip(_PARAM_NAMES, stacked)}
        xp, xs = _encoder_layer(xp, xs, c_all, chain_seq, seg, alpha, params)
    return xp.reshape(bp, lp, d), xs.reshape(bs, lsample, d)
```

```python
import functools
import math

import numpy as np
import jax
import jax.numpy as jnp
from jax import lax
from jax.experimental import pallas as pl
from jax.experimental.pallas import tpu as pltpu

F32 = jnp.float32
BF16 = jnp.bfloat16
U32 = jnp.uint32
I32 = jnp.int32

LN_EPS = 1e-5
N_ADA = 6
TOP_K = 8
N_EXPERT_GROUPS = 8
TOPK_GROUPS = 4
ROUTED_SCALE = 2.5

NCHAIN = 8
LANES = 128
MXU_DIM = 256
VMEM_LIMIT = 56 * 1024 * 1024

TILES = dict(inproj=64, conv=128, s5=128, glu=1024, outproj=32, route=512, expert=512, combine=256)


def _cparams(sem):
    return pltpu.CompilerParams(dimension_semantics=sem, vmem_limit_bytes=VMEM_LIMIT)


def _resident(shape):
    zeros = (0,) * len(shape)
    return pl.BlockSpec(shape, lambda *_: zeros, pipeline_mode=pl.Buffered(1))


def _dot(a, b):
    return jnp.dot(a, b, preferred_element_type=F32)


def _sigmoid(x):
    return jax.nn.sigmoid(x)


def _layer_norm(v, g, b):
    mu = jnp.mean(v, axis=-1, keepdims=True)
    d = v - mu
    var = jnp.mean(d * d, axis=-1, keepdims=True)
    return d * lax.rsqrt(var + LN_EPS) * g + b


def _ada_kernel(c_ref, w_ref, b_ref, o_ref):
    c = c_ref[...]
    s = (c * _sigmoid(c)).astype(BF16)
    o_ref[...] = _dot(s, w_ref[...].astype(BF16)) + b_ref[...]


def _ada(c_all, w_ada, b_ada):
    rows, d = c_all.shape
    n = w_ada.shape[1]
    tn = _tile(2048, n, LANES)
    return pl.pallas_call(
        _ada_kernel,
        grid=(n // tn,),
        in_specs=[pl.BlockSpec((rows, d), lambda j: (0, 0)),
                  pl.BlockSpec((d, tn), lambda j: (0, j)),
                  pl.BlockSpec((1, tn), lambda j: (0, j))],
        out_specs=pl.BlockSpec((rows, tn), lambda j: (0, j)),
        out_shape=jax.ShapeDtypeStruct((rows, n), F32),
        compiler_params=_cparams(("arbitrary",)),
        name="ada",
    )(c_all, w_ada, b_ada.reshape(1, n))


def _x_specs(tt, d, nt):
    xp = pl.BlockSpec((NCHAIN, tt, d), lambda g, t: (0, jnp.where(g == 0, t, nt - 1), 0))
    xs = pl.BlockSpec((NCHAIN, tt, d), lambda g, t: (0, jnp.where(g == 1, t, 0), 0))
    return xp, xs


def _mod_spec(d):
    return pl.BlockSpec((1, NCHAIN, 1, d), lambda g, t: (g, 0, 0, 0))


def _inproj_kernel(xp_ref, xs_ref, sc_ref, sh_ref, wa_ref, wg_ref, ws_ref, ba_ref, bg_ref, bs_ref,
                   u_ref, s_ref):
    g = pl.program_id(0)
    x = jnp.where(g == 0, xp_ref[...], xs_ref[...])
    h = x * (1.0 + sc_ref[0]) + sh_ref[0]
    tt = h.shape[1]
    ht = pltpu.einshape("ctd->tcd", h).reshape(tt * NCHAIN, h.shape[2]).astype(BF16)
    a = _dot(ht, wa_ref[...]) + ba_ref[...]
    gt = _dot(ht, wg_ref[...]) + bg_ref[...]
    u = a * _sigmoid(gt)
    s = _dot(ht, ws_ref[...]) + bs_ref[...]
    u_ref[0] = u.reshape(tt, NCHAIN, u.shape[-1])
    s_ref[0] = s.reshape(tt, NCHAIN, s.shape[-1])


def _inproj(xp, xs, scale1, shift1, wa, wg, ws, ba, bg, bs, tt):
    _, ls, d = xp.shape
    cc, cs = wa.shape[1], ws.shape[1]
    nt = ls // tt
    xp_spec, xs_spec = _x_specs(tt, d, nt)
    const2 = lambda g, t: (0, 0)
    return pl.pallas_call(
        _inproj_kernel,
        grid=(2, nt),
        in_specs=[xp_spec, xs_spec, _mod_spec(d), _mod_spec(d),
                  _resident((d, cc)), _resident((d, cc)), _resident((d, cs)),
                  pl.BlockSpec((1, cc), const2), pl.BlockSpec((1, cc), const2), pl.BlockSpec((1, cs), const2)],
        out_specs=[pl.BlockSpec((1, tt, NCHAIN, cc), lambda g, t: (g, t, 0, 0)),
                   pl.BlockSpec((1, tt, NCHAIN, cs), lambda g, t: (g, t, 0, 0))],
        out_shape=[jax.ShapeDtypeStruct((2, ls, NCHAIN, cc), F32),
                   jax.ShapeDtypeStruct((2, ls, NCHAIN, cs), F32)],
        compiler_params=_cparams(("arbitrary", "arbitrary")),
        name="inproj",
    )(xp, xs, scale1, shift1, wa, wg, ws, ba, bg, bs)


CONV_HALO = 16
CONV_BLOCK = 16


def _conv_kernel(u_ref, up_ref, un_ref, pok_ref, nok_ref, w_ref, cb_ref, g_ref, b_ref, o_ref,
                 buf_ref, acc_ref, *, width):
    t = pl.program_id(1)
    nt = pl.num_programs(1)
    tt = u_ref.shape[1]
    pad = (width - 1) // 2
    prev = up_ref[0]
    prev_wrapped = pltpu.roll(prev, shift=1, axis=1) * pok_ref[0]
    buf_ref[0:CONV_HALO] = jnp.where(t == 0, prev_wrapped, prev)
    buf_ref[CONV_HALO:CONV_HALO + tt] = u_ref[0]
    nxt = un_ref[0]
    next_wrapped = pltpu.roll(nxt, shift=NCHAIN - 1, axis=1) * nok_ref[0]
    buf_ref[CONV_HALO + tt:2 * CONV_HALO + tt] = jnp.where(t == nt - 1, next_wrapped, nxt)

    base = CONV_HALO - pad

    for c0 in range(0, u_ref.shape[3], LANES):
        lanes = slice(c0, c0 + LANES)
        taps = [w_ref[k, :, lanes] for k in range(width)]
        bias = cb_ref[:, lanes]

        def body(b, carry, lanes=lanes, taps=taps, bias=bias):
            t0 = b * CONV_BLOCK
            acc = [bias] * CONV_BLOCK
            for s in range(CONV_BLOCK + width - 1):
                x = buf_ref[t0 + base + s, :, lanes]
                for i in range(CONV_BLOCK):
                    if 0 <= s - i < width:
                        acc[i] = acc[i] + x * taps[s - i]
            for i in range(CONV_BLOCK):
                acc_ref[t0 + i, :, lanes] = acc[i]
            return carry

        lax.fori_loop(0, tt // CONV_BLOCK, body, 0)
    v = _layer_norm(acc_ref[...], g_ref[...], b_ref[...])
    v = v * _sigmoid(v)
    o_ref[0] = v.reshape(tt * NCHAIN, v.shape[-1]).astype(o_ref.dtype)


def _conv(u, prev_ok, next_ok, conv_w, conv_b, ln_g, ln_b, tt):
    _, ls, _, cc = u.shape
    width = conv_w.shape[0]
    assert (width - 1) // 2 <= CONV_HALO and tt % CONV_HALO == 0 and cc % LANES == 0
    nt = ls // tt
    hb = tt // CONV_HALO
    nh = ls // CONV_HALO
    wb = jnp.broadcast_to(conv_w[:, None, :], (width, NCHAIN, cc))
    row = lambda v: jnp.broadcast_to(v[None, :], (NCHAIN, cc))
    const2 = lambda g, t: (0, 0)
    return pl.pallas_call(
        functools.partial(_conv_kernel, width=width),
        grid=(2, nt),
        in_specs=[pl.BlockSpec((1, tt, NCHAIN, cc), lambda g, t: (g, t, 0, 0)),
                  pl.BlockSpec((1, CONV_HALO, NCHAIN, cc),
                               lambda g, t: (g, jnp.where(t == 0, nh - 1, t * hb - 1), 0, 0)),
                  pl.BlockSpec((1, CONV_HALO, NCHAIN, cc),
                               lambda g, t: (g, jnp.where(t == nt - 1, 0, (t + 1) * hb), 0, 0)),
                  pl.BlockSpec((1, NCHAIN, 1), lambda g, t: (g, 0, 0)),
                  pl.BlockSpec((1, NCHAIN, 1), lambda g, t: (g, 0, 0)),
                  pl.BlockSpec((width, NCHAIN, cc), lambda g, t: (0, 0, 0)),
                  pl.BlockSpec((NCHAIN, cc), const2), pl.BlockSpec((NCHAIN, cc), const2),
                  pl.BlockSpec((NCHAIN, cc), const2)],
        out_specs=pl.BlockSpec((1, tt * NCHAIN, cc), lambda g, t: (g, t, 0)),
        out_shape=jax.ShapeDtypeStruct((2, ls * NCHAIN, cc), BF16),
        scratch_shapes=[pltpu.VMEM((tt + 2 * CONV_HALO, NCHAIN, cc), F32),
                        pltpu.VMEM((tt, NCHAIN, cc), F32)],
        compiler_params=_cparams(("arbitrary", "arbitrary")),
        name="conv",
    )(u, u, u, prev_ok, next_ok, wb, row(conv_b), row(ln_g), row(ln_b))


S5_LANE_BLOCK = 512
S5_ROW_CHUNKS = {False: 2, True: 4}


def _s5_kernel(u_ref, wb_ref, wc_ref, are_ref, aim_ref, s0_ref, *rest, reverse, emit_y):
    if emit_y:
        y_ref, sfin_ref, bu_ref, st_ref, carry_ref = rest
    else:
        sfin_ref, bu_ref, carry_ref = rest
        st_ref = None
    i = pl.program_id(1)
    tt = u_ref.shape[1]
    nk, ks, sw2 = wb_ref.shape
    sw = sw2 // 2

    @pl.when(i == 0)
    def _():
        carry_ref[...] = s0_ref[0]

    u2 = u_ref[0].reshape(tt * NCHAIN, u_ref.shape[3]).astype(BF16)
    chunks = S5_ROW_CHUNKS[reverse]
    tc = tt // chunks
    order = range(chunks - 1, -1, -1) if reverse else range(chunks)
    for kc in range(nk):
        b = kc % 2
        for m in order:
            rows = slice(m * tc * NCHAIN, (m + 1) * tc * NCHAIN)
            bu_ref[b, m * tc:(m + 1) * tc] = _dot(u2[rows, kc * ks:(kc + 1) * ks],
                                                  wb_ref[kc]).reshape(tc, NCHAIN, sw2)
        for lo in range(0, sw, S5_LANE_BLOCK):
            lb = min(S5_LANE_BLOCK, sw - lo)
            re_sl = slice(lo, lo + lb)
            im_sl = slice(sw + lo, sw + lo + lb)
            ar = are_ref[kc, :, re_sl]
            ai = aim_ref[kc, :, re_sl]
            sre, sim = carry_ref[kc, :, re_sl], carry_ref[kc, :, im_sl]
            for jj in range(tt):
                tloc = tt - 1 - jj if reverse else jj
                nre = ar * sre - ai * sim + bu_ref[b, tloc, :, re_sl]
                nim = ar * sim + ai * sre + bu_ref[b, tloc, :, im_sl]
                if emit_y:
                    st_ref[b, tloc, :, re_sl] = nre
                    st_ref[b, tloc, :, im_sl] = nim
                sre, sim = nre, nim
            carry_ref[kc, :, re_sl] = sre
            carry_ref[kc, :, im_sl] = sim
        if emit_y:
            for m in order:
                st = st_ref[b, m * tc:(m + 1) * tc].reshape(tc * NCHAIN, sw2).astype(BF16)
                y_ref[0, m * tc:(m + 1) * tc, :, kc * ks:(kc + 1) * ks] = (
                    _dot(st, wc_ref[kc]).reshape(tc, NCHAIN, ks))
    sfin_ref[0] = carry_ref[...]


def _s5_pass(s_in, ngroups, wb, wc, are, aim, s0, tt, reverse, emit_y):
    _, ls, _, cs = s_in.shape
    nk, ks, sw2 = wb.shape
    nt = ls // tt
    tmap = (lambda i: nt - 1 - i) if reverse else (lambda i: i)
    state_spec = pl.BlockSpec((1, nk, NCHAIN, sw2), lambda g, i: (g, 0, 0, 0))
    out_specs = [state_spec]
    out_shape = [jax.ShapeDtypeStruct((ngroups, nk, NCHAIN, sw2), F32)]
    scratch = [pltpu.VMEM((2, tt, NCHAIN, sw2), F32)]
    if emit_y:
        out_specs.insert(0, pl.BlockSpec((1, tt, NCHAIN, cs), lambda g, i: (g, tmap(i), 0, 0)))
        out_shape.insert(0, jax.ShapeDtypeStruct((ngroups, ls, NCHAIN, cs), F32))
        scratch.append(pltpu.VMEM((2, tt, NCHAIN, sw2), F32))
    scratch.append(pltpu.VMEM((nk, NCHAIN, sw2), F32))
    res = pl.pallas_call(
        functools.partial(_s5_kernel, reverse=reverse, emit_y=emit_y),
        grid=(ngroups, nt),
        in_specs=[pl.BlockSpec((1, tt, NCHAIN, cs), lambda g, i: (g, tmap(i), 0, 0)),
                  _resident((nk, ks, sw2)), _resident((nk, sw2, ks)),
                  _resident((nk, NCHAIN, sw2 // 2)), _resident((nk, NCHAIN, sw2 // 2)),
                  state_spec],
        out_specs=out_specs,
        out_shape=out_shape,
        scratch_shapes=scratch,
        compiler_params=_cparams(("arbitrary", "arbitrary")),
        name="s5_" + ("bwd" if reverse else "fwd") + ("" if emit_y else "_state"),
    )(s_in, wb, wc, are, aim, s0)
    return (res[0], res[1]) if emit_y else (None, res[0])


def _cmul(are, aim, bre, bim):
    return are * bre - aim * bim, are * bim + aim * bre


def _cpow(re, im, n):
    rre, rim = jnp.ones_like(re), jnp.zeros_like(im)
    while n:
        if n & 1:
            rre, rim = _cmul(rre, rim, re, im)
        re, im = _cmul(re, im, re, im)
        n >>= 1
    return rre, rim


def _s5_params(b_re, b_im, a_re, a_im, log_dt, c_re, c_im):
    ng, ns, nh = b_re.shape
    cs = ng * nh
    ks = min(MXU_DIM, cs)
    gps = ks // nh
    nk = cs // ks
    sw = gps * ns
    a_re, a_im = a_re.astype(F32), a_im.astype(F32)
    dt = jnp.exp(log_dt.astype(F32))[:, None]
    mag = jnp.exp(a_re * dt)
    lre, lim = mag * jnp.cos(a_im * dt), mag * jnp.sin(a_im * dt)
    den = a_re * a_re + a_im * a_im
    zre = ((lre - 1.0) * a_re + lim * a_im) / den
    zim = (lim * a_re - (lre - 1.0) * a_im) / den
    cre, cim = _cmul(c_re.astype(F32), c_im.astype(F32), zre[:, None, :], zim[:, None, :])

    eye = jnp.eye(gps, dtype=F32)

    def in_block(b):
        b = b.astype(F32).reshape(nk, gps, ns, nh)
        return jnp.einsum("kgph,gj->kghjp", b, eye).reshape(nk, ks, sw)

    def out_block(c):
        c = c.reshape(nk, gps, nh, ns)
        return jnp.einsum("kghp,gj->kgpjh", c, eye).reshape(nk, sw, ks)

    wb = jnp.concatenate([in_block(b_re), in_block(b_im)], axis=2).astype(BF16)
    wc = jnp.concatenate([out_block(cre), out_block(-cim)], axis=1).astype(BF16)
    bc = lambda v: jnp.broadcast_to(v.reshape(nk, 1, sw), (nk, NCHAIN, sw))
    return dict(wb=wb, wc=wc, are=bc(lre), aim=bc(lim), lre=lre.reshape(nk, sw), lim=lim.reshape(nk, sw))


def _chain_states(local_end, p, seg, ls, reverse):
    nk, _, sw2 = local_end.shape
    sw = sw2 // 2
    pre, pim = _cpow(p["lre"], p["lim"], ls)
    e = local_end.reshape(nk, NCHAIN // seg, seg, sw2)
    ere, eim = e[..., :sw], e[..., sw:]
    zero = jnp.zeros_like(ere[:, :, 0])
    order = range(seg - 1, -1, -1) if reverse else range(seg)
    sre, sim = zero, zero
    out = [None] * seg
    for k in order:
        out[k] = jnp.concatenate([sre, sim], axis=-1)
        mre, mim = _cmul(pre[:, None, :], pim[:, None, :], sre, sim)
        sre, sim = mre + ere[:, :, k], mim + eim[:, :, k]
    return jnp.stack(out, axis=2).reshape(nk, NCHAIN, sw2)


def _s5_both(s_in, seg, pf, pb, tt):
    ls = s_in.shape[1]
    ys = []
    for p, reverse in ((pf, False), (pb, True)):
        zero = jnp.zeros((1, p["wb"].shape[0], NCHAIN, p["wb"].shape[2]), F32)
        s0 = zero
        if seg > 1:
            _, local_end = _s5_pass(s_in, 1, p["wb"], p["wc"], p["are"], p["aim"], zero, tt, reverse, False)
            s0 = _chain_states(local_end[0], p, seg, ls, reverse)[None]
        y, _ = _s5_pass(s_in, 2, p["wb"], p["wc"], p["are"], p["aim"], jnp.concatenate([s0, zero]), tt,
                        reverse, True)
        ys.append(y)
    return ys


def _glu_kernel(yf_ref, yb_ref, s_ref, d_ref, w_ref, b_ref, o_ref):
    y = yf_ref[...] + yb_ref[...] + d_ref[...] * s_ref[...]
    g = jax.nn.gelu(y)
    o_ref[...] = (g * _sigmoid(_dot(g.astype(BF16), w_ref[...]) + b_ref[...])).astype(o_ref.dtype)


def _glu(yf, yb, s_in, d_skip, glu_w, glu_b, tm):
    rows, cs = yf.shape
    rspec = pl.BlockSpec((tm, cs), lambda i: (i, 0))
    c2 = lambda i: (0, 0)
    return pl.pallas_call(
        _glu_kernel,
        grid=(rows // tm,),
        in_specs=[rspec, rspec, rspec, pl.BlockSpec((1, cs), c2), pl.BlockSpec((cs, cs), c2),
                  pl.BlockSpec((1, cs), c2)],
        out_specs=rspec,
        out_shape=jax.ShapeDtypeStruct((rows, cs), BF16),
        compiler_params=_cparams(("arbitrary",)),
        name="s5_glu",
    )(yf, yb, s_in, d_skip, glu_w, glu_b)


def _pack_halves(v):
    n = v.shape[-1] // 2
    bits = lax.bitcast_convert_type(v.astype(BF16).astype(F32), U32)
    return (bits[..., :n] >> 16) | (bits[..., n:] & jnp.uint32(0xFFFF0000))


def _unpack_halves(p):
    return (lax.bitcast_convert_type(p << 16, F32),
            lax.bitcast_convert_type(p & jnp.uint32(0xFFFF0000), F32))


OUTPROJ_SPLIT = 2


def _outproj_kernel(xp_ref, xs_ref, co_ref, so_ref, g1_ref, sc2_ref, sh2_ref, wo1_ref, wo2_ref, bo_ref,
                    lg_ref, lb_ref, rwh_ref, rwl_ref, x1_ref, hp_ref, lo_ref, *, alpha):
    g = pl.program_id(0)
    tt, d = xp_ref.shape[1], xp_ref.shape[2]
    nj = d // 2 // LANES
    th = tt // OUTPROJ_SPLIT
    for h in range(OUTPROJ_SPLIT):
        ts = slice(h * th, (h + 1) * th)
        rows = slice(h * th * NCHAIN, (h + 1) * th * NCHAIN)
        x = jnp.where(g == 0, xp_ref[:, ts, :], xs_ref[:, ts, :])
        mix = _dot(co_ref[0, rows, :], wo1_ref[...]) + _dot(so_ref[0, rows, :], wo2_ref[...]) + bo_ref[...]
        mix = pltpu.einshape("tcd->ctd", mix.reshape(th, NCHAIN, d))
        x1 = _layer_norm(alpha * x + g1_ref[0] * mix, lg_ref[...], lb_ref[...])
        h2 = x1 * (1.0 + sc2_ref[0]) + sh2_ref[0]
        x1_ref[0, :, ts, :] = x1
        hb = h2.astype(BF16)
        h2f = h2.reshape(NCHAIN * th, d)
        packed = _pack_halves(h2f)
        for j in range(nj):
            hp_ref[0, :, pl.ds(h * th * nj + j, th, stride=nj), :] = (
                packed[:, j * LANES:(j + 1) * LANES].reshape(NCHAIN, th, LANES))
        hi = hb.reshape(NCHAIN * th, d)
        lo = (h2f - hi.astype(F32)).astype(BF16)
        logits = _dot(hi, rwh_ref[...]) + (_dot(hi, rwl_ref[...]) + _dot(lo, rwh_ref[...]))
        lo_ref[0, :, ts, :] = logits.reshape(NCHAIN, th, logits.shape[-1])


def _outproj(xp, xs, conv_out, ssm_out, gate1, scale2, shift2, wo1, wo2, b_out, ln_g, ln_b, rw_hi, rw_lo,
             alpha, tt):
    _, ls, d = xp.shape
    cc, cs = wo1.shape[0], wo2.shape[0]
    ne = rw_hi.shape[1]
    nt = ls // tt
    nj = d // 2 // LANES
    xp_spec, xs_spec = _x_specs(tt, d, nt)
    c2 = lambda g, t: (0, 0)
    nat = lambda w: pl.BlockSpec((1, NCHAIN, tt, w), lambda g, t: (g, 0, t, 0))
    return pl.pallas_call(
        functools.partial(_outproj_kernel, alpha=alpha),
        grid=(2, nt),
        in_specs=[xp_spec, xs_spec,
                  pl.BlockSpec((1, tt * NCHAIN, cc), lambda g, t: (g, t, 0)),
                  pl.BlockSpec((1, tt * NCHAIN, cs), lambda g, t: (g, t, 0)),
                  _mod_spec(d), _mod_spec(d), _mod_spec(d),
                  _resident((cc, d)), _resident((cs, d)), pl.BlockSpec((1, d), c2),
                  pl.BlockSpec((1, d), c2), pl.BlockSpec((1, d), c2),
                  _resident((d, ne)), _resident((d, ne))],
        out_specs=[nat(d),
                   pl.BlockSpec((1, NCHAIN, tt * nj, LANES), lambda g, t: (g, 0, t, 0)),
                   nat(ne)],
        out_shape=[jax.ShapeDtypeStruct((2, NCHAIN, ls, d), F32),
                   jax.ShapeDtypeStruct((2, NCHAIN, ls * nj, LANES), U32),
                   jax.ShapeDtypeStruct((2, NCHAIN, ls, ne), F32)],
        compiler_params=_cparams(("arbitrary", "arbitrary")),
        name="outproj",
    )(xp, xs, conv_out, ssm_out, gate1, scale2, shift2, wo1, wo2, b_out, ln_g, ln_b, rw_hi, rw_lo)


def _route_kernel(lg_ref, bias_ref, eid_ref, e_ref, w_ref, r_ref, cnt_ref, sc_ref, carry_ref, *, span):
    i = pl.program_id(0)
    tm, ne = lg_ref.shape
    gs = ne // N_EXPERT_GROUPS
    neg = jnp.float32(-jnp.inf)

    @pl.when(i == 0)
    def _():
        carry_ref[...] = jnp.zeros_like(carry_ref)

    scores = _sigmoid(lg_ref[...]).T
    biased = scores + bias_ref[...]
    eid = eid_ref[...]

    def first_max(v, ids):
        m = jnp.max(v, axis=0, keepdims=True)
        idx = jnp.min(jnp.where(v == m, ids, float(ne)), axis=0, keepdims=True)
        return m, idx

    gscore = []
    for q in range(N_EXPERT_GROUPS):
        vg, ids = biased[q * gs:(q + 1) * gs], eid[q * gs:(q + 1) * gs]
        m1, i1 = first_max(vg, ids)
        m2 = jnp.max(jnp.where(ids == i1, neg, vg), axis=0, keepdims=True)
        gscore.append(m1 + m2)
    parts = []
    for q in range(N_EXPERT_GROUPS):
        beaten = jnp.zeros((1, tm), F32)
        for o in range(N_EXPERT_GROUPS):
            if o != q:
                wins = (gscore[o] >= gscore[q]) if o < q else (gscore[o] > gscore[q])
                beaten = beaten + wins.astype(F32)
        parts.append(jnp.where(beaten < TOPK_GROUPS, biased[q * gs:(q + 1) * gs], neg))
    masked = jnp.concatenate(parts, axis=0)

    chosen = jnp.zeros((ne, tm), F32)
    hits, e_rows, w_rows = [], [], []
    for k in range(TOP_K):
        _, idx = first_max(masked, eid)
        hit = eid == idx
        hits.append(hit)
        w_rows.append(jnp.sum(jnp.where(hit, scores, 0.0), axis=0, keepdims=True))
        e_rows.append(idx)
        masked = jnp.where(hit, neg, masked)
        chosen = jnp.where(hit, 1.0, chosen)
    w_t = jnp.concatenate(w_rows, axis=0)
    w_ref[...] = w_t / jnp.sum(w_t, axis=0, keepdims=True) * ROUTED_SCALE
    e_ref[...] = jnp.concatenate(e_rows, axis=0).astype(I32)

    rr = lax.broadcasted_iota(I32, (tm, tm), 0)
    cc = lax.broadcasted_iota(I32, (tm, tm), 1)
    before = (rr < cc).astype(BF16)
    prefix = _dot(chosen.astype(BF16), before) + carry_ref[...]
    r_rows = [jnp.sum(jnp.where(hits[k], prefix, 0.0), axis=0, keepdims=True) for k in range(TOP_K)]
    r_ref[...] = jnp.concatenate(r_rows, axis=0).astype(I32)
    carry_ref[...] = carry_ref[...] + jnp.sum(chosen, axis=1, keepdims=True)
    cnt_ref[...] = carry_ref[...]
    span_of = lax.broadcasted_iota(I32, (tm, LANES), 0) // span
    col = lax.broadcasted_iota(I32, (tm, LANES), 1)
    sc_ref[0] = _dot(chosen.astype(BF16), (span_of == col).astype(BF16))


def _route(logits, bias, tm, span):
    t, ne = logits.shape
    nb = tm // span
    assert tm % span == 0 and nb <= LANES
    kspec = pl.BlockSpec((TOP_K, tm), lambda i: (0, i))
    col = pl.BlockSpec((ne, 1), lambda i: (0, 0))
    eid = jnp.broadcast_to(jnp.arange(ne, dtype=F32)[:, None], (ne, tm))
    e_idx, e_w, rank, counts, span_cnt = pl.pallas_call(
        functools.partial(_route_kernel, span=span),
        grid=(t // tm,),
        in_specs=[pl.BlockSpec((tm, ne), lambda i: (i, 0)), col, _resident((ne, tm))],
        out_specs=[kspec, kspec, kspec, col, pl.BlockSpec((1, ne, LANES), lambda i: (i, 0, 0))],
        out_shape=[jax.ShapeDtypeStruct((TOP_K, t), I32), jax.ShapeDtypeStruct((TOP_K, t), F32),
                   jax.ShapeDtypeStruct((TOP_K, t), I32), jax.ShapeDtypeStruct((ne, 1), F32),
                   jax.ShapeDtypeStruct((t // tm, ne, LANES), F32)],
        scratch_shapes=[pltpu.VMEM((ne, 1), F32)],
        compiler_params=_cparams(("arbitrary",)),
        name="route",
    )(logits, bias.reshape(ne, 1), eid)
    span_cnt = span_cnt[:, :, :nb].transpose(0, 2, 1).reshape(t // span, ne).astype(I32)
    return e_idx, e_w, rank, counts, span_cnt


def _dest_kernel(e_ref, r_ref, ps_ref, eid_ref, o_ref):
    nb = ps_ref.shape[0]
    span = e_ref.shape[1] // nb
    for b in range(nb):
        lanes = slice(b * span, (b + 1) * span)
        e = e_ref[:, lanes].astype(F32)
        starts = [jnp.sum(jnp.where(eid_ref[:, lanes] == e[k:k + 1], ps_ref[b], 0.0), axis=0, keepdims=True)
                  for k in range(TOP_K)]
        o_ref[:, lanes] = r_ref[:, lanes] + jnp.concatenate(starts, axis=0).astype(I32)


def _dest(e_idx, rank, table, tm, span=None):
    t = e_idx.shape[1]
    ntab, ne, _ = table.shape
    nb = 1 if span is None else tm // span
    assert ntab == (1 if span is None else t // span)
    kspec = pl.BlockSpec((TOP_K, tm), lambda i: (0, i))
    tspec = pl.BlockSpec((nb, ne, 1), (lambda i: (0, 0, 0)) if span is None else (lambda i: (i, 0, 0)))
    eid = jnp.broadcast_to(jnp.arange(ne, dtype=F32)[:, None], (ne, tm))
    return pl.pallas_call(
        _dest_kernel,
        grid=(t // tm,),
        in_specs=[kspec, kspec, tspec, _resident((ne, tm))],
        out_specs=kspec,
        out_shape=jax.ShapeDtypeStruct((TOP_K, t), I32),
        compiler_params=_cparams(("arbitrary",)),
        name="dest" if span is None else "slot",
    )(e_idx, rank, table, eid)


def _dispatch_kernel(last_ref, nu_ref, nck_ref, src_ref, pos_ref, h_ref, sg_ref, su_ref, sd_ref, xs_ref, sh_ref,
                     zero_ref, stage_ref, sem, zsem, *, nj, tile_rows):
    i = pl.program_id(0)
    n = pl.num_programs(0)
    tm = h_ref.shape[0] // nj
    ne = last_ref.shape[0]
    max_chunks = src_ref.shape[0]
    slot = i % 2
    chunk_rows = COMBINE_CHUNK * nj

    @pl.when(i == 0)
    def _():
        zero_ref[...] = jnp.zeros_like(zero_ref)
        stage_ref[...] = jnp.zeros_like(stage_ref)

        def fill(row):
            start = pl.multiple_of(row * nj, 8)
            return pltpu.make_async_copy(zero_ref, xs_ref.at[pl.ds(start, tile_rows * nj)], zsem)

        def issue(e, c):
            @pl.when(last_ref[e] >= 0)
            def _():
                fill(last_ref[e]).start()
            return c

        def drain(e, c):
            @pl.when(last_ref[e] >= 0)
            def _():
                fill(last_ref[e]).wait()
            return c

        lax.fori_loop(0, ne, issue, 0)
        lax.fori_loop(0, ne, drain, 0)
        n_tiles = xs_ref.shape[0] // (tile_rows * nj)
        lax.fori_loop(nu_ref[0], n_tiles, lambda q, c: (fill(q * tile_rows).start(), c)[1], 0)
        lax.fori_loop(nu_ref[0], n_tiles, lambda q, c: (fill(q * tile_rows).wait(), c)[1], 0)

    def place(tt, c):
        for u in range(COMBINE_UNROLL):
            t = tt * COMBINE_UNROLL + u
            v = h_ref[pl.ds(pl.multiple_of(t * nj, nj), nj), :]
            for k in range(TOP_K):
                p = pos_ref[t * TOP_K + k]
                stage_ref[slot, pl.ds(pl.multiple_of(p * nj, nj), nj), :] = v
        return c

    lax.fori_loop(0, tm // COMBINE_UNROLL, place, 0)

    def chunk_copy(s, q):
        src = stage_ref.at[s, pl.ds(pl.multiple_of(q * chunk_rows, chunk_rows), chunk_rows)]
        dst = xs_ref.at[pl.ds(pl.multiple_of(src_ref[q] * nj, nj), chunk_rows)]
        return pltpu.make_async_copy(src, dst, sem.at[s])

    def wait_chunks(s, count):
        bit = 1
        while bit <= max_chunks:
            @pl.when((count & bit) != 0)
            def _(rows=bit * chunk_rows):
                pltpu.make_async_copy(stage_ref.at[s, pl.ds(0, rows)], xs_ref.at[pl.ds(0, rows)], sem.at[s]).wait()
            bit *= 2

    @pl.when(i > 0)
    def _():
        wait_chunks(1 - slot, nck_ref[jnp.maximum(i - 1, 0)])

    count = nck_ref[i]

    def issue(g, c):
        for u in range(COMBINE_ISSUE):
            chunk_copy(slot, g * COMBINE_ISSUE + u).start()
        return c

    lax.fori_loop(0, count // COMBINE_ISSUE, issue, 0)
    lax.fori_loop(count // COMBINE_ISSUE * COMBINE_ISSUE, count, lambda q, c: (chunk_copy(slot, q).start(), c)[1], 0)

    x = _unpack_rows(h_ref, tm, nj)
    gate = _dot(x, sg_ref[...])
    up = _dot(x, su_ref[...])
    sh_ref[...] = _dot((gate * _sigmoid(gate) * up).astype(BF16), sd_ref[...]).astype(sh_ref.dtype)

    @pl.when(i == n - 1)
    def _():
        wait_chunks(slot, count)


def _dispatch(last_tile_row, n_used, n_chunks, chunk_src, pos_flat, h_packed, sh_gate, sh_up, sh_down,
              n_rows, nj, tile_rows, tm):
    t = h_packed.shape[0] // nj
    n = t // tm
    d, ff = sh_gate.shape
    max_chunks = chunk_src.shape[0] // n
    smem = lambda size: pl.BlockSpec((size,), lambda i, *_: (i,), memory_space=pltpu.SMEM)
    grid_spec = pltpu.PrefetchScalarGridSpec(
        num_scalar_prefetch=3,
        grid=(n,),
        in_specs=[smem(max_chunks), smem(tm * TOP_K),
                  pl.BlockSpec((tm * nj, LANES), lambda i, *_: (i, 0)),
                  _resident((d, ff)), _resident((d, ff)), _resident((ff, d))],
        out_specs=[pl.BlockSpec(memory_space=pl.ANY), pl.BlockSpec((tm, d), lambda i, *_: (i, 0))],
        scratch_shapes=[pltpu.VMEM((tile_rows * nj, LANES), U32),
                        pltpu.VMEM((2, max_chunks * COMBINE_CHUNK * nj, LANES), U32),
                        pltpu.SemaphoreType.DMA((2,)), pltpu.SemaphoreType.DMA],
    )
    return pl.pallas_call(
        functools.partial(_dispatch_kernel, nj=nj, tile_rows=tile_rows),
        grid_spec=grid_spec,
        out_shape=[jax.ShapeDtypeStruct((n_rows * nj, LANES), U32), jax.ShapeDtypeStruct((t, d), BF16)],
        compiler_params=_cparams(("arbitrary",)),
        name="dispatch",
    )(last_tile_row, n_used, n_chunks, chunk_src, pos_flat, h_packed, sh_gate, sh_up, sh_down)


def _unpack_rows(p_ref, tm, nj):
    lo, hi = [], []
    for j in range(nj):
        l, h = _unpack_halves(p_ref[pl.ds(j, tm, stride=nj), :])
        lo.append(l.astype(BF16))
        hi.append(h.astype(BF16))
    return jnp.concatenate(lo + hi, axis=-1)


CAST_ROWS = 128


def _experts_kernel(te_ref, nu_ref, nx_ref, x_ref, wg_hbm, wu_hbm, wd_hbm, y_ref,
                    wgf_ref, wuf_ref, wdf_ref, wgb_ref, wub_ref, wdb_ref, sem, *, nj):
    i = pl.program_id(0)
    tm = x_ref.shape[0] // nj
    live = i < nu_ref[0]
    e = te_ref[i]
    first = live & ((i == 0) | (e != te_ref[jnp.maximum(i - 1, 0)]))

    def fetch(expert):
        return (pltpu.make_async_copy(wg_hbm.at[expert], wgf_ref, sem.at[0]),
                pltpu.make_async_copy(wu_hbm.at[expert], wuf_ref, sem.at[1]),
                pltpu.make_async_copy(wd_hbm.at[expert], wdf_ref, sem.at[2]))

    @pl.when(live & (i == 0))
    def _():
        for c in fetch(e):
            c.start()

    @pl.when(first)
    def _():
        for c in fetch(e):
            c.wait()
        for src, dst in ((wgf_ref, wgb_ref), (wuf_ref, wub_ref), (wdf_ref, wdb_ref)):
            for r in range(0, src.shape[0], CAST_ROWS):
                dst[r:r + CAST_ROWS] = src[r:r + CAST_ROWS].astype(BF16)

        @pl.when(nx_ref[i] >= 0)
        def _():
            for c in fetch(nx_ref[i]):
                c.start()

    @pl.when(live)
    def _():
        x = _unpack_rows(x_ref, tm, nj)
        gate = _dot(x, wgb_ref[...])
        up = _dot(x, wub_ref[...])
        act = (gate * _sigmoid(gate) * up).astype(BF16)
        y = _dot(act, wdb_ref[...])
        packed = _pack_halves(y)
        for j in range(nj):
            y_ref[pl.ds(j, tm, stride=nj), :] = packed[:, j * LANES:(j + 1) * LANES]

    @pl.when(i >= nu_ref[0])
    def _():
        y_ref[...] = jnp.zeros_like(y_ref)


def _experts(tile_e, n_used, next_e, x_sorted, w_gate, w_up, w_down, nj, tm):
    n_rows = x_sorted.shape[0] // nj
    ne, d, ff = w_gate.shape
    n_tiles = n_rows // tm
    hbm = pl.BlockSpec(memory_space=pl.ANY)
    grid_spec = pltpu.PrefetchScalarGridSpec(
        num_scalar_prefetch=3,
        grid=(n_tiles,),
        in_specs=[pl.BlockSpec((tm * nj, LANES), lambda i, te, nu, nx: (jnp.minimum(i, nu[0] - 1), 0)),
                  hbm, hbm, hbm],
        out_specs=pl.BlockSpec((tm * nj, LANES), lambda i, te, nu, nx: (i, 0)),
        scratch_shapes=[pltpu.VMEM((d, ff), F32), pltpu.VMEM((d, ff), F32), pltpu.VMEM((ff, d), F32),
                        pltpu.VMEM((d, ff), BF16), pltpu.VMEM((d, ff), BF16), pltpu.VMEM((ff, d), BF16),
                        pltpu.SemaphoreType.DMA((3,))],
    )
    return pl.pallas_call(
        functools.partial(_experts_kernel, nj=nj),
        grid_spec=grid_spec,
        out_shape=jax.ShapeDtypeStruct((n_rows * nj, LANES), U32),
        compiler_params=_cparams(("arbitrary",)),
        name="experts",
    )(tile_e, n_used, next_e, x_sorted, w_gate, w_up, w_down)


COMBINE_CHUNK = 8
COMBINE_UNROLL = 4
COMBINE_ISSUE = 4


def _combine_kernel(nck_ref, src_ref, srcn_ref, pos_ref, w_ref,
                    x1_ref, sh_ref, g2_ref, lg_ref, lb_ref, ys_ref, op_ref, os_ref,
                    buf_ref, alo_ref, ahi_ref, sem, *, alpha, half, nj):
    i = pl.program_id(0)
    n = pl.num_programs(0)
    tm, d = sh_ref.shape
    max_chunks = src_ref.shape[0]
    slot = i % 2
    chunk_rows = COMBINE_CHUNK * nj

    def fetch(s_ref, count, s):
        def start(g, c):
            for u in range(COMBINE_ISSUE):
                q = g * COMBINE_ISSUE + u
                src = ys_ref.at[pl.ds(pl.multiple_of(s_ref[q] * nj, nj), chunk_rows)]
                dst = buf_ref.at[s, pl.ds(pl.multiple_of(q * chunk_rows, chunk_rows), chunk_rows)]
                pltpu.make_async_copy(src, dst, sem.at[s]).start()
            return c
        lax.fori_loop(0, count // COMBINE_ISSUE, start, 0)

    rounded = lambda c: (c + COMBINE_ISSUE - 1) // COMBINE_ISSUE * COMBINE_ISSUE

    @pl.when(i == 0)
    def _():
        fetch(src_ref, rounded(nck_ref[0]), 0)

    @pl.when(i + 1 < n)
    def _():
        fetch(srcn_ref, rounded(nck_ref[i + 1]), 1 - slot)

    count = rounded(nck_ref[i])
    bit = COMBINE_ISSUE
    while bit <= max_chunks:
        @pl.when((count & bit) != 0)
        def _(rows=bit * chunk_rows):
            pltpu.make_async_copy(ys_ref.at[pl.ds(0, rows)], buf_ref.at[slot, pl.ds(0, rows)],
                                  sem.at[slot]).wait()
        bit *= 2

    def reduce_tokens(tt, c):
        for u in range(COMBINE_UNROLL):
            t = tt * COMBINE_UNROLL + u
            lo = hi = None
            for k in range(TOP_K):
                p = pos_ref[t * TOP_K + k]
                l, h = _unpack_halves(buf_ref[slot, pl.ds(pl.multiple_of(p * nj, nj), nj), :])
                wk = w_ref[t * TOP_K + k]
                lo = l * wk if k == 0 else lo + l * wk
                hi = h * wk if k == 0 else hi + h * wk
            alo_ref[pl.ds(pl.multiple_of(t * nj, nj), nj), :] = lo
            ahi_ref[pl.ds(pl.multiple_of(t * nj, nj), nj), :] = hi
        return c

    lax.fori_loop(0, tm // COMBINE_UNROLL, reduce_tokens, 0)
    routed = jnp.concatenate([alo_ref[pl.ds(j, tm, stride=nj), :] for j in range(nj)]
                             + [ahi_ref[pl.ds(j, tm, stride=nj), :] for j in range(nj)], axis=-1)
    ffn = routed + sh_ref[...].astype(F32)
    out = _layer_norm(alpha * x1_ref[...] + g2_ref[0] * ffn, lg_ref[...], lb_ref[...])

    @pl.when(i < half)
    def _():
        op_ref[...] = out

    @pl.when(i >= half)
    def _():
        os_ref[...] = out


def _combine(n_chunks, chunk_src, pos_flat, w_flat, x1, shared, gate2_rows, y_sorted, ln_g, ln_b, alpha, nj, tm, ls):
    t, d = shared.shape
    n = t // tm
    max_chunks = chunk_src.shape[0] // n
    half = n // 2
    per_chain = ls // tm
    assert tm % COMBINE_UNROLL == 0
    c2 = lambda i, nck: (0, 0)
    row = lambda i, nck: (i, 0)
    smem = lambda size, imap: pl.BlockSpec((size,), imap, memory_space=pltpu.SMEM)
    grid_spec = pltpu.PrefetchScalarGridSpec(
        num_scalar_prefetch=1,
        grid=(n,),
        in_specs=[smem(max_chunks, lambda i, nck: (i,)),
                  smem(max_chunks, lambda i, nck: (jnp.minimum(i + 1, n - 1),)),
                  smem(tm * TOP_K, lambda i, nck: (i,)), smem(tm * TOP_K, lambda i, nck: (i,)),
                  pl.BlockSpec((tm, d), row), pl.BlockSpec((tm, d), row),
                  pl.BlockSpec((1, 1, d), lambda i, nck: (i // per_chain, 0, 0)),
                  pl.BlockSpec((1, d), c2), pl.BlockSpec((1, d), c2),
                  pl.BlockSpec(memory_space=pl.ANY)],
        out_specs=[pl.BlockSpec((tm, d), lambda i, nck: (jnp.minimum(i, half - 1), 0)),
                   pl.BlockSpec((tm, d), lambda i, nck: (jnp.maximum(i - half, 0), 0))],
        scratch_shapes=[pltpu.VMEM((2, max_chunks * COMBINE_CHUNK * nj, LANES), U32),
                        pltpu.VMEM((tm * nj, LANES), F32), pltpu.VMEM((tm * nj, LANES), F32),
                        pltpu.SemaphoreType.DMA((2,))],
    )
    return pl.pallas_call(
        functools.partial(_combine_kernel, alpha=alpha, half=half, nj=nj),
        grid_spec=grid_spec,
        out_shape=[jax.ShapeDtypeStruct((t // 2, d), F32), jax.ShapeDtypeStruct((t // 2, d), F32)],
        compiler_params=_cparams(("arbitrary",)),
        name="combine",
    )(n_chunks, chunk_src, chunk_src, pos_flat, w_flat, x1, shared, gate2_rows, ln_g, ln_b, y_sorted)


def _tile(pref, n, mult=8):
    t = min(pref, n)
    while n % t or t % mult:
        t -= 1
    return t


def _encoder_layer(xp, xs, c_all, chain_seq, seg, alpha, p):
    _, ls, d = xp.shape
    cc = p["conv_w"].shape[-1]

    ada = _ada(c_all, p["w_ada"], p["b_ada"])
    mods = ada.reshape(ada.shape[0], N_ADA, d)[chain_seq].reshape(2, NCHAIN, N_ADA, 1, d)
    shift1, scale1, gate1, shift2, scale2, gate2 = (mods[:, :, k] for k in range(N_ADA))

    w_in = p["w_in"].astype(BF16)
    b_in = p["b_in"].reshape(1, -1)
    u, s_in = _inproj(xp, xs, scale1, shift1, w_in[:, :cc], w_in[:, cc:2 * cc], w_in[:, 2 * cc:],
                      b_in[:, :cc], b_in[:, cc:2 * cc], b_in[:, 2 * cc:], _tile(TILES["inproj"], ls))

    chain = np.arange(NCHAIN)
    prev_ok = jnp.asarray(np.stack([(chain % seg != 0), np.zeros(NCHAIN, bool)]).astype(np.float32)[..., None])
    next_ok = jnp.asarray(np.stack([(chain % seg != seg - 1), np.zeros(NCHAIN, bool)]).astype(np.float32)[..., None])
    conv_out = _conv(u, prev_ok, next_ok, p["conv_w"], p["conv_b"], p["conv_ln_g"], p["conv_ln_b"],
                     _tile(TILES["conv"], ls, CONV_HALO))

    pf = _s5_params(p["ssm_b_re"], p["ssm_b_im"], p["ssm_a_re_f"], p["ssm_a_im_f"], p["ssm_log_dt_f"],
                    p["ssm_c_re_f"], p["ssm_c_im_f"])
    pb = _s5_params(p["ssm_b_re"], p["ssm_b_im"], p["ssm_a_re_b"], p["ssm_a_im_b"], p["ssm_log_dt_b"],
                    p["ssm_c_re_b"], p["ssm_c_im_b"])
    yf, yb = _s5_both(s_in, seg, pf, pb, _tile(TILES["s5"], ls))
    cs = s_in.shape[-1]
    rows_tm = 2 * ls * NCHAIN
    flat = lambda a: a.reshape(rows_tm, cs)
    ssm_out = _glu(flat(yf), flat(yb), flat(s_in), p["ssm_d"].reshape(1, cs),
                   p["ssm_glu_w"].astype(BF16), p["ssm_glu_b"].reshape(1, cs), _tile(TILES["glu"], rows_tm))
    ssm_out = ssm_out.reshape(2, ls * NCHAIN, cs)

    w_out = p["w_out"].astype(BF16)
    rw = p["router_w"].astype(F32)
    rw_hi = rw.astype(BF16)
    rw_lo = (rw - rw_hi.astype(F32)).astype(BF16)
    row = lambda v: v.reshape(1, -1)
    x1, hp, logits = _outproj(xp, xs, conv_out, ssm_out, gate1, scale2, shift2, w_out[:cc], w_out[cc:],
                              row(p["b_out"]), row(p["ln1_g"]), row(p["ln1_b"]), rw_hi, rw_lo,
                              alpha, _tile(TILES["outproj"], ls, 16))

    t = 2 * NCHAIN * ls
    ne = rw.shape[1]
    nj = d // 2 // LANES
    tm_c = _tile(TILES["combine"], ls, COMBINE_UNROLL)
    tm_r = _tile(TILES["route"], t, LANES)
    e_idx, e_w, rank, counts, tile_cnt = _route(logits.reshape(t, ne), row(p["router_bias"]).astype(F32),
                                                tm_r, tm_c)

    tm_e = _tile(TILES["expert"], t)
    slack = COMBINE_CHUNK - 1
    n_rows = (t * TOP_K + ne * (tm_e + slack) + tm_e - 1) // tm_e * tm_e
    counts = counts.reshape(ne).astype(I32)
    padded = jnp.where(counts > 0, (counts + slack + tm_e - 1) // tm_e * tm_e, 0)
    pad_end = jnp.cumsum(padded)
    pad_start = pad_end - padded
    n_tiles = n_rows // tm_e
    tile_start = jnp.arange(n_tiles, dtype=I32)[:, None] * tm_e
    tile_e = jnp.minimum(jnp.sum((pad_end[None, :] <= tile_start).astype(I32), axis=1), ne - 1)
    n_used = (pad_end[-1:] // tm_e).astype(I32)
    pad_tile = (pad_start + counts) // tm_e * tm_e
    fill_rows = jnp.stack([jnp.where(counts > 0, pad_tile, -1),
                           jnp.where((counts > 0) & (pad_tile + tm_e < pad_end), pad_tile + tm_e, -1)],
                          axis=1).reshape(2 * ne).astype(I32)
    following = pad_end[tile_e] // tm_e
    next_e = jnp.where(following < n_used[0], tile_e[jnp.minimum(following, n_tiles - 1)], -1).astype(I32)

    before = jnp.cumsum(tile_cnt, axis=0) - tile_cnt
    run_src = pad_start[None, :] + before
    run_chunks = (tile_cnt + COMBINE_CHUNK - 1) // COMBINE_CHUNK
    chunks_through = jnp.cumsum(run_chunks, axis=1)
    chunks_before = chunks_through - run_chunks
    slot = _dest(e_idx, rank, (chunks_before * COMBINE_CHUNK - before).astype(F32)[:, :, None], tm_r, tm_c)
    slot = slot.T.reshape(t * TOP_K)
    max_chunks = ne + tm_c * TOP_K // COMBINE_CHUNK
    q = jnp.arange(max_chunks, dtype=I32)
    owner = jnp.sum((chunks_through[:, None, :] <= q[None, :, None]).astype(I32), axis=2)
    run_base = run_src - chunks_before * COMBINE_CHUNK
    chunk_src = jnp.sum(jnp.where(jnp.arange(ne, dtype=I32)[None, None, :] == owner[:, :, None],
                                  run_base[:, None, :], 0), axis=2) + q[None, :] * COMBINE_CHUNK

    n_chunks = chunks_through[:, -1].astype(I32)
    chunk_src = chunk_src.astype(I32).reshape(-1)

    x_sorted, shared = _dispatch(fill_rows, n_used, n_chunks, chunk_src, slot, hp.reshape(t * nj, LANES),
                                 p["sh_w_gate"].astype(BF16), p["sh_w_up"].astype(BF16),
                                 p["sh_w_down"].astype(BF16), n_rows, nj, tm_e, tm_c)
    y_sorted = _experts(tile_e, n_used, next_e, x_sorted, p["exp_w_gate"], p["exp_w_up"], p["exp_w_down"],
                        nj, tm_e)
    gate2_rows = gate2.reshape(2 * NCHAIN, 1, d)
    yp, ys = _combine(n_chunks, chunk_src, slot,
                      e_w.T.reshape(t * TOP_K), x1.reshape(t, d), shared, gate2_rows, y_sorted,
                      row(p["ln2_g"]), row(p["ln2_b"]), alpha, nj, tm_c, ls)
    return yp.reshape(NCHAIN, ls, d), ys.reshape(NCHAIN, ls, d)


_PARAM_NAMES = ("w_ada", "b_ada", "w_in", "b_in", "conv_w", "conv_b", "conv_ln_g", "conv_ln_b",
                "ssm_b_re", "ssm_b_im", "ssm_a_re_f", "ssm_a_im_f", "ssm_log_dt_f",
                "ssm_a_re_b", "ssm_a_im_b", "ssm_log_dt_b", "ssm_c_re_f", "ssm_c_im_f",
                "ssm_c_re_b", "ssm_c_im_b", "ssm_d", "ssm_glu_w", "ssm_glu_b", "w_out", "b_out",
                "ln1_g", "ln1_b", "router_w", "router_bias", "exp_w_gate", "exp_w_up", "exp_w_down",
                "sh_w_gate", "sh_w_up", "sh_w_down", "ln2_g", "ln2_b")


def kernel(x_prompt, x_sample, c_prompt, c_sample, w_ada, b_ada, w_in, b_in, conv_w, conv_b, conv_ln_g, conv_ln_b, ssm_b_re, ssm_b_im, ssm_a_re_f, ssm_a_im_f, ssm_log_dt_f, ssm_a_re_b, ssm_a_im_b, ssm_log_dt_b, ssm_c_re_f, ssm_c_im_f, ssm_c_re_b, ssm_c_im_b, ssm_d, ssm_glu_w, ssm_glu_b, w_out, b_out, ln1_g, ln1_b, router_w, router_bias, exp_w_gate, exp_w_up, exp_w_down, sh_w_gate, sh_w_up, sh_w_down, ln2_g, ln2_b):
    stacked = (w_ada, b_ada, w_in, b_in, conv_w, conv_b, conv_ln_g, conv_ln_b, ssm_b_re, ssm_b_im,
               ssm_a_re_f, ssm_a_im_f, ssm_log_dt_f, ssm_a_re_b, ssm_a_im_b, ssm_log_dt_b, ssm_c_re_f,
               ssm_c_im_f, ssm_c_re_b, ssm_c_im_b, ssm_d, ssm_glu_w, ssm_glu_b, w_out, b_out, ln1_g, ln1_b,
               router_w, router_bias, exp_w_gate, exp_w_up, exp_w_down, sh_w_gate, sh_w_up, sh_w_down,
               ln2_g, ln2_b)
    depth = w_ada.shape[0]
    alpha = (2 * depth) ** 0.25
    bp, lp, d = x_prompt.shape
    bs, lsample, _ = x_sample.shape
    assert NCHAIN % bp == 0 and bs == NCHAIN and lp % (NCHAIN // bp) == 0
    seg = NCHAIN // bp
    ls = lp // seg
    assert ls == lsample

    c_all = jnp.concatenate([c_prompt, c_sample], axis=0)
    c_all = jnp.pad(c_all, ((0, -c_all.shape[0] % 8), (0, 0)))
    chain_seq = np.concatenate([np.arange(NCHAIN) // seg, bp + np.arange(NCHAIN)])

    xp = x_prompt.reshape(NCHAIN, ls, d)
    xs = x_sample.reshape(NCHAIN, ls, d)
    for l in range(depth):
        params = {n: v[l] for n, v in zip(_PARAM_NAMES, stacked)}
        xp, xs = _encoder_layer(xp, xs, c_all, chain_seq, seg, alpha, params)
    return xp.reshape(bp, lp, d), xs.reshape(bs, lsample, d)
```

```python
import functools
import math

import numpy as np
import jax
import jax.numpy as jnp
from jax import lax
from jax.experimental import pallas as pl
from jax.experimental.pallas import tpu as pltpu

F32 = jnp.float32
BF16 = jnp.bfloat16
U32 = jnp.uint32
I32 = jnp.int32

LN_EPS = 1e-5
N_ADA = 6
TOP_K = 8
N_EXPERT_GROUPS = 8
TOPK_GROUPS = 4
ROUTED_SCALE = 2.5

NCHAIN = 8
LANES = 128
MXU_DIM = 256
VMEM_LIMIT = 56 * 1024 * 1024

TILES = dict(inproj=64, conv=128, s5=128, glu=1024, outproj=32, route=512, expert=512, combine=256)


def _cparams(sem):
    return pltpu.CompilerParams(dimension_semantics=sem, vmem_limit_bytes=VMEM_LIMIT)


def _resident(shape):
    zeros = (0,) * len(shape)
    return pl.BlockSpec(shape, lambda *_: zeros, pipeline_mode=pl.Buffered(1))


def _dot(a, b):
    return jnp.dot(a, b, preferred_element_type=F32)


def _sigmoid(x):
    return jax.nn.sigmoid(x)


def _layer_norm(v, g, b):
    mu = jnp.mean(v, axis=-1, keepdims=True)
    d = v - mu
    var = jnp.mean(d * d, axis=-1, keepdims=True)
    return d * lax.rsqrt(var + LN_EPS) * g + b


def _ada_kernel(c_ref, w_ref, b_ref, o_ref):
    c = c_ref[...]
    s = (c * _sigmoid(c)).astype(BF16)
    o_ref[...] = _dot(s, w_ref[...].astype(BF16)) + b_ref[...]


def _ada(c_all, w_ada, b_ada):
    rows, d = c_all.shape
    n = w_ada.shape[1]
    tn = _tile(2048, n, LANES)
    return pl.pallas_call(
        _ada_kernel,
        grid=(n // tn,),
        in_specs=[pl.BlockSpec((rows, d), lambda j: (0, 0)),
                  pl.BlockSpec((d, tn), lambda j: (0, j)),
                  pl.BlockSpec((1, tn), lambda j: (0, j))],
        out_specs=pl.BlockSpec((rows, tn), lambda j: (0, j)),
        out_shape=jax.ShapeDtypeStruct((rows, n), F32),
        compiler_params=_cparams(("arbitrary",)),
        name="ada",
    )(c_all, w_ada, b_ada.reshape(1, n))


def _x_specs(tt, d, nt):
    xp = pl.BlockSpec((NCHAIN, tt, d), lambda g, t: (0, jnp.where(g == 0, t, nt - 1), 0))
    xs = pl.BlockSpec((NCHAIN, tt, d), lambda g, t: (0, jnp.where(g == 1, t, 0), 0))
    return xp, xs


def _mod_spec(d):
    return pl.BlockSpec((1, NCHAIN, 1, d), lambda g, t: (g, 0, 0, 0))


def _inproj_kernel(xp_ref, xs_ref, sc_ref, sh_ref, wa_ref, wg_ref, ws_ref, ba_ref, bg_ref, bs_ref,
                   u_ref, s_ref):
    g = pl.program_id(0)
    x = jnp.where(g == 0, xp_ref[...], xs_ref[...])
    h = x * (1.0 + sc_ref[0]) + sh_ref[0]
    tt = h.shape[1]
    ht = pltpu.einshape("ctd->tcd", h).reshape(tt * NCHAIN, h.shape[2]).astype(BF16)
    a = _dot(ht, wa_ref[...]) + ba_ref[...]
    gt = _dot(ht, wg_ref[...]) + bg_ref[...]
    u = a * _sigmoid(gt)
    s = _dot(ht, ws_ref[...]) + bs_ref[...]
    u_ref[0] = u.reshape(tt, NCHAIN, u.shape[-1])
    s_ref[0] = s.reshape(tt, NCHAIN, s.shape[-1])


def _inproj(xp, xs, scale1, shift1, wa, wg, ws, ba, bg, bs, tt):
    _, ls, d = xp.shape
    cc, cs = wa.shape[1], ws.shape[1]
    nt = ls // tt
    xp_spec, xs_spec = _x_specs(tt, d, nt)
    const2 = lambda g, t: (0, 0)
    return pl.pallas_call(
        _inproj_kernel,
        grid=(2, nt),
        in_specs=[xp_spec, xs_spec, _mod_spec(d), _mod_spec(d),
                  _resident((d, cc)), _resident((d, cc)), _resident((d, cs)),
                  pl.BlockSpec((1, cc), const2), pl.BlockSpec((1, cc), const2), pl.BlockSpec((1, cs), const2)],
        out_specs=[pl.BlockSpec((1, tt, NCHAIN, cc), lambda g, t: (g, t, 0, 0)),
                   pl.BlockSpec((1, tt, NCHAIN, cs), lambda g, t: (g, t, 0, 0))],
        out_shape=[jax.ShapeDtypeStruct((2, ls, NCHAIN, cc), F32),
                   jax.ShapeDtypeStruct((2, ls, NCHAIN, cs), F32)],
        compiler_params=_cparams(("arbitrary", "arbitrary")),
        name="inproj",
    )(xp, xs, scale1, shift1, wa, wg, ws, ba, bg, bs)


CONV_HALO = 16
CONV_BLOCK = 16


def _conv_kernel(u_ref, up_ref, un_ref, pok_ref, nok_ref, w_ref, cb_ref, g_ref, b_ref, o_ref,
                 buf_ref, acc_ref, *, width):
    t = pl.program_id(1)
    nt = pl.num_programs(1)
    tt = u_ref.shape[1]
    pad = (width - 1) // 2
    prev = up_ref[0]
    prev_wrapped = pltpu.roll(prev, shift=1, axis=1) * pok_ref[0]
    buf_ref[0:CONV_HALO] = jnp.where(t == 0, prev_wrapped, prev)
    buf_ref[CONV_HALO:CONV_HALO + tt] = u_ref[0]
    nxt = un_ref[0]
    next_wrapped = pltpu.roll(nxt, shift=NCHAIN - 1, axis=1) * nok_ref[0]
    buf_ref[CONV_HALO + tt:2 * CONV_HALO + tt] = jnp.where(t == nt - 1, next_wrapped, nxt)

    base = CONV_HALO - pad

    for c0 in range(0, u_ref.shape[3], LANES):
        lanes = slice(c0, c0 + LANES)
        taps = [w_ref[k, :, lanes] for k in range(width)]
        bias = cb_ref[:, lanes]

        def body(b, carry, lanes=lanes, taps=taps, bias=bias):
            t0 = b * CONV_BLOCK
            acc = [bias] * CONV_BLOCK
            for s in range(CONV_BLOCK + width - 1):
                x = buf_ref[t0 + base + s, :, lanes]
                for i in range(CONV_BLOCK):
                    if 0 <= s - i < width:
                        acc[i] = acc[i] + x * taps[s - i]
            for i in range(CONV_BLOCK):
                acc_ref[t0 + i, :, lanes] = acc[i]
            return carry

        lax.fori_loop(0, tt // CONV_BLOCK, body, 0)
    v = _layer_norm(acc_ref[...], g_ref[...], b_ref[...])
    v = v * _sigmoid(v)
    o_ref[0] = v.reshape(tt * NCHAIN, v.shape[-1]).astype(o_ref.dtype)


def _conv(u, prev_ok, next_ok, conv_w, conv_b, ln_g, ln_b, tt):
    _, ls, _, cc = u.shape
    width = conv_w.shape[0]
    assert (width - 1) // 2 <= CONV_HALO and tt % CONV_HALO == 0 and cc % LANES == 0
    nt = ls // tt
    hb = tt // CONV_HALO
    nh = ls // CONV_HALO
    wb = jnp.broadcast_to(conv_w[:, None, :], (width, NCHAIN, cc))
    row = lambda v: jnp.broadcast_to(v[None, :], (NCHAIN, cc))
    const2 = lambda g, t: (0, 0)
    return pl.pallas_call(
        functools.partial(_conv_kernel, width=width),
        grid=(2, nt),
        in_specs=[pl.BlockSpec((1, tt, NCHAIN, cc), lambda g, t: (g, t, 0, 0)),
                  pl.BlockSpec((1, CONV_HALO, NCHAIN, cc),
                               lambda g, t: (g, jnp.where(t == 0, nh - 1, t * hb - 1), 0, 0)),
                  pl.BlockSpec((1, CONV_HALO, NCHAIN, cc),
                               lambda g, t: (g, jnp.where(t == nt - 1, 0, (t + 1) * hb), 0, 0)),
                  pl.BlockSpec((1, NCHAIN, 1), lambda g, t: (g, 0, 0)),
                  pl.BlockSpec((1, NCHAIN, 1), lambda g, t: (g, 0, 0)),
                  pl.BlockSpec((width, NCHAIN, cc), lambda g, t: (0, 0, 0)),
                  pl.BlockSpec((NCHAIN, cc), const2), pl.BlockSpec((NCHAIN, cc), const2),
                  pl.BlockSpec((NCHAIN, cc), const2)],
        out_specs=pl.BlockSpec((1, tt * NCHAIN, cc), lambda g, t: (g, t, 0)),
        out_shape=jax.ShapeDtypeStruct((2, ls * NCHAIN, cc), BF16),
        scratch_shapes=[pltpu.VMEM((tt + 2 * CONV_HALO, NCHAIN, cc), F32),
                        pltpu.VMEM((tt, NCHAIN, cc), F32)],
        compiler_params=_cparams(("arbitrary", "arbitrary")),
        name="conv",
    )(u, u, u, prev_ok, next_ok, wb, row(conv_b), row(ln_g), row(ln_b))


S5_LANE_BLOCK = 512
S5_ROW_CHUNKS = {False: 2, True: 4}


def _s5_kernel(u_ref, wb_ref, wc_ref, are_ref, aim_ref, s0_ref, *rest, reverse, emit_y):
    if emit_y:
        y_ref, sfin_ref, bu_ref, st_ref, carry_ref = rest
    else:
        sfin_ref, bu_ref, carry_ref = rest
        st_ref = None
    i = pl.program_id(1)
    tt = u_ref.shape[1]
    nk, ks, sw2 = wb_ref.shape
    sw = sw2 // 2

    @pl.when(i == 0)
    def _():
        carry_ref[...] = s0_ref[0]

    u2 = u_ref[0].reshape(tt * NCHAIN, u_ref.shape[3]).astype(BF16)
    chunks = S5_ROW_CHUNKS[reverse]
    tc = tt // chunks
    order = range(chunks - 1, -1, -1) if reverse else range(chunks)
    for kc in range(nk):
        b = kc % 2
        for m in order:
            rows = slice(m * tc * NCHAIN, (m + 1) * tc * NCHAIN)
            bu_ref[b, m * tc:(m + 1) * tc] = _dot(u2[rows, kc * ks:(kc + 1) * ks],
                                                  wb_ref[kc]).reshape(tc, NCHAIN, sw2)
        for lo in range(0, sw, S5_LANE_BLOCK):
            lb = min(S5_LANE_BLOCK, sw - lo)
            re_sl = slice(lo, lo + lb)
            im_sl = slice(sw + lo, sw + lo + lb)
            ar = are_ref[kc, :, re_sl]
            ai = aim_ref[kc, :, re_sl]
            sre, sim = carry_ref[kc, :, re_sl], carry_ref[kc, :, im_sl]
            for jj in range(tt):
                tloc = tt - 1 - jj if reverse else jj
                nre = ar * sre - ai * sim + bu_ref[b, tloc, :, re_sl]
                nim = ar * sim + ai * sre + bu_ref[b, tloc, :, im_sl]
                if emit_y:
                    st_ref[b, tloc, :, re_sl] = nre
                    st_ref[b, tloc, :, im_sl] = nim
                sre, sim = nre, nim
            carry_ref[kc, :, re_sl] = sre
            carry_ref[kc, :, im_sl] = sim
        if emit_y:
            for m in order:
                st = st_ref[b, m * tc:(m + 1) * tc].reshape(tc * NCHAIN, sw2).astype(BF16)
                y_ref[0, m * tc:(m + 1) * tc, :, kc * ks:(kc + 1) * ks] = (
                    _dot(st, wc_ref[kc]).reshape(tc, NCHAIN, ks))
    sfin_ref[0] = carry_ref[...]


def _s5_pass(s_in, ngroups, wb, wc, are, aim, s0, tt, reverse, emit_y):
    _, ls, _, cs = s_in.shape
    nk, ks, sw2 = wb.shape
    nt = ls // tt
    tmap = (lambda i: nt - 1 - i) if reverse else (lambda i: i)
    state_spec = pl.BlockSpec((1, nk, NCHAIN, sw2), lambda g, i: (g, 0, 0, 0))
    out_specs = [state_spec]
    out_shape = [jax.ShapeDtypeStruct((ngroups, nk, NCHAIN, sw2), F32)]
    scratch = [pltpu.VMEM((2, tt, NCHAIN, sw2), F32)]
    if emit_y:
        out_specs.insert(0, pl.BlockSpec((1, tt, NCHAIN, cs), lambda g, i: (g, tmap(i), 0, 0)))
        out_shape.insert(0, jax.ShapeDtypeStruct((ngroups, ls, NCHAIN, cs), F32))
        scratch.append(pltpu.VMEM((2, tt, NCHAIN, sw2), F32))
    scratch.append(pltpu.VMEM((nk, NCHAIN, sw2), F32))
    res = pl.pallas_call(
        functools.partial(_s5_kernel, reverse=reverse, emit_y=emit_y),
        grid=(ngroups, nt),
        in_specs=[pl.BlockSpec((1, tt, NCHAIN, cs), lambda g, i: (g, tmap(i), 0, 0)),
                  _resident((nk, ks, sw2)), _resident((nk, sw2, ks)),
                  _resident((nk, NCHAIN, sw2 // 2)), _resident((nk, NCHAIN, sw2 // 2)),
                  state_spec],
        out_specs=out_specs,
        out_shape=out_shape,
        scratch_shapes=scratch,
        compiler_params=_cparams(("arbitrary", "arbitrary")),
        name="s5_" + ("bwd" if reverse else "fwd") + ("" if emit_y else "_state"),
    )(s_in, wb, wc, are, aim, s0)
    return (res[0], res[1]) if emit_y else (None, res[0])


def _cmul(are, aim, bre, bim):
    return are * bre - aim * bim, are * bim + aim * bre


def _cpow(re, im, n):
    rre, rim = jnp.ones_like(re), jnp.zeros_like(im)
    while n:
        if n & 1:
            rre, rim = _cmul(rre, rim, re, im)
        re, im = _cmul(re, im, re, im)
        n >>= 1
    return rre, rim


def _s5_params(b_re, b_im, a_re, a_im, log_dt, c_re, c_im):
    ng, ns, nh = b_re.shape
    cs = ng * nh
    ks = min(MXU_DIM, cs)
    gps = ks // nh
    nk = cs // ks
    sw = gps * ns
    a_re, a_im = a_re.astype(F32), a_im.astype(F32)
    dt = jnp.exp(log_dt.astype(F32))[:, None]
    mag = jnp.exp(a_re * dt)
    lre, lim = mag * jnp.cos(a_im * dt), mag * jnp.sin(a_im * dt)
    den = a_re * a_re + a_im * a_im
    zre = ((lre - 1.0) * a_re + lim * a_im) / den
    zim = (lim * a_re - (lre - 1.0) * a_im) / den
    cre, cim = _cmul(c_re.astype(F32), c_im.astype(F32), zre[:, None, :], zim[:, None, :])

    eye = jnp.eye(gps, dtype=F32)

    def in_block(b):
        b = b.astype(F32).reshape(nk, gps, ns, nh)
        return jnp.einsum("kgph,gj->kghjp", b, eye).reshape(nk, ks, sw)

    def out_block(c):
        c = c.reshape(nk, gps, nh, ns)
        return jnp.einsum("kghp,gj->kgpjh", c, eye).reshape(nk, sw, ks)

    wb = jnp.concatenate([in_block(b_re), in_block(b_im)], axis=2).astype(BF16)
    wc = jnp.concatenate([out_block(cre), out_block(-cim)], axis=1).astype(BF16)
    bc = lambda v: jnp.broadcast_to(v.reshape(nk, 1, sw), (nk, NCHAIN, sw))
    return dict(wb=wb, wc=wc, are=bc(lre), aim=bc(lim), lre=lre.reshape(nk, sw), lim=lim.reshape(nk, sw))


def _chain_states(local_end, p, seg, ls, reverse):
    nk, _, sw2 = local_end.shape
    sw = sw2 // 2
    pre, pim = _cpow(p["lre"], p["lim"], ls)
    e = local_end.reshape(nk, NCHAIN // seg, seg, sw2)
    ere, eim = e[..., :sw], e[..., sw:]
    zero = jnp.zeros_like(ere[:, :, 0])
    order = range(seg - 1, -1, -1) if reverse else range(seg)
    sre, sim = zero, zero
    out = [None] * seg
    for k in order:
        out[k] = jnp.concatenate([sre, sim], axis=-1)
        mre, mim = _cmul(pre[:, None, :], pim[:, None, :], sre, sim)
        sre, sim = mre + ere[:, :, k], mim + eim[:, :, k]
    return jnp.stack(out, axis=2).reshape(nk, NCHAIN, sw2)


def _s5_both(s_in, seg, pf, pb, tt):
    ls = s_in.shape[1]
    ys = []
    for p, reverse in ((pf, False), (pb, True)):
        zero = jnp.zeros((1, p["wb"].shape[0], NCHAIN, p["wb"].shape[2]), F32)
        s0 = zero
        if seg > 1:
            _, local_end = _s5_pass(s_in, 1, p["wb"], p["wc"], p["are"], p["aim"], zero, tt, reverse, False)
            s0 = _chain_states(local_end[0], p, seg, ls, reverse)[None]
        y, _ = _s5_pass(s_in, 2, p["wb"], p["wc"], p["are"], p["aim"], jnp.concatenate([s0, zero]), tt,
                        reverse, True)
        ys.append(y)
    return ys


def _glu_kernel(yf_ref, yb_ref, s_ref, d_ref, w_ref, b_ref, o_ref):
    y = yf_ref[...] + yb_ref[...] + d_ref[...] * s_ref[...]
    g = jax.nn.gelu(y)
    o_ref[...] = (g * _sigmoid(_dot(g.astype(BF16), w_ref[...]) + b_ref[...])).astype(o_ref.dtype)


def _glu(yf, yb, s_in, d_skip, glu_w, glu_b, tm):
    rows, cs = yf.shape
    rspec = pl.BlockSpec((tm, cs), lambda i: (i, 0))
    c2 = lambda i: (0, 0)
    return pl.pallas_call(
        _glu_kernel,
        grid=(rows // tm,),
        in_specs=[rspec, rspec, rspec, pl.BlockSpec((1, cs), c2), pl.BlockSpec((cs, cs), c2),
                  pl.BlockSpec((1, cs), c2)],
        out_specs=rspec,
        out_shape=jax.ShapeDtypeStruct((rows, cs), BF16),
        compiler_params=_cparams(("arbitrary",)),
        name="s5_glu",
    )(yf, yb, s_in, d_skip, glu_w, glu_b)


def _pack_halves(v):
    n = v.shape[-1] // 2
    bits = lax.bitcast_convert_type(v.astype(BF16).astype(F32), U32)
    return (bits[..., :n] >> 16) | (bits[..., n:] & jnp.uint32(0xFFFF0000))


def _unpack_halves(p):
    return (lax.bitcast_convert_type(p << 16, F32),
            lax.bitcast_convert_type(p & jnp.uint32(0xFFFF0000), F32))


OUTPROJ_SPLIT = 2


def _outproj_kernel(xp_ref, xs_ref, co_ref, so_ref, g1_ref, sc2_ref, sh2_ref, wo1_ref, wo2_ref, bo_ref,
                    lg_ref, lb_ref, rwh_ref, rwl_ref, x1_ref, hp_ref, lo_ref, *, alpha):
    g = pl.program_id(0)
    tt, d = xp_ref.shape[1], xp_ref.shape[2]
    nj = d // 2 // LANES
    th = tt // OUTPROJ_SPLIT
    for h in range(OUTPROJ_SPLIT):
        ts = slice(h * th, (h + 1) * th)
        rows = slice(h * th * NCHAIN, (h + 1) * th * NCHAIN)
        x = jnp.where(g == 0, xp_ref[:, ts, :], xs_ref[:, ts, :])
        mix = _dot(co_ref[0, rows, :], wo1_ref[...]) + _dot(so_ref[0, rows, :], wo2_ref[...]) + bo_ref[...]
        mix = pltpu.einshape("tcd->ctd", mix.reshape(th, NCHAIN, d))
        x1 = _layer_norm(alpha * x + g1_ref[0] * mix, lg_ref[...], lb_ref[...])
        h2 = x1 * (1.0 + sc2_ref[0]) + sh2_ref[0]
        x1_ref[0, :, ts, :] = x1
        hb = h2.astype(BF16)
        h2f = h2.reshape(NCHAIN * th, d)
        packed = _pack_halves(h2f)
        for j in range(nj):
            hp_ref[0, :, pl.ds(h * th * nj + j, th, stride=nj), :] = (
                packed[:, j * LANES:(j + 1) * LANES].reshape(NCHAIN, th, LANES))
        hi = hb.reshape(NCHAIN * th, d)
        lo = (h2f - hi.astype(F32)).astype(BF16)
        logits = _dot(hi, rwh_ref[...]) + (_dot(hi, rwl_ref[...]) + _dot(lo, rwh_ref[...]))
        lo_ref[0, :, ts, :] = logits.reshape(NCHAIN, th, logits.shape[-1])


def _outproj(xp, xs, conv_out, ssm_out, gate1, scale2, shift2, wo1, wo2, b_out, ln_g, ln_b, rw_hi, rw_lo,
             alpha, tt):
    _, ls, d = xp.shape
    cc, cs = wo1.shape[0], wo2.shape[0]
    ne = rw_hi.shape[1]
    nt = ls // tt
    nj = d // 2 // LANES
    xp_spec, xs_spec = _x_specs(tt, d, nt)
    c2 = lambda g, t: (0, 0)
    nat = lambda w: pl.BlockSpec((1, NCHAIN, tt, w), lambda g, t: (g, 0, t, 0))
    return pl.pallas_call(
        functools.partial(_outproj_kernel, alpha=alpha),
        grid=(2, nt),
        in_specs=[xp_spec, xs_spec,
                  pl.BlockSpec((1, tt * NCHAIN, cc), lambda g, t: (g, t, 0)),
                  pl.BlockSpec((1, tt * NCHAIN, cs), lambda g, t: (g, t, 0)),
                  _mod_spec(d), _mod_spec(d), _mod_spec(d),
                  _resident((cc, d)), _resident((cs, d)), pl.BlockSpec((1, d), c2),
                  pl.BlockSpec((1, d), c2), pl.BlockSpec((1, d), c2),
                  _resident((d, ne)), _resident((d, ne))],
        out_specs=[nat(d),
                   pl.BlockSpec((1, NCHAIN, tt * nj, LANES), lambda g, t: (g, 0, t, 0)),
                   nat(ne)],
        out_shape=[jax.ShapeDtypeStruct((2, NCHAIN, ls, d), F32),
                   jax.ShapeDtypeStruct((2, NCHAIN, ls * nj, LANES), U32),
                   jax.ShapeDtypeStruct((2, NCHAIN, ls, ne), F32)],
        compiler_params=_cparams(("arbitrary", "arbitrary")),
        name="outproj",
    )(xp, xs, conv_out, ssm_out, gate1, scale2, shift2, wo1, wo2, b_out, ln_g, ln_b, rw_hi, rw_lo)


def _route_kernel(lg_ref, bias_ref, eid_ref, e_ref, w_ref, r_ref, cnt_ref, sc_ref, carry_ref, *, span):
    i = pl.program_id(0)
    tm, ne = lg_ref.shape
    gs = ne // N_EXPERT_GROUPS
    neg = jnp.float32(-jnp.inf)

    @pl.when(i == 0)
    def _():
        carry_ref[...] = jnp.zeros_like(carry_ref)

    scores = _sigmoid(lg_ref[...]).T
    biased = scores + bias_ref[...]
    eid = eid_ref[...]

    def first_max(v, ids):
        m = jnp.max(v, axis=0, keepdims=True)
        idx = jnp.min(jnp.where(v == m, ids, float(ne)), axis=0, keepdims=True)
        return m, idx

    gscore = []
    for q in range(N_EXPERT_GROUPS):
        vg, ids = biased[q * gs:(q + 1) * gs], eid[q * gs:(q + 1) * gs]
        m1, i1 = first_max(vg, ids)
        m2 = jnp.max(jnp.where(ids == i1, neg, vg), axis=0, keepdims=True)
        gscore.append(m1 + m2)
    parts = []
    for q in range(N_EXPERT_GROUPS):
        beaten = jnp.zeros((1, tm), F32)
        for o in range(N_EXPERT_GROUPS):
            if o != q:
                wins = (gscore[o] >= gscore[q]) if o < q else (gscore[o] > gscore[q])
                beaten = beaten + wins.astype(F32)
        parts.append(jnp.where(beaten < TOPK_GROUPS, biased[q * gs:(q + 1) * gs], neg))
    masked = jnp.concatenate(parts, axis=0)

    chosen = jnp.zeros((ne, tm), F32)
    hits, e_rows, w_rows = [], [], []
    for k in range(TOP_K):
        _, idx = first_max(masked, eid)
        hit = eid == idx
        hits.append(hit)
        w_rows.append(jnp.sum(jnp.where(hit, scores, 0.0), axis=0, keepdims=True))
        e_rows.append(idx)
        masked = jnp.where(hit, neg, masked)
        chosen = jnp.where(hit, 1.0, chosen)
    w_t = jnp.concatenate(w_rows, axis=0)
    w_ref[...] = w_t / jnp.sum(w_t, axis=0, keepdims=True) * ROUTED_SCALE
    e_ref[...] = jnp.concatenate(e_rows, axis=0).astype(I32)

    rr = lax.broadcasted_iota(I32, (tm, tm), 0)
    cc = lax.broadcasted_iota(I32, (tm, tm), 1)
    before = (rr < cc).astype(BF16)
    prefix = _dot(chosen.astype(BF16), before) + carry_ref[...]
    r_rows = [jnp.sum(jnp.where(hits[k], prefix, 0.0), axis=0, keepdims=True) for k in range(TOP_K)]
    r_ref[...] = jnp.concatenate(r_rows, axis=0).astype(I32)
    carry_ref[...] = carry_ref[...] + jnp.sum(chosen, axis=1, keepdims=True)
    cnt_ref[...] = carry_ref[...]
    span_of = lax.broadcasted_iota(I32, (tm, LANES), 0) // span
    col = lax.broadcasted_iota(I32, (tm, LANES), 1)
    sc_ref[0] = _dot(chosen.astype(BF16), (span_of == col).astype(BF16))


def _route(logits, bias, tm, span):
    t, ne = logits.shape
    nb = tm // span
    assert tm % span == 0 and nb <= LANES
    kspec = pl.BlockSpec((TOP_K, tm), lambda i: (0, i))
    col = pl.BlockSpec((ne, 1), lambda i: (0, 0))
    eid = jnp.broadcast_to(jnp.arange(ne, dtype=F32)[:, None], (ne, tm))
    e_idx, e_w, rank, counts, span_cnt = pl.pallas_call(
        functools.partial(_route_kernel, span=span),
        grid=(t // tm,),
        in_specs=[pl.BlockSpec((tm, ne), lambda i: (i, 0)), col, _resident((ne, tm))],
        out_specs=[kspec, kspec, kspec, col, pl.BlockSpec((1, ne, LANES), lambda i: (i, 0, 0))],
        out_shape=[jax.ShapeDtypeStruct((TOP_K, t), I32), jax.ShapeDtypeStruct((TOP_K, t), F32),
                   jax.ShapeDtypeStruct((TOP_K, t), I32), jax.ShapeDtypeStruct((ne, 1), F32),
                   jax.ShapeDtypeStruct((t // tm, ne, LANES), F32)],
        scratch_shapes=[pltpu.VMEM((ne, 1), F32)],
        compiler_params=_cparams(("arbitrary",)),
        name="route",
    )(logits, bias.reshape(ne, 1), eid)
    span_cnt = span_cnt[:, :, :nb].transpose(0, 2, 1).reshape(t // span, ne).astype(I32)
    return e_idx, e_w, rank, counts, span_cnt


def _dest_kernel(e_ref, r_ref, ps_ref, eid_ref, o_ref):
    nb = ps_ref.shape[0]
    span = e_ref.shape[1] // nb
    for b in range(nb):
        lanes = slice(b * span, (b + 1) * span)
        e = e_ref[:, lanes].astype(F32)
        starts = [jnp.sum(jnp.where(eid_ref[:, lanes] == e[k:k + 1], ps_ref[b], 0.0), axis=0, keepdims=True)
                  for k in range(TOP_K)]
        o_ref[:, lanes] = r_ref[:, lanes] + jnp.concatenate(starts, axis=0).astype(I32)


def _dest(e_idx, rank, table, tm, span=None):
    t = e_idx.shape[1]
    ntab, ne, _ = table.shape
    nb = 1 if span is None else tm // span
    assert ntab == (1 if span is None else t // span)
    kspec = pl.BlockSpec((TOP_K, tm), lambda i: (0, i))
    tspec = pl.BlockSpec((nb, ne, 1), (lambda i: (0, 0, 0)) if span is None else (lambda i: (i, 0, 0)))
    eid = jnp.broadcast_to(jnp.arange(ne, dtype=F32)[:, None], (ne, tm))
    return pl.pallas_call(
        _dest_kernel,
        grid=(t // tm,),
        in_specs=[kspec, kspec, tspec, _resident((ne, tm))],
        out_specs=kspec,
        out_shape=jax.ShapeDtypeStruct((TOP_K, t), I32),
        compiler_params=_cparams(("arbitrary",)),
        name="dest" if span is None else "slot",
    )(e_idx, rank, table, eid)


def _dispatch_kernel(last_ref, nu_ref, nck_ref, src_ref, pos_ref, h_ref, sg_ref, su_ref, sd_ref, xs_ref, sh_ref,
                     zero_ref, stage_ref, sem, zsem, *, nj, tile_rows):
    i = pl.program_id(0)
    n = pl.num_programs(0)
    tm = h_ref.shape[0] // nj
    ne = last_ref.shape[0]
    max_chunks = src_ref.shape[0]
    slot = i % 2
    chunk_rows = COMBINE_CHUNK * nj

    @pl.when(i == 0)
    def _():
        zero_ref[...] = jnp.zeros_like(zero_ref)
        stage_ref[...] = jnp.zeros_like(stage_ref)

        def fill(row):
            start = pl.multiple_of(row * nj, 8)
            return pltpu.make_async_copy(zero_ref, xs_ref.at[pl.ds(start, tile_rows * nj)], zsem)

        def issue(e, c):
            @pl.when(last_ref[e] >= 0)
            def _():
                fill(last_ref[e]).start()
            return c

        def drain(e, c):
            @pl.when(last_ref[e] >= 0)
            def _():
                fill(last_ref[e]).wait()
            return c

        lax.fori_loop(0, ne, issue, 0)
        lax.fori_loop(0, ne, drain, 0)
        n_tiles = xs_ref.shape[0] // (tile_rows * nj)
        lax.fori_loop(nu_ref[0], n_tiles, lambda q, c: (fill(q * tile_rows).start(), c)[1], 0)
        lax.fori_loop(nu_ref[0], n_tiles, lambda q, c: (fill(q * tile_rows).wait(), c)[1], 0)

    def place(tt, c):
        for u in range(COMBINE_UNROLL):
            t = tt * COMBINE_UNROLL + u
            v = h_ref[pl.ds(pl.multiple_of(t * nj, nj), nj), :]
            for k in range(TOP_K):
                p = pos_ref[t * TOP_K + k]
                stage_ref[slot, pl.ds(pl.multiple_of(p * nj, nj), nj), :] = v
        return c

    lax.fori_loop(0, tm // COMBINE_UNROLL, place, 0)

    def chunk_copy(s, q):
        src = stage_ref.at[s, pl.ds(pl.multiple_of(q * chunk_rows, chunk_rows), chunk_rows)]
        dst = xs_ref.at[pl.ds(pl.multiple_of(src_ref[q] * nj, nj), chunk_rows)]
        return pltpu.make_async_copy(src, dst, sem.at[s])

    def wait_chunks(s, count):
        bit = 1
        while bit <= max_chunks:
            @pl.when((count & bit) != 0)
            def _(rows=bit * chunk_rows):
                pltpu.make_async_copy(stage_ref.at[s, pl.ds(0, rows)], xs_ref.at[pl.ds(0, rows)], sem.at[s]).wait()
            bit *= 2

    @pl.when(i > 0)
    def _():
        wait_chunks(1 - slot, nck_ref[jnp.maximum(i - 1, 0)])

    count = nck_ref[i]

    def issue(g, c):
        for u in range(COMBINE_ISSUE):
            chunk_copy(slot, g * COMBINE_ISSUE + u).start(priority=u % 2)
        return c

    lax.fori_loop(0, count // COMBINE_ISSUE, issue, 0)
    lax.fori_loop(count // COMBINE_ISSUE * COMBINE_ISSUE, count, lambda q, c: (chunk_copy(slot, q).start(), c)[1], 0)

    x = _unpack_rows(h_ref, tm, nj)
    gate = _dot(x, sg_ref[...])
    up = _dot(x, su_ref[...])
    sh_ref[...] = _dot((gate * _sigmoid(gate) * up).astype(BF16), sd_ref[...]).astype(sh_ref.dtype)

    @pl.when(i == n - 1)
    def _():
        wait_chunks(slot, count)


def _dispatch(last_tile_row, n_used, n_chunks, chunk_src, pos_flat, h_packed, sh_gate, sh_up, sh_down,
              n_rows, nj, tile_rows, tm):
    t = h_packed.shape[0] // nj
    n = t // tm
    d, ff = sh_gate.shape
    max_chunks = chunk_src.shape[0] // n
    smem = lambda size: pl.BlockSpec((size,), lambda i, *_: (i,), memory_space=pltpu.SMEM)
    grid_spec = pltpu.PrefetchScalarGridSpec(
        num_scalar_prefetch=3,
        grid=(n,),
        in_specs=[smem(max_chunks), smem(tm * TOP_K),
                  pl.BlockSpec((tm * nj, LANES), lambda i, *_: (i, 0)),
                  _resident((d, ff)), _resident((d, ff)), _resident((ff, d))],
        out_specs=[pl.BlockSpec(memory_space=pl.ANY), pl.BlockSpec((tm, d), lambda i, *_: (i, 0))],
        scratch_shapes=[pltpu.VMEM((tile_rows * nj, LANES), U32),
                        pltpu.VMEM((2, max_chunks * COMBINE_CHUNK * nj, LANES), U32),
                        pltpu.SemaphoreType.DMA((2,)), pltpu.SemaphoreType.DMA],
    )
    return pl.pallas_call(
        functools.partial(_dispatch_kernel, nj=nj, tile_rows=tile_rows),
        grid_spec=grid_spec,
        out_shape=[jax.ShapeDtypeStruct((n_rows * nj, LANES), U32), jax.ShapeDtypeStruct((t, d), BF16)],
        compiler_params=_cparams(("arbitrary",)),
        name="dispatch",
    )(last_tile_row, n_used, n_chunks, chunk_src, pos_flat, h_packed, sh_gate, sh_up, sh_down)


def _unpack_rows(p_ref, tm, nj):
    lo, hi = [], []
    for j in range(nj):
        l, h = _unpack_halves(p_ref[pl.ds(j, tm, stride=nj), :])
        lo.append(l.astype(BF16))
        hi.append(h.astype(BF16))
    return jnp.concatenate(lo + hi, axis=-1)


CAST_ROWS = 128


def _experts_kernel(te_ref, nu_ref, nx_ref, x_ref, wg_hbm, wu_hbm, wd_hbm, y_ref,
                    wgf_ref, wuf_ref, wdf_ref, wgb_ref, wub_ref, wdb_ref, sem, *, nj):
    i = pl.program_id(0)
    tm = x_ref.shape[0] // nj
    live = i < nu_ref[0]
    e = te_ref[i]
    first = live & ((i == 0) | (e != te_ref[jnp.maximum(i - 1, 0)]))

    def fetch(expert):
        return (pltpu.make_async_copy(wg_hbm.at[expert], wgf_ref, sem.at[0]),
                pltpu.make_async_copy(wu_hbm.at[expert], wuf_ref, sem.at[1]),
                pltpu.make_async_copy(wd_hbm.at[expert], wdf_ref, sem.at[2]))

    @pl.when(live & (i == 0))
    def _():
        for c in fetch(e):
            c.start()

    @pl.when(first)
    def _():
        for c in fetch(e):
            c.wait()
        for src, dst in ((wgf_ref, wgb_ref), (wuf_ref, wub_ref), (wdf_ref, wdb_ref)):
            for r in range(0, src.shape[0], CAST_ROWS):
                dst[r:r + CAST_ROWS] = src[r:r + CAST_ROWS].astype(BF16)

        @pl.when(nx_ref[i] >= 0)
        def _():
            for c in fetch(nx_ref[i]):
                c.start()

    @pl.when(live)
    def _():
        x = _unpack_rows(x_ref, tm, nj)
        gate = _dot(x, wgb_ref[...])
        up = _dot(x, wub_ref[...])
        act = (gate * _sigmoid(gate) * up).astype(BF16)
        y = _dot(act, wdb_ref[...])
        packed = _pack_halves(y)
        for j in range(nj):
            y_ref[pl.ds(j, tm, stride=nj), :] = packed[:, j * LANES:(j + 1) * LANES]

    @pl.when(i >= nu_ref[0])
    def _():
        y_ref[...] = jnp.zeros_like(y_ref)


def _experts(tile_e, n_used, next_e, x_sorted, w_gate, w_up, w_down, nj, tm):
    n_rows = x_sorted.shape[0] // nj
    ne, d, ff = w_gate.shape
    n_tiles = n_rows // tm
    hbm = pl.BlockSpec(memory_space=pl.ANY)
    grid_spec = pltpu.PrefetchScalarGridSpec(
        num_scalar_prefetch=3,
        grid=(n_tiles,),
        in_specs=[pl.BlockSpec((tm * nj, LANES), lambda i, te, nu, nx: (jnp.minimum(i, nu[0] - 1), 0)),
                  hbm, hbm, hbm],
        out_specs=pl.BlockSpec((tm * nj, LANES), lambda i, te, nu, nx: (i, 0)),
        scratch_shapes=[pltpu.VMEM((d, ff), F32), pltpu.VMEM((d, ff), F32), pltpu.VMEM((ff, d), F32),
                        pltpu.VMEM((d, ff), BF16), pltpu.VMEM((d, ff), BF16), pltpu.VMEM((ff, d), BF16),
                        pltpu.SemaphoreType.DMA((3,))],
    )
    return pl.pallas_call(
        functools.partial(_experts_kernel, nj=nj),
        grid_spec=grid_spec,
        out_shape=jax.ShapeDtypeStruct((n_rows * nj, LANES), U32),
        compiler_params=_cparams(("arbitrary",)),
        name="experts",
    )(tile_e, n_used, next_e, x_sorted, w_gate, w_up, w_down)


COMBINE_CHUNK = 8
COMBINE_UNROLL = 4
COMBINE_ISSUE = 4


def _combine_kernel(nck_ref, src_ref, srcn_ref, pos_ref, w_ref,
                    x1_ref, sh_ref, g2_ref, lg_ref, lb_ref, ys_ref, op_ref, os_ref,
                    buf_ref, alo_ref, ahi_ref, sem, *, alpha, half, nj):
    i = pl.program_id(0)
    n = pl.num_programs(0)
    tm, d = sh_ref.shape
    max_chunks = src_ref.shape[0]
    slot = i % 2
    chunk_rows = COMBINE_CHUNK * nj

    def fetch(s_ref, count, s):
        def start(g, c):
            for u in range(COMBINE_ISSUE):
                q = g * COMBINE_ISSUE + u
                src = ys_ref.at[pl.ds(pl.multiple_of(s_ref[q] * nj, nj), chunk_rows)]
                dst = buf_ref.at[s, pl.ds(pl.multiple_of(q * chunk_rows, chunk_rows), chunk_rows)]
                pltpu.make_async_copy(src, dst, sem.at[s]).start(priority=u % 2)
            return c
        lax.fori_loop(0, count // COMBINE_ISSUE, start, 0)

    rounded = lambda c: (c + COMBINE_ISSUE - 1) // COMBINE_ISSUE * COMBINE_ISSUE

    @pl.when(i == 0)
    def _():
        fetch(src_ref, rounded(nck_ref[0]), 0)

    @pl.when(i + 1 < n)
    def _():
        fetch(srcn_ref, rounded(nck_ref[i + 1]), 1 - slot)

    count = rounded(nck_ref[i])
    bit = COMBINE_ISSUE
    while bit <= max_chunks:
        @pl.when((count & bit) != 0)
        def _(rows=bit * chunk_rows):
            pltpu.make_async_copy(ys_ref.at[pl.ds(0, rows)], buf_ref.at[slot, pl.ds(0, rows)],
                                  sem.at[slot]).wait()
        bit *= 2

    def reduce_tokens(tt, c):
        for u in range(COMBINE_UNROLL):
            t = tt * COMBINE_UNROLL + u
            lo = hi = None
            for k in range(TOP_K):
                p = pos_ref[t * TOP_K + k]
                l, h = _unpack_halves(buf_ref[slot, pl.ds(pl.multiple_of(p * nj, nj), nj), :])
                wk = w_ref[t * TOP_K + k]
                lo = l * wk if k == 0 else lo + l * wk
                hi = h * wk if k == 0 else hi + h * wk
            alo_ref[pl.ds(pl.multiple_of(t * nj, nj), nj), :] = lo
            ahi_ref[pl.ds(pl.multiple_of(t * nj, nj), nj), :] = hi
        return c

    lax.fori_loop(0, tm // COMBINE_UNROLL, reduce_tokens, 0)
    routed = jnp.concatenate([alo_ref[pl.ds(j, tm, stride=nj), :] for j in range(nj)]
                             + [ahi_ref[pl.ds(j, tm, stride=nj), :] for j in range(nj)], axis=-1)
    ffn = routed + sh_ref[...].astype(F32)
    out = _layer_norm(alpha * x1_ref[...] + g2_ref[0] * ffn, lg_ref[...], lb_ref[...])

    @pl.when(i < half)
    def _():
        op_ref[...] = out

    @pl.when(i >= half)
    def _():
        os_ref[...] = out


def _combine(n_chunks, chunk_src, pos_flat, w_flat, x1, shared, gate2_rows, y_sorted, ln_g, ln_b, alpha, nj, tm, ls):
    t, d = shared.shape
    n = t // tm
    max_chunks = chunk_src.shape[0] // n
    half = n // 2
    per_chain = ls // tm
    assert tm % COMBINE_UNROLL == 0
    c2 = lambda i, nck: (0, 0)
    row = lambda i, nck: (i, 0)
    smem = lambda size, imap: pl.BlockSpec((size,), imap, memory_space=pltpu.SMEM)
    grid_spec = pltpu.PrefetchScalarGridSpec(
        num_scalar_prefetch=1,
        grid=(n,),
        in_specs=[smem(max_chunks, lambda i, nck: (i,)),
                  smem(max_chunks, lambda i, nck: (jnp.minimum(i + 1, n - 1),)),
                  smem(tm * TOP_K, lambda i, nck: (i,)), smem(tm * TOP_K, lambda i, nck: (i,)),
                  pl.BlockSpec((tm, d), row), pl.BlockSpec((tm, d), row),
                  pl.BlockSpec((1, 1, d), lambda i, nck: (i // per_chain, 0, 0)),
                  pl.BlockSpec((1, d), c2), pl.BlockSpec((1, d), c2),
                  pl.BlockSpec(memory_space=pl.ANY)],
        out_specs=[pl.BlockSpec((tm, d), lambda i, nck: (jnp.minimum(i, half - 1), 0)),
                   pl.BlockSpec((tm, d), lambda i, nck: (jnp.maximum(i - half, 0), 0))],
        scratch_shapes=[pltpu.VMEM((2, max_chunks * COMBINE_CHUNK * nj, LANES), U32),
                        pltpu.VMEM((tm * nj, LANES), F32), pltpu.VMEM((tm * nj, LANES), F32),
                        pltpu.SemaphoreType.DMA((2,))],
    )
    return pl.pallas_call(
        functools.partial(_combine_kernel, alpha=alpha, half=half, nj=nj),
        grid_spec=grid_spec,
        out_shape=[jax.ShapeDtypeStruct((t // 2, d), F32), jax.ShapeDtypeStruct((t // 2, d), F32)],
        compiler_params=_cparams(("arbitrary",)),
        name="combine",
    )(n_chunks, chunk_src, chunk_src, pos_flat, w_flat, x1, shared, gate2_rows, ln_g, ln_b, y_sorted)


def _tile(pref, n, mult=8):
    t = min(pref, n)
    while n % t or t % mult:
        t -= 1
    return t


def _encoder_layer(xp, xs, c_all, chain_seq, seg, alpha, p):
    _, ls, d = xp.shape
    cc = p["conv_w"].shape[-1]

    ada = _ada(c_all, p["w_ada"], p["b_ada"])
    mods = ada.reshape(ada.shape[0], N_ADA, d)[chain_seq].reshape(2, NCHAIN, N_ADA, 1, d)
    shift1, scale1, gate1, shift2, scale2, gate2 = (mods[:, :, k] for k in range(N_ADA))

    w_in = p["w_in"].astype(BF16)
    b_in = p["b_in"].reshape(1, -1)
    u, s_in = _inproj(xp, xs, scale1, shift1, w_in[:, :cc], w_in[:, cc:2 * cc], w_in[:, 2 * cc:],
                      b_in[:, :cc], b_in[:, cc:2 * cc], b_in[:, 2 * cc:], _tile(TILES["inproj"], ls))

    chain = np.arange(NCHAIN)
    prev_ok = jnp.asarray(np.stack([(chain % seg != 0), np.zeros(NCHAIN, bool)]).astype(np.float32)[..., None])
    next_ok = jnp.asarray(np.stack([(chain % seg != seg - 1), np.zeros(NCHAIN, bool)]).astype(np.float32)[..., None])
    conv_out = _conv(u, prev_ok, next_ok, p["conv_w"], p["conv_b"], p["conv_ln_g"], p["conv_ln_b"],
                     _tile(TILES["conv"], ls, CONV_HALO))

    pf = _s5_params(p["ssm_b_re"], p["ssm_b_im"], p["ssm_a_re_f"], p["ssm_a_im_f"], p["ssm_log_dt_f"],
                    p["ssm_c_re_f"], p["ssm_c_im_f"])
    pb = _s5_params(p["ssm_b_re"], p["ssm_b_im"], p["ssm_a_re_b"], p["ssm_a_im_b"], p["ssm_log_dt_b"],
                    p["ssm_c_re_b"], p["ssm_c_im_b"])
    yf, yb = _s5_both(s_in, seg, pf, pb, _tile(TILES["s5"], ls))
    cs = s_in.shape[-1]
    rows_tm = 2 * ls * NCHAIN
    flat = lambda a: a.reshape(rows_tm, cs)
    ssm_out = _glu(flat(yf), flat(yb), flat(s_in), p["ssm_d"].reshape(1, cs),
                   p["ssm_glu_w"].astype(BF16), p["ssm_glu_b"].reshape(1, cs), _tile(TILES["glu"], rows_tm))
    ssm_out = ssm_out.reshape(2, ls * NCHAIN, cs)

    w_out = p["w_out"].astype(BF16)
    rw = p["router_w"].astype(F32)
    rw_hi = rw.astype(BF16)
    rw_lo = (rw - rw_hi.astype(F32)).astype(BF16)
    row = lambda v: v.reshape(1, -1)
    x1, hp, logits = _outproj(xp, xs, conv_out, ssm_out, gate1, scale2, shift2, w_out[:cc], w_out[cc:],
                              row(p["b_out"]), row(p["ln1_g"]), row(p["ln1_b"]), rw_hi, rw_lo,
                              alpha, _tile(TILES["outproj"], ls, 16))

    t = 2 * NCHAIN * ls
    ne = rw.shape[1]
    nj = d // 2 // LANES
    tm_c = _tile(TILES["combine"], ls, COMBINE_UNROLL)
    tm_r = _tile(TILES["route"], t, LANES)
    e_idx, e_w, rank, counts, tile_cnt = _route(logits.reshape(t, ne), row(p["router_bias"]).astype(F32),
                                                tm_r, tm_c)

    tm_e = _tile(TILES["expert"], t)
    slack = COMBINE_CHUNK - 1
    n_rows = (t * TOP_K + ne * (tm_e + slack) + tm_e - 1) // tm_e * tm_e
    counts = counts.reshape(ne).astype(I32)
    padded = jnp.where(counts > 0, (counts + slack + tm_e - 1) // tm_e * tm_e, 0)
    pad_end = jnp.cumsum(padded)
    pad_start = pad_end - padded
    n_tiles = n_rows // tm_e
    tile_start = jnp.arange(n_tiles, dtype=I32)[:, None] * tm_e
    tile_e = jnp.minimum(jnp.sum((pad_end[None, :] <= tile_start).astype(I32), axis=1), ne - 1)
    n_used = (pad_end[-1:] // tm_e).astype(I32)
    pad_tile = (pad_start + counts) // tm_e * tm_e
    fill_rows = jnp.stack([jnp.where(counts > 0, pad_tile, -1),
                           jnp.where((counts > 0) & (pad_tile + tm_e < pad_end), pad_tile + tm_e, -1)],
                          axis=1).reshape(2 * ne).astype(I32)
    following = pad_end[tile_e] // tm_e
    next_e = jnp.where(following < n_used[0], tile_e[jnp.minimum(following, n_tiles - 1)], -1).astype(I32)

    before = jnp.cumsum(tile_cnt, axis=0) - tile_cnt
    run_src = pad_start[None, :] + before
    run_chunks = (tile_cnt + COMBINE_CHUNK - 1) // COMBINE_CHUNK
    chunks_through = jnp.cumsum(run_chunks, axis=1)
    chunks_before = chunks_through - run_chunks
    slot = _dest(e_idx, rank, (chunks_before * COMBINE_CHUNK - before).astype(F32)[:, :, None], tm_r, tm_c)
    slot = slot.T.reshape(t * TOP_K)
    max_chunks = ne + tm_c * TOP_K // COMBINE_CHUNK
    q = jnp.arange(max_chunks, dtype=I32)
    owner = jnp.sum((chunks_through[:, None, :] <= q[None, :, None]).astype(I32), axis=2)
    run_base = run_src - chunks_before * COMBINE_CHUNK
    chunk_src = jnp.sum(jnp.where(jnp.arange(ne, dtype=I32)[None, None, :] == owner[:, :, None],
                                  run_base[:, None, :], 0), axis=2) + q[None, :] * COMBINE_CHUNK

    n_chunks = chunks_through[:, -1].astype(I32)
    chunk_src = chunk_src.astype(I32).reshape(-1)

    x_sorted, shared = _dispatch(fill_rows, n_used, n_chunks, chunk_src, slot, hp.reshape(t * nj, LANES),
                                 p["sh_w_gate"].astype(BF16), p["sh_w_up"].astype(BF16),
                                 p["sh_w_down"].astype(BF16), n_rows, nj, tm_e, tm_c)
    y_sorted = _experts(tile_e, n_used, next_e, x_sorted, p["exp_w_gate"], p["exp_w_up"], p["exp_w_down"],
                        nj, tm_e)
    gate2_rows = gate2.reshape(2 * NCHAIN, 1, d)
    yp, ys = _combine(n_chunks, chunk_src, slot,
                      e_w.T.reshape(t * TOP_K), x1.reshape(t, d), shared, gate2_rows, y_sorted,
                      row(p["ln2_g"]), row(p["ln2_b"]), alpha, nj, tm_c, ls)
    return yp.reshape(NCHAIN, ls, d), ys.reshape(NCHAIN, ls, d)


_PARAM_NAMES = ("w_ada", "b_ada", "w_in", "b_in", "conv_w", "conv_b", "conv_ln_g", "conv_ln_b",
                "ssm_b_re", "ssm_b_im", "ssm_a_re_f", "ssm_a_im_f", "ssm_log_dt_f",
                "ssm_a_re_b", "ssm_a_im_b", "ssm_log_dt_b", "ssm_c_re_f", "ssm_c_im_f",
                "ssm_c_re_b", "ssm_c_im_b", "ssm_d", "ssm_glu_w", "ssm_glu_b", "w_out", "b_out",
                "ln1_g", "ln1_b", "router_w", "router_bias", "exp_w_gate", "exp_w_up", "exp_w_down",
                "sh_w_gate", "sh_w_up", "sh_w_down", "ln2_g", "ln2_b")


def kernel(x_prompt, x_sample, c_prompt, c_sample, w_ada, b_ada, w_in, b_in, conv_w, conv_b, conv_ln_g, conv_ln_b, ssm_b_re, ssm_b_im, ssm_a_re_f, ssm_a_im_f, ssm_log_dt_f, ssm_a_re_b, ssm_a_im_b, ssm_log_dt_b, ssm_c_re_f, ssm_c_im_f, ssm_c_re_b, ssm_c_im_b, ssm_d, ssm_glu_w, ssm_glu_b, w_out, b_out, ln1_g, ln1_b, router_w, router_bias, exp_w_gate, exp_w_up, exp_w_down, sh_w_gate, sh_w_up, sh_w_down, ln2_g, ln2_b):
    stacked = (w_ada, b_ada, w_in, b_in, conv_w, conv_b, conv_ln_g, conv_ln_b, ssm_b_re, ssm_b_im,
               ssm_a_re_f, ssm_a_im_f, ssm_log_dt_f, ssm_a_re_b, ssm_a_im_b, ssm_log_dt_b, ssm_c_re_f,
               ssm_c_im_f, ssm_c_re_b, ssm_c_im_b, ssm_d, ssm_glu_w, ssm_glu_b, w_out, b_out, ln1_g, ln1_b,
               router_w, router_bias, exp_w_gate, exp_w_up, exp_w_down, sh_w_gate, sh_w_up, sh_w_down,
               ln2_g, ln2_b)
    depth = w_ada.shape[0]
    alpha = (2 * depth) ** 0.25
    bp, lp, d = x_prompt.shape
    bs, lsample, _ = x_sample.shape
    assert NCHAIN % bp == 0 and bs == NCHAIN and lp % (NCHAIN // bp) == 0
    seg = NCHAIN // bp
    ls = lp // seg
    assert ls == lsample

    c_all = jnp.concatenate([c_prompt, c_sample], axis=0)
    c_all = jnp.pad(c_all, ((0, -c_all.shape[0] % 8), (0, 0)))
    chain_seq = np.concatenate([np.arange(NCHAIN) // seg, bp + np.arange(NCHAIN)])

    xp = x_prompt.reshape(NCHAIN, ls, d)
    xs = x_sample.reshape(NCHAIN, ls, d)
    for l in range(depth):
        params = {n: v[l] for n, v in zip(_PARAM_NAMES, stacked)}
        xp, xs = _encoder_layer(xp, xs, c_all, chain_seq, seg, alpha, params)
    return xp.reshape(bp, lp, d), xs.reshape(bs, lsample, d)
```
